```python
import jax, jax.numpy as jnp
from jax import lax
import numpy as np

D_MODEL = 2048
BATCH = 1
SEQ = 8192
DEPTH = 1
DEC_BATCH = 32
DEC_SEQ = 16
PAST_LEN = 1024

CHUNK = 64
RW_HEAD = 64
RW_HEADS = D_MODEL // RW_HEAD
DECAY_LORA = 96
AAA_LORA = 96
GATE_LORA = 256
DECAY_SCALE = 0.6065306597126334
RW_GN_EPS = RW_HEAD * 1e-5
CONV_DIM = D_MODEL // 2
CONV_W = 3
N_MEM = 256
MEM_HEADS = 4
MEM_HEAD_DIM = 256
MEM_DIM = MEM_HEADS * MEM_HEAD_DIM
N_EXPERTS = 64
N_GROUPS = 8
TOPK_GROUPS = 4
TOP_K = 8
EXPERT_FF = 512
SHARED_FF = 512
ROUTED_SCALE = 2.5
EXPERT_BLOCK = 128
LN_EPS = 1e-5
ALPHA = (2 * DEPTH) ** 0.25
BETA = (8 * DEPTH) ** -0.25
RW_COLS = 3 * D_MODEL + DECAY_LORA + AAA_LORA + GATE_LORA
IN_COLS = RW_COLS + 3 * CONV_DIM + MEM_DIM + 3 * D_MODEL

kernel_name = 'hybrid_rwkv7_shortconv_memxattn_moe_step'


def split_cols(p, sizes):
    offsets = np.cumsum(np.asarray(sizes))[:-1].tolist()
    return jnp.split(p, offsets, axis=-1)


def layer_norm(x, g, b, eps):
    xf = x.astype(jnp.float32)
    mu = jnp.mean(xf, -1, keepdims=True)
    var = jnp.mean(jnp.square(xf - mu), -1, keepdims=True)
    return ((xf - mu) * lax.rsqrt(var + eps) * g + b).astype(x.dtype)


def rwkv7_scan(r, log_w, k, v, kk, a, s0):
    xs = tuple(jnp.moveaxis(t.astype(jnp.float32), 1, 0) for t in (r, log_w, k, v, kk, a))

    def step(S, inp):
        r_t, lw_t, k_t, v_t, kk_t, a_t = inp
        s_kk = jnp.einsum('bhij,bhj->bhi', S, kk_t)
        S = (S * jnp.exp(lw_t)[:, :, None, :]
             - s_kk[..., :, None] * (kk_t * a_t)[..., None, :]
             + v_t[..., :, None] * k_t[..., None, :])
        return S, jnp.einsum('bhij,bhj->bhi', S, r_t)

    s_last, ys = lax.scan(step, s0.astype(jnp.float32), xs)
    return s_last, jnp.moveaxis(ys, 0, 1)


def rwkv7_branch(r, k, v, w_lo, a_lo, g_lo, s0, w0, w2, a0, a2, g2, k_k, k_a, r_k, gn_g, gn_b):
    B, T, D = r.shape

    def heads(t):
        return t.reshape(B, T, RW_HEADS, RW_HEAD)

    log_w = -DECAY_SCALE * jax.nn.sigmoid((w0 + jnp.tanh(w_lo) @ w2).astype(jnp.float32))
    a = jax.nn.sigmoid(a0 + a_lo @ a2)
    g = jax.nn.sigmoid(g_lo) @ g2
    kk = heads(k * k_k).astype(jnp.float32)
    kk = kk * lax.rsqrt(jnp.maximum(jnp.sum(kk * kk, -1, keepdims=True), 1e-24))
    k = k * (1.0 + (a - 1.0) * k_a)
    rh, kh, vh = heads(r), heads(k), heads(v)
    s_last, y = rwkv7_scan(rh, heads(log_w), kh, vh, kk, heads(a), s0)
    y = layer_norm(y, gn_g.reshape(RW_HEADS, RW_HEAD), gn_b.reshape(RW_HEADS, RW_HEAD), RW_GN_EPS)
    bonus = jnp.sum((rh * kh * r_k).astype(jnp.float32), -1, keepdims=True) * vh.astype(jnp.float32)
    out = ((y + bonus).reshape(B, T, D) * g).astype(r.dtype)
    return out, s_last.astype(s0.dtype)


def short_conv_branch(c_b, c_c, c_h, conv_prev, conv_w, w_conv_out):
    T = c_h.shape[1]
    u = c_c * c_h
    u_ext = jnp.concatenate([conv_prev.astype(u.dtype), u], axis=1)
    conv = sum(u_ext[:, j:j + T] * conv_w[j] for j in range(CONV_W))
    return (c_b * conv) @ w_conv_out, u_ext[:, -(CONV_W - 1):]


def memory_branch(q, mem_k, mem_v, w_mem_o):
    B, T, _ = q.shape
    qh = q.reshape(B, T, MEM_HEADS, MEM_HEAD_DIM)
    s = jnp.einsum('bthd,bmhd->bhtm', qh, mem_k.astype(q.dtype)).astype(jnp.float32) * MEM_HEAD_DIM ** -0.5
    pr = jax.nn.softmax(s, axis=-1).astype(q.dtype)
    o = jnp.einsum('bhtm,bmhd->bthd', pr, mem_v.astype(q.dtype))
    return o.reshape(B, T, MEM_DIM) @ w_mem_o


def swiglu(x, w_up, w_down):
    gate, up = split_cols(x @ w_up, (w_up.shape[-1] // 2, w_up.shape[-1] // 2))
    return (jax.nn.silu(gate) * up) @ w_down


def route(x2d, w_router, router_bias):
    T = x2d.shape[0]
    scores = jax.nn.sigmoid((x2d @ w_router).astype(jnp.float32))
    choice = scores + router_bias.astype(jnp.float32)
    grp = choice.reshape(T, N_GROUPS, N_EXPERTS // N_GROUPS)
    grp_score = jnp.sum(lax.top_k(grp, 2)[0], -1)
    _, top_g = lax.top_k(grp_score, TOPK_GROUPS)
    gmask = jnp.sum(jax.nn.one_hot(top_g, N_GROUPS, dtype=jnp.float32), 1)
    emask = jnp.repeat(gmask, N_EXPERTS // N_GROUPS, axis=1)
    _, idx = lax.top_k(jnp.where(emask > 0, choice, -jnp.inf), TOP_K)
    w = jnp.take_along_axis(scores, idx, axis=1)
    w = w / jnp.sum(w, -1, keepdims=True) * ROUTED_SCALE
    return idx, w


def routed_experts(x2d, idx, wts, w_exp_up, w_exp_down):
    T, D = x2d.shape
    A = T * TOP_K
    nb = -(-A // EXPERT_BLOCK) + N_EXPERTS
    flat_e = idx.reshape(A)
    order = jnp.argsort(flat_e)
    e_sorted = flat_e[order]
    counts = jnp.bincount(flat_e, length=N_EXPERTS)
    padded = (counts + EXPERT_BLOCK - 1) // EXPERT_BLOCK * EXPERT_BLOCK
    seg_end = jnp.cumsum(padded)
    seg_start = seg_end - padded
    first = jnp.cumsum(counts) - counts
    dest = seg_start[e_sorted] + jnp.arange(A) - first[e_sorted]
    row_token = jnp.full((nb * EXPERT_BLOCK,), T, dtype=jnp.int32).at[dest].set((order // TOP_K).astype(jnp.int32))
    block_e = jnp.minimum(jnp.searchsorted(seg_end, jnp.arange(nb) * EXPERT_BLOCK, side='right'), N_EXPERTS - 1)
    x_pad = jnp.concatenate([x2d, jnp.zeros((1, D), x2d.dtype)], axis=0)
    xb = x_pad[row_token].reshape(nb, EXPERT_BLOCK, D)

    def one_block(args):
        x_blk, e = args
        return swiglu(x_blk, w_exp_up[e], w_exp_down[e])

    yb = lax.map(one_block, (xb, block_e)).reshape(nb * EXPERT_BLOCK, D)
    y_assign = jnp.zeros((A, D), yb.dtype).at[order].set(yb[dest])
    return jnp.einsum('tkd,tk->td', y_assign.reshape(T, TOP_K, D), wts.astype(yb.dtype))


def moe_ffn(x2d, w_router, router_bias, w_exp_up, w_exp_down, w_sh_up, w_sh_down):
    idx, wts = route(x2d, w_router, router_bias)
    return swiglu(x2d, w_sh_up, w_sh_down) + routed_experts(x2d, idx, wts, w_exp_up, w_exp_down)


def trunk_layer(x, shift_prev, conv_prev, s0, mem_k, mem_v, p):
    (w_in, mu_shift, rw_w0, rw_w2, rw_a0, rw_a2, rw_g2, rw_k_k, rw_k_a, rw_r_k, rw_gn_g, rw_gn_b,
     conv_w, w_conv_out, w_mem_o, w_o, ln1_g, ln1_b, w_router, router_bias,
     w_exp_up, w_exp_down, w_sh_up, w_sh_down, ln2_g, ln2_b) = p
    B, T, D = x.shape
    proj = x @ w_in
    p_rw, p_cv, p_q, p_gate = split_cols(proj, (RW_COLS, 3 * CONV_DIM, MEM_DIM, 3 * D_MODEL))
    prev = jnp.concatenate([shift_prev.astype(p_rw.dtype), p_rw[:, :-1]], axis=1)
    p_mix = p_rw + (prev - p_rw) * mu_shift
    r, k, v, w_lo, a_lo, g_lo = split_cols(p_mix, (D_MODEL, D_MODEL, D_MODEL, DECAY_LORA, AAA_LORA, GATE_LORA))
    o_rw, s_new = rwkv7_branch(r, k, v, w_lo, a_lo, g_lo, s0, rw_w0, rw_w2, rw_a0, rw_a2, rw_g2,
                               rw_k_k, rw_k_a, rw_r_k, rw_gn_g, rw_gn_b)
    c_b, c_c, c_h = split_cols(p_cv, (CONV_DIM, CONV_DIM, CONV_DIM))
    o_cv, conv_new = short_conv_branch(c_b, c_c, c_h, conv_prev, conv_w, w_conv_out)
    o_mem = memory_branch(p_q, mem_k, mem_v, w_mem_o)
    gates = jax.nn.sigmoid(p_gate).reshape(B, T, 3, D)
    merged = gates[:, :, 0] * o_rw + gates[:, :, 1] * o_cv + gates[:, :, 2] * o_mem
    h = layer_norm(ALPHA * x + merged @ w_o, ln1_g, ln1_b, LN_EPS)
    f = moe_ffn(h.reshape(B * T, D), w_router, router_bias, w_exp_up, w_exp_down, w_sh_up, w_sh_down)
    y = layer_norm(ALPHA * h + f.reshape(B, T, D), ln2_g, ln2_b, LN_EPS)
    return y, s_new, p_rw[:, -1:], conv_new


def setup_inputs(seed: int = 0) -> dict:
    key = jax.random.key(seed)
    ks = jax.random.split(key, 36)
    L, D = DEPTH, D_MODEL

    def nrm(k, shape, scale):
        return jax.random.normal(k, shape, jnp.float32) * scale

    def gain(k, shape):
        return 1.0 + nrm(k, shape, 0.05)

    conv_keep = min(CONV_W - 1, PAST_LEN)
    mem_shape = (L, DEC_BATCH, N_MEM, MEM_HEADS, MEM_HEAD_DIM)
    return {
        'x_prompt': nrm(ks[0], (BATCH, SEQ, D), 1.0),
        'x_sample': nrm(ks[1], (DEC_BATCH, DEC_SEQ, D), 1.0),
        'mem_prompt': nrm(ks[2], (BATCH, N_MEM, D), 1.0),
        'state_rwkv': nrm(ks[3], (L, DEC_BATCH, RW_HEADS, RW_HEAD, RW_HEAD), 0.5),
        'state_shift': nrm(ks[4], (L, DEC_BATCH, 1, RW_COLS), 1.0),
        'state_conv': nrm(ks[5], (L, DEC_BATCH, conv_keep, CONV_DIM), 1.0),
        'cache_mem_k': nrm(ks[6], mem_shape, 1.0),
        'cache_mem_v': nrm(ks[7], mem_shape, 1.0),
        'w_in': nrm(ks[8], (L, D, IN_COLS), D ** -0.5),
        'mu_shift': jax.random.uniform(ks[9], (L, RW_COLS), jnp.float32),
        'rw_w0': nrm(ks[10], (L, D), 0.5),
        'rw_w2': nrm(ks[11], (L, DECAY_LORA, D), 0.1 * DECAY_LORA ** -0.5),
        'rw_a0': nrm(ks[12], (L, D), 0.5),
        'rw_a2': nrm(ks[13], (L, AAA_LORA, D), 0.5 * AAA_LORA ** -0.5),
        'rw_g2': nrm(ks[14], (L, GATE_LORA, D), GATE_LORA ** -0.5),
        'rw_k_k': 0.85 + nrm(ks[15], (L, D), 0.05),
        'rw_k_a': gain(ks[16], (L, D)),
        'rw_r_k': nrm(ks[17], (L, RW_HEADS, RW_HEAD), 0.1),
        'rw_gn_g': gain(ks[18], (L, D)),
        'rw_gn_b': nrm(ks[19], (L, D), 0.01),
        'conv_w': nrm(ks[20], (L, CONV_W, CONV_DIM), CONV_W ** -0.5),
        'w_conv_out': nrm(ks[21], (L, CONV_DIM, D), CONV_DIM ** -0.5),
        'w_mem_k': nrm(ks[22], (L, D, MEM_DIM), D ** -0.5),
        'w_mem_v': nrm(ks[23], (L, D, MEM_DIM), D ** -0.5),
        'w_mem_o': nrm(ks[24], (L, MEM_DIM, D), MEM_DIM ** -0.5),
        'w_o': nrm(ks[25], (L, D, D), BETA * D ** -0.5),
        'ln1_g': gain(ks[26], (L, D)),
        'ln1_b': nrm(ks[27], (L, D), 0.01),
        'w_router': nrm(ks[28], (L, D, N_EXPERTS), D ** -0.5),
        'router_bias': nrm(ks[29], (L, N_EXPERTS), 0.01),
        'w_exp_up': nrm(ks[30], (L, N_EXPERTS, D, 2 * EXPERT_FF), D ** -0.5),
        'w_exp_down': nrm(ks[31], (L, N_EXPERTS, EXPERT_FF, D), BETA * EXPERT_FF ** -0.5),
        'w_sh_up': nrm(ks[32], (L, D, 2 * SHARED_FF), D ** -0.5),
        'w_sh_down': nrm(ks[33], (L, SHARED_FF, D), BETA * SHARED_FF ** -0.5),
        'ln2_g': gain(ks[34], (L, D)),
        'ln2_b': nrm(ks[35], (L, D), 0.01),
    }


def reference(x_prompt, x_sample, mem_prompt, state_rwkv, state_shift, state_conv, cache_mem_k, cache_mem_v,
              w_in, mu_shift, rw_w0, rw_w2, rw_a0, rw_a2, rw_g2, rw_k_k, rw_k_a, rw_r_k, rw_gn_g, rw_gn_b,
              conv_w, w_conv_out, w_mem_k, w_mem_v, w_mem_o, w_o, ln1_g, ln1_b, w_router, router_bias,
              w_exp_up, w_exp_down, w_sh_up, w_sh_down, ln2_g, ln2_b):
    assert x_sample.shape[1] <= CHUNK
    B = x_prompt.shape[0]
    dt = x_prompt.dtype
    layer_weights = (w_in, mu_shift, rw_w0, rw_w2, rw_a0, rw_a2, rw_g2, rw_k_k, rw_k_a, rw_r_k, rw_gn_g, rw_gn_b,
                     conv_w, w_conv_out, w_mem_o, w_o, ln1_g, ln1_b, w_router, router_bias,
                     w_exp_up, w_exp_down, w_sh_up, w_sh_down, ln2_g, ln2_b)
    y_p, y_s = x_prompt, x_sample
    rw_p, sh_p, cv_p, mk_p, mv_p, rw_s, sh_s, cv_s = [], [], [], [], [], [], [], []
    for l in range(DEPTH):
        p = tuple(w[l] for w in layer_weights)
        mk = (mem_prompt @ w_mem_k[l]).reshape(B, N_MEM, MEM_HEADS, MEM_HEAD_DIM)
        mv = (mem_prompt @ w_mem_v[l]).reshape(B, N_MEM, MEM_HEADS, MEM_HEAD_DIM)
        y_p, s1, s2, s3 = trunk_layer(
            y_p, jnp.zeros((B, 1, RW_COLS), dt), jnp.zeros((B, CONV_W - 1, CONV_DIM), dt),
            jnp.zeros((B, RW_HEADS, RW_HEAD, RW_HEAD), dt), mk, mv, p)
        y_s, t1, t2, t3 = trunk_layer(y_s, state_shift[l], state_conv[l], state_rwkv[l],
                                      cache_mem_k[l], cache_mem_v[l], p)
        rw_p.append(s1); sh_p.append(s2); cv_p.append(s3); mk_p.append(mk); mv_p.append(mv)
        rw_s.append(t1); sh_s.append(t2); cv_s.append(t3)
    return (y_p, y_s, jnp.stack(rw_p), jnp.stack(sh_p), jnp.stack(cv_p), jnp.stack(mk_p), jnp.stack(mv_p),
            jnp.stack(rw_s), jnp.stack(sh_s), jnp.stack(cv_s))
```

```python
import functools

import jax
import jax.numpy as jnp
from jax import lax
from jax.experimental import pallas as pl
from jax.experimental.pallas import tpu as pltpu

F32 = jnp.float32
BF16 = jnp.bfloat16

D = 2048
HEAD = 64
N_HEADS = D // HEAD
LORA_W = 96
LORA_A = 96
LORA_G = 256
DECAY_SCALE = 0.6065306597126334
GN_EPS = HEAD * 1e-5
CONV_DIM = D // 2
N_MEM = 256
MEM_HEADS = 4
MEM_HEAD_DIM = 256
MEM_DIM = MEM_HEADS * MEM_HEAD_DIM
N_EXPERTS = 64
N_GROUPS = 8
GROUP_SIZE = N_EXPERTS // N_GROUPS
TOPK_GROUPS = 4
TOP_K = 8
EXPERT_FF = 512
SHARED_FF = 512
ROUTED_SCALE = 2.5
LN_EPS = 1e-5
DEPTH = 1
ALPHA = (2 * DEPTH) ** 0.25
RW_COLS = 3 * D + LORA_W + LORA_A + LORA_G

LORA_PAD = 128
LORA_COLS = 2 * LORA_PAD + LORA_G
COL_RKV = 0
COL_GATE = 3 * D
COL_CONV = 6 * D
COL_Q = COL_CONV + 3 * CONV_DIM
COL_LORA = COL_Q + MEM_DIM
P_COLS = COL_LORA + LORA_COLS

CHUNK = 16
GROUP_HEADS = 4
GROUP_LANES = GROUP_HEADS * HEAD
N_LANE_GROUPS = D // GROUP_LANES
STACK = GROUP_HEADS * CHUNK
SEQ_S = 16
STATE_SLOTS = 8

TR = 128
SCAN_ROWS = STATE_SLOTS * CHUNK
EXPERT_BM = 256
VMEM_LIMIT = 56 * 1024 * 1024


def _cparams(sem):
    return pltpu.CompilerParams(dimension_semantics=sem, vmem_limit_bytes=VMEM_LIMIT)


def _sigmoid(x):
    return 1.0 / (1.0 + jnp.exp(-x))


def _dot(a, b, dims=(((1,), (0,)), ((), ()))):
    return lax.dot_general(a.astype(BF16), b.astype(BF16), dims, preferred_element_type=F32)


_NN = (((1,), (0,)), ((), ()))
_NT = (((1,), (1,)), ((), ()))
_TN = (((0,), (0,)), ((), ()))


def _split2(x):
    hi = x.astype(BF16)
    lo = (x - hi.astype(F32)).astype(BF16)
    return hi, lo


def _split3(x):
    hi = x.astype(BF16)
    r1 = x - hi.astype(F32)
    mid = r1.astype(BF16)
    lo = (r1 - mid.astype(F32)).astype(BF16)
    return hi, mid, lo


def _dot3(a, b, dims=_NN):
    ah, al = _split2(a)
    bh, bl = _split2(b)
    f = functools.partial(lax.dot_general, dimension_numbers=dims, preferred_element_type=F32)
    return f(ah, bh) + (f(ah, bl) + f(al, bh))


def _dot_exact_rhs(a, b_bf16, dims=_NN):
    hi, mid, lo = _split3(a)
    f = functools.partial(lax.dot_general, dimension_numbers=dims, preferred_element_type=F32)
    return f(hi, b_bf16) + (f(mid, b_bf16) + f(lo, b_bf16))


def _dot_exact_lhs(a_bf16, b):
    hi, mid, lo = _split3(b)
    f = functools.partial(lax.dot_general, dimension_numbers=_NN, preferred_element_type=F32)
    return f(a_bf16, hi) + (f(a_bf16, mid) + f(a_bf16, lo))


def _mm_kernel(x_ref, w_ref, o_ref):
    o_ref[...] = _dot(x_ref[...], w_ref[...]).astype(o_ref.dtype)


def _matmul(x, w, tm, tn, name):
    m, k = x.shape
    n = w.shape[1]
    return pl.pallas_call(
        _mm_kernel,
        grid=(m // tm, n // tn),
        in_specs=[pl.BlockSpec((tm, k), lambda i, j: (i, 0)),
                  pl.BlockSpec((k, tn), lambda i, j: (0, j))],
        out_specs=pl.BlockSpec((tm, tn), lambda i, j: (i, j)),
        out_shape=jax.ShapeDtypeStruct((m, n), F32),
        compiler_params=_cparams(("parallel", "arbitrary")),
        name=name,
    )(x, w)


def _head_sum(x, bd):
    parts = []
    for g in range(N_LANE_GROUPS):
        parts.append(_dot_exact_rhs(x[:, g * GROUP_LANES:(g + 1) * GROUP_LANES], bd))
    return jnp.concatenate(parts, axis=1)


def _prep_kernel(n_prompt_tiles, rkv_ref, lora_ref, c_rkv_ref, c_lora_ref, b_rkv_ref, b_lora_ref,
                 mu_rkv_ref, mu_lora_ref, w0_ref, a0_ref, kk_ref, ka_ref, rk_ref,
                 w2_ref, a2_ref, g2_ref, bd_ref,
                 r_o, k_o, v_o, kk_o, b_o, lw_o, g_o, bonus_o):
    i = pl.program_id(0)
    rows = rkv_ref.shape[0]
    row = lax.broadcasted_iota(jnp.int32, (rows, 1), 0)
    is_sample = i >= n_prompt_tiles
    seq_start = jnp.logical_and(is_sample, (row % SEQ_S) == 0)

    def mixed(x, carry_row, bnd, mu):
        prev = pltpu.roll(x, 1, 0)
        carry_row = jnp.where(i == 0, 0.0, carry_row)
        prev = jnp.where(row == 0, carry_row, prev)
        prev = jnp.where(seq_start, bnd, prev)
        return x + (prev - x) * mu

    def section(s):
        sl = slice(s * D, (s + 1) * D)
        return mixed(rkv_ref[:, sl], c_rkv_ref[7:8, sl], b_rkv_ref[:, sl], mu_rkv_ref[:, sl])

    lo = mixed(lora_ref[...], c_lora_ref[7:8, :], b_lora_ref[...], mu_lora_ref[...])
    w_lo = lo[:, 0:LORA_PAD]
    a_lo = lo[:, LORA_PAD:2 * LORA_PAD]
    g_lo = lo[:, 2 * LORA_PAD:]
    log_w = -DECAY_SCALE * _sigmoid(w0_ref[...] + _dot(jnp.tanh(w_lo), w2_ref[...]))
    a = _sigmoid(a0_ref[...] + _dot(a_lo, a2_ref[...]))
    g_o[...] = _dot(_sigmoid(g_lo), g2_ref[...])
    lw_o[...] = log_w

    bd = bd_ref[...]
    k = section(1)
    kk = k * kk_ref[...]
    ss = _head_sum(kk * kk, bd)
    kk = kk * lax.rsqrt(jnp.maximum(ss, 1e-24))
    kk_o[...] = kk
    b_o[...] = kk * a
    k = k * (1.0 + (a - 1.0) * ka_ref[...])
    k_o[...] = k
    r = section(0)
    r_o[...] = r
    v = section(2)
    v_o[...] = v
    bonus_o[...] = _head_sum(r * k * rk_ref[...], bd) * v


def _rwkv_prep(p, bnd_rkv, bnd_lora, mu_rkv, mu_lora, w0, a0, k_k, k_a, r_k, w2p, a2p, g2, bd, n_prompt):
    n = p.shape[0]
    n_prompt_tiles = n_prompt // TR
    carry_blk = TR // 8
    lora_blk = COL_LORA // LORA_COLS

    def row_spec(cols, cb=0):
        return pl.BlockSpec((TR, cols), lambda i: (i, cb))

    def carry_spec(cols, cb=0):
        return pl.BlockSpec((8, cols), lambda i: (jnp.maximum(i * carry_blk - 1, 0), cb))

    def bnd_spec(cols):
        return pl.BlockSpec((TR, cols), lambda i: (jnp.maximum(i - n_prompt_tiles, 0), 0))

    def const_spec(shape):
        return pl.BlockSpec(shape, lambda i: (0,) * len(shape))

    out = jax.ShapeDtypeStruct((n, D), F32)
    return pl.pallas_call(
        functools.partial(_prep_kernel, n_prompt_tiles),
        grid=(n // TR,),
        in_specs=[row_spec(3 * D), row_spec(LORA_COLS, lora_blk),
                  carry_spec(3 * D), carry_spec(LORA_COLS, lora_blk),
                  bnd_spec(3 * D), bnd_spec(LORA_COLS),
                  const_spec((1, 3 * D)), const_spec((1, LORA_COLS)),
                  const_spec((1, D)), const_spec((1, D)), const_spec((1, D)), const_spec((1, D)),
                  const_spec((1, D)),
                  const_spec((LORA_PAD, D)), const_spec((LORA_PAD, D)), const_spec((LORA_G, D)),
                  const_spec((GROUP_LANES, GROUP_LANES))],
        out_specs=[row_spec(D)] * 8,
        out_shape=[out] * 8,
        compiler_params=_cparams(("arbitrary",)),
        name="rwkv_prep",
    )(p, p, p, p, bnd_rkv, bnd_lora, mu_rkv, mu_lora, w0, a0, k_k, k_a, r_k, w2p, a2p, g2, bd)


def _scan_kernel(n_prompt_tiles, r_ref, k_ref, v_ref, kk_ref, b_ref, lw_ref, s_in_ref, y_ref, s_out_ref, s_scr):
    i = pl.program_id(0)
    is_sample = i >= n_prompt_tiles
    n_chunks = r_ref.shape[0] // CHUNK

    lane = lax.broadcasted_iota(jnp.int32, (1, GROUP_LANES), 1)
    head_masks = [(lane // HEAD == h).astype(F32) for h in range(GROUP_HEADS)]
    ri = lax.broadcasted_iota(jnp.int32, (STACK, STACK), 0)
    ci = lax.broadcasted_iota(jnp.int32, (STACK, STACK), 1)
    same_head = (ri // CHUNK) == (ci // CHUNK)
    strict_lower = jnp.logical_and(same_head, (ci % CHUNK) < (ri % CHUNK)).astype(F32)
    lower_incl = jnp.logical_and(same_head, (ci % CHUNK) <= (ri % CHUNK)).astype(F32)
    eye = (ri == ci).astype(F32)
    ti = lax.broadcasted_iota(jnp.int32, (CHUNK, CHUNK), 0)
    si = lax.broadcasted_iota(jnp.int32, (CHUNK, CHUNK), 1)
    tri_incl = (si <= ti).astype(BF16)
    rb = lax.broadcasted_iota(jnp.int32, (GROUP_LANES, GROUP_LANES), 0)
    cb = lax.broadcasted_iota(jnp.int32, (GROUP_LANES, GROUP_LANES), 1)
    block_diag = ((rb // HEAD) == (cb // HEAD)).astype(F32)

    def stack(x):
        return jnp.concatenate([x * m for m in head_masks], axis=0)

    def unstack(x):
        out = x[0:CHUNK]
        for h in range(1, GROUP_HEADS):
            out = out + x[h * CHUNK:(h + 1) * CHUNK]
        return out

    def compact(s):
        out = s[0:HEAD]
        for h in range(1, GROUP_HEADS):
            out = out + s[h * HEAD:(h + 1) * HEAD]
        return out

    @pl.when(i == 0)
    def _():
        s_out_ref[...] = jnp.zeros_like(s_out_ref)

    def chunk_body(c, carry):
        row0 = pl.multiple_of(c * CHUNK, CHUNK)
        rows = pl.ds(row0, CHUNK)
        load_state = jnp.logical_or(is_sample, jnp.logical_and(i == 0, c == 0))
        slot = jnp.where(is_sample, c, 0)
        store_state = jnp.logical_or(
            is_sample, jnp.logical_and(i == n_prompt_tiles - 1, c == n_chunks - 1))
        for g in range(N_LANE_GROUPS):
            lanes = slice(g * GROUP_LANES, (g + 1) * GROUP_LANES)
            lw = lw_ref[rows, lanes]
            cum = _dot_exact_lhs(tri_incl, lw)
            e_incl = jnp.exp(cum)
            e_excl = jnp.exp(cum - lw)
            e_neg = jnp.exp(-cum)
            p_end = e_incl[CHUNK - 1:CHUNK, :]
            r_t = r_ref[rows, lanes] * e_incl
            kk_t = kk_ref[rows, lanes] * e_excl
            b_t = b_ref[rows, lanes] * e_neg
            k_t = k_ref[rows, lanes] * e_neg
            v_s = stack(v_ref[rows, lanes])
            kk_s, r_s, b_s, k_s = stack(kk_t), stack(r_t), stack(b_t), stack(k_t)
            m_ab = _dot3(kk_s, b_s, _NT) * strict_lower
            m_ak = _dot3(kk_s, k_s, _NT) * strict_lower
            n_rb = _dot3(r_s, b_s, _NT) * lower_incl
            n_rk = _dot3(r_s, k_s, _NT) * lower_incl
            m2 = _dot3(m_ab, m_ab)
            m4 = _dot3(m2, m2)
            m8 = _dot3(m4, m4)
            t_inv = _dot3(eye - m_ab, eye + m2)
            t_inv = _dot3(t_inv, eye + m4)
            t_inv = _dot3(t_inv, eye + m8)

            s_loaded = jnp.concatenate([s_in_ref[slot, :, lanes]] * GROUP_HEADS, axis=0) * block_diag
            s0 = jnp.where(load_state, s_loaded, s_scr[g])
            gr = _dot3(jnp.concatenate([kk_t, r_t], axis=0), s0, _NT)
            rhs = stack(gr[0:CHUNK]) + _dot3(m_ak, v_s)
            u_s = -_dot3(t_inv, rhs)
            y = gr[CHUNK:] + unstack(_dot3(n_rb, u_s) + _dot3(n_rk, v_s))
            y_ref[rows, lanes] = y
            s_new = s0 * p_end + _dot3(u_s, b_s * p_end, _TN) + _dot3(v_s, k_s * p_end, _TN)
            s_scr[g] = s_new

            @pl.when(store_state)
            def _():
                s_out_ref[slot, :, lanes] = compact(s_new)
        return carry

    lax.fori_loop(0, n_chunks, chunk_body, 0)


def _rwkv_scan(r, k, v, kk, b, lw, s_in, n_prompt):
    n = r.shape[0]
    n_prompt_tiles = n_prompt // SCAN_ROWS
    row_spec = pl.BlockSpec((SCAN_ROWS, D), lambda i: (i, 0))
    state_spec = pl.BlockSpec((STATE_SLOTS, HEAD, D),
                              lambda i: (jnp.maximum(i - n_prompt_tiles + 1, 0), 0, 0))
    return pl.pallas_call(
        functools.partial(_scan_kernel, n_prompt_tiles),
        grid=(n // SCAN_ROWS,),
        in_specs=[row_spec] * 6 + [state_spec],
        out_specs=[row_spec, state_spec],
        out_shape=[jax.ShapeDtypeStruct((n, D), F32), jax.ShapeDtypeStruct(s_in.shape, F32)],
        scratch_shapes=[pltpu.VMEM((N_LANE_GROUPS, GROUP_LANES, GROUP_LANES), F32)],
        compiler_params=_cparams(("arbitrary",)),
        name="rwkv_scan",
    )(r, k, v, kk, b, lw, s_in)


def _conv_kernel(n_prompt_tiles, cb_ref, cc_ref, ch_ref, ccc_ref, cch_ref, bnd1_ref, bnd2_ref,
                 cw_ref, wout_ref, o_ref, u_ref):
    i = pl.program_id(0)
    rows = cb_ref.shape[0]
    row = lax.broadcasted_iota(jnp.int32, (rows, 1), 0)
    is_sample = i >= n_prompt_tiles
    pos = row % SEQ_S
    u = cc_ref[...] * ch_ref[...]
    u_ref[...] = u
    u_prev = jnp.where(i == 0, 0.0, ccc_ref[...] * cch_ref[...])
    prev1 = pltpu.roll(u, 1, 0)
    prev1 = jnp.where(row == 0, u_prev[7:8, :], prev1)
    prev2 = pltpu.roll(u, 2, 0)
    prev2 = jnp.where(row == 0, u_prev[6:7, :], prev2)
    prev2 = jnp.where(row == 1, u_prev[7:8, :], prev2)
    bnd1 = bnd1_ref[...]
    prev1 = jnp.where(jnp.logical_and(is_sample, pos == 0), bnd1, prev1)
    prev2 = jnp.where(jnp.logical_and(is_sample, pos == 0), bnd2_ref[...], prev2)
    prev2 = jnp.where(jnp.logical_and(is_sample, pos == 1), bnd1, prev2)
    cw = cw_ref[...]
    conv = prev2 * cw[0:1, :] + prev1 * cw[1:2, :] + u * cw[2:3, :]
    o_ref[...] = _dot(cb_ref[...] * conv, wout_ref[...])


def _short_conv(p, bnd1, bnd2, conv_w, w_out, n_prompt):
    n = p.shape[0]
    n_prompt_tiles = n_prompt // TR
    cblk = COL_CONV // CONV_DIM
    carry_blk = TR // 8

    def row_spec(cb):
        return pl.BlockSpec((TR, CONV_DIM), lambda i: (i, cb))

    def carry_spec(cb):
        return pl.BlockSpec((8, CONV_DIM), lambda i: (jnp.maximum(i * carry_blk - 1, 0), cb))

    bnd_spec = pl.BlockSpec((TR, CONV_DIM), lambda i: (jnp.maximum(i - n_prompt_tiles, 0), 0))
    return pl.pallas_call(
        functools.partial(_conv_kernel, n_prompt_tiles),
        grid=(n // TR,),
        in_specs=[row_spec(cblk), row_spec(cblk + 1), row_spec(cblk + 2),
                  carry_spec(cblk + 1), carry_spec(cblk + 2), bnd_spec, bnd_spec,
                  pl.BlockSpec((8, CONV_DIM), lambda i: (0, 0)),
                  pl.BlockSpec((CONV_DIM, D), lambda i: (0, 0))],
        out_specs=[pl.BlockSpec((TR, D), lambda i: (i, 0)), pl.BlockSpec((TR, CONV_DIM), lambda i: (i, 0))],
        out_shape=[jax.ShapeDtypeStruct((n, D), F32), jax.ShapeDtypeStruct((n, CONV_DIM), F32)],
        compiler_params=_cparams(("arbitrary",)),
        name="short_conv",
    )(p, p, p, p, p, bnd1, bnd2, conv_w, w_out)


def _mem_kernel(q_ref, k_ref, v_ref, wo_ref, o_ref):
    q = q_ref[...]
    k = k_ref[0]
    v = v_ref[0]
    outs = []
    for h in range(MEM_HEADS):
        sl = slice(h * MEM_HEAD_DIM, (h + 1) * MEM_HEAD_DIM)
        s = _dot(q[:, sl], k[:, sl], _NT) * (MEM_HEAD_DIM ** -0.5)
        s = s - jnp.max(s, axis=-1, keepdims=True)
        e = jnp.exp(s)
        pr = e / jnp.sum(e, axis=-1, keepdims=True)
        outs.append(_dot(pr, v[:, sl]))
    o_ref[...] = _dot(jnp.concatenate(outs, axis=1), wo_ref[...])


def _mem_sample(p, mem_k, mem_v, w_o, row_start, n_seq):
    qblk = COL_Q // MEM_DIM
    rb0 = row_start // SEQ_S
    return pl.pallas_call(
        _mem_kernel,
        grid=(n_seq,),
        in_specs=[pl.BlockSpec((SEQ_S, MEM_DIM), lambda i: (rb0 + i, qblk)),
                  pl.BlockSpec((1, N_MEM, MEM_DIM), lambda i: (i, 0, 0)),
                  pl.BlockSpec((1, N_MEM, MEM_DIM), lambda i: (i, 0, 0)),
                  pl.BlockSpec((MEM_DIM, D), lambda i: (0, 0))],
        out_specs=pl.BlockSpec((SEQ_S, D), lambda i: (i, 0)),
        out_shape=jax.ShapeDtypeStruct((n_seq * SEQ_S, D), F32),
        compiler_params=_cparams(("arbitrary",)),
        name="mem_attention_sample",
    )(p, mem_k, mem_v, w_o)


MEM_TILE = 256


def _mem_prompt_kernel(n_prompt_tiles, q_ref, k_ref, v_ref, wo_ref, tail_ref, o_ref):
    i = pl.program_id(0)

    @pl.when(i < n_prompt_tiles)
    def _():
        _mem_kernel(q_ref, k_ref, v_ref, wo_ref, o_ref)

    @pl.when(i >= n_prompt_tiles)
    def _():
        o_ref[...] = tail_ref[...]


def _mem_attention(p, mem_k, mem_v, w_o, o_sample, n_prompt):
    n = p.shape[0]
    qblk = COL_Q // MEM_DIM
    n_prompt_tiles = n_prompt // MEM_TILE
    return pl.pallas_call(
        functools.partial(_mem_prompt_kernel, n_prompt_tiles),
        grid=(n // MEM_TILE,),
        in_specs=[pl.BlockSpec((MEM_TILE, MEM_DIM), lambda i: (jnp.minimum(i, n_prompt_tiles - 1), qblk)),
                  pl.BlockSpec((1, N_MEM, MEM_DIM), lambda i: (0, 0, 0)),
                  pl.BlockSpec((1, N_MEM, MEM_DIM), lambda i: (0, 0, 0)),
                  pl.BlockSpec((MEM_DIM, D), lambda i: (0, 0)),
                  pl.BlockSpec((MEM_TILE, D), lambda i: (jnp.maximum(i - n_prompt_tiles, 0), 0))],
        out_specs=pl.BlockSpec((MEM_TILE, D), lambda i: (i, 0)),
        out_shape=jax.ShapeDtypeStruct((n, D), F32),
        compiler_params=_cparams(("arbitrary",)),
        name="mem_attention",
    )(p, mem_k, mem_v, w_o, o_sample)


def _layer_norm(z, g, b):
    mu = jnp.mean(z, axis=-1, keepdims=True)
    d = z - mu
    var = jnp.mean(d * d, axis=-1, keepdims=True)
    return d * lax.rsqrt(var + LN_EPS) * g + b


def _merge_kernel(x_ref, ga_ref, gb_ref, gm_ref, y_ref, bonus_ref, g_ref, ocv_ref, omem_ref,
                  gng_ref, gnb_ref, bd_ref, wo_ref, l1g_ref, l1b_ref, wr_ref, h_o, lt_o):
    bd = bd_ref[...]
    y = y_ref[...]
    mean = _head_sum(y, bd) * (1.0 / HEAD)
    d = y - mean
    var = _head_sum(d * d, bd) * (1.0 / HEAD)
    yn = d * lax.rsqrt(var + GN_EPS) * gng_ref[...] + gnb_ref[...]
    o_rw = (yn + bonus_ref[...]) * g_ref[...]
    merged = (_sigmoid(ga_ref[...]) * o_rw + _sigmoid(gb_ref[...]) * ocv_ref[...]
              + _sigmoid(gm_ref[...]) * omem_ref[...])
    z = ALPHA * x_ref[...] + _dot(merged, wo_ref[...])
    h = _layer_norm(z, l1g_ref[...], l1b_ref[...])
    h_o[...] = h
    lt_o[...] = _dot3(wr_ref[...], h, _NT)


def _merge_ln1(x, p, y_raw, bonus, g, o_cv, o_mem, gn_g, gn_b, bd, w_o, ln_g, ln_b, w_router_t):
    n = x.shape[0]
    gblk = COL_GATE // D
    row = pl.BlockSpec((TR, D), lambda i: (i, 0))

    def gate_spec(j):
        return pl.BlockSpec((TR, D), lambda i: (i, gblk + j))

    def const_spec(shape):
        return pl.BlockSpec(shape, lambda i: (0,) * len(shape))

    vec = const_spec((1, D))
    return pl.pallas_call(
        _merge_kernel,
        grid=(n // TR,),
        in_specs=[row, gate_spec(0), gate_spec(1), gate_spec(2), row, row, row, row, row,
                  vec, vec, const_spec((GROUP_LANES, GROUP_LANES)), const_spec((D, D)), vec, vec,
                  const_spec((N_EXPERTS, D))],
        out_specs=[row, pl.BlockSpec((N_EXPERTS, TR), lambda i: (0, i))],
        out_shape=[jax.ShapeDtypeStruct((n, D), F32), jax.ShapeDtypeStruct((N_EXPERTS, n), F32)],
        compiler_params=_cparams(("arbitrary",)),
        name="merge_ln1",
    )(x, p, p, p, y_raw, bonus, g, o_cv, o_mem, gn_g, gn_b, bd, w_o, ln_g, ln_b, w_router_t)


ROUTE_TILE = 256


def _routing_kernel(lt_ref, bias_ref, tri_ref, idx_o, w_o, pos_o, cnt_o, carry):
    i = pl.program_id(0)
    tile = lt_ref.shape[1]

    @pl.when(i == 0)
    def _():
        carry[...] = jnp.zeros_like(carry)

    neg_inf = -jnp.inf
    scores = _sigmoid(lt_ref[...])
    choice = scores + bias_ref[...]
    row = lax.broadcasted_iota(jnp.int32, (N_EXPERTS, tile), 0)
    rowf = row.astype(F32)
    grpf = (row // GROUP_SIZE).astype(F32)

    def group_allreduce(x, op):
        for s in (1, 2, 4):
            up = pltpu.roll(x, N_EXPERTS - s, 0)
            dn = pltpu.roll(x, s, 0)
            x = op(x, jnp.where((row & s) == 0, up, dn))
        return x

    m1 = group_allreduce(choice, jnp.maximum)
    first = group_allreduce(jnp.where(choice == m1, rowf, float(N_EXPERTS)), jnp.minimum)
    m2 = group_allreduce(jnp.where(rowf == first, neg_inf, choice), jnp.maximum)
    gscore = m1 + m2

    gsel = jnp.zeros_like(choice)
    for _ in range(TOPK_GROUPS):
        gmax = jnp.max(gscore, axis=0, keepdims=True)
        pick = jnp.min(jnp.where(gscore == gmax, grpf, float(N_GROUPS)), axis=0, keepdims=True)
        hit = grpf == pick
        gsel = jnp.where(hit, 1.0, gsel)
        gscore = jnp.where(hit, neg_inf, gscore)

    masked = jnp.where(gsel > 0.0, choice, neg_inf)
    row8 = lax.broadcasted_iota(jnp.int32, (TOP_K, tile), 0)
    idx_acc = jnp.zeros((TOP_K, tile), F32)
    w_acc = jnp.zeros((TOP_K, tile), F32)
    sel_all = jnp.zeros_like(choice)
    for kk in range(TOP_K):
        mx = jnp.max(masked, axis=0, keepdims=True)
        pick = jnp.min(jnp.where(masked == mx, rowf, float(N_EXPERTS)), axis=0, keepdims=True)
        hit = rowf == pick
        wk = jnp.sum(jnp.where(hit, scores, 0.0), axis=0, keepdims=True)
        idx_acc = jnp.where(row8 == kk, pick, idx_acc)
        w_acc = jnp.where(row8 == kk, wk, w_acc)
        sel_all = jnp.where(hit, 1.0, sel_all)
        masked = jnp.where(hit, neg_inf, masked)

    w_sum = jnp.sum(w_acc, axis=0, keepdims=True)
    w_o[...] = w_acc / w_sum * ROUTED_SCALE
    idx_o[...] = idx_acc.astype(jnp.int32)

    prefix = lax.dot_general(sel_all.astype(BF16), tri_ref[...], _NN, preferred_element_type=F32) + carry[...]
    pos_acc = jnp.zeros((TOP_K, tile), F32)
    for kk in range(TOP_K):
        hit = rowf == idx_acc[kk:kk + 1, :]
        pk = jnp.sum(jnp.where(hit, prefix, 0.0), axis=0, keepdims=True)
        pos_acc = jnp.where(row8 == kk, pk, pos_acc)
    pos_o[...] = pos_acc.astype(jnp.int32)
    carry[...] = carry[...] + jnp.sum(sel_all, axis=1, keepdims=True)
    cnt_o[...] = carry[...]


def _routing(logits_t, bias_col, tri):
    n = logits_t.shape[1]
    tile = ROUTE_TILE
    tok = pl.BlockSpec((TOP_K, tile), lambda i: (0, i))
    return pl.pallas_call(
        _routing_kernel,
        grid=(n // tile,),
        in_specs=[pl.BlockSpec((N_EXPERTS, tile), lambda i: (0, i)),
                  pl.BlockSpec((N_EXPERTS, 1), lambda i: (0, 0)),
                  pl.BlockSpec((tile, tile), lambda i: (0, 0))],
        out_specs=[tok, tok, tok, pl.BlockSpec((N_EXPERTS, 1), lambda i: (0, 0))],
        out_shape=[jax.ShapeDtypeStruct((TOP_K, n), jnp.int32), jax.ShapeDtypeStruct((TOP_K, n), F32),
                   jax.ShapeDtypeStruct((TOP_K, n), jnp.int32), jax.ShapeDtypeStruct((N_EXPERTS, 1), F32)],
        scratch_shapes=[pltpu.VMEM((N_EXPERTS, 1), F32)],
        compiler_params=_cparams(("arbitrary",)),
        name="routing",
    )(logits_t, bias_col, tri)


def _dispatch_kernel(dest_ref, h_ref, xb_in, xb_out, dest_smem, sem, idx_sem):
    del xb_in
    rows = h_ref.shape[0]
    cp = pltpu.make_async_copy(dest_ref, dest_smem, idx_sem)
    cp.start()
    cp.wait()

    def row_copy(t, k):
        return pltpu.make_async_copy(h_ref.at[pl.ds(t, 1), :],
                                     xb_out.at[pl.ds(dest_smem[k, t], 1), :], sem)

    def issue(t, c):
        for k in range(TOP_K):
            row_copy(t, k).start()
        return c

    def drain(t, c):
        for k in range(TOP_K):
            row_copy(t, k).wait()
        return c

    lax.fori_loop(0, rows, issue, 0)
    lax.fori_loop(0, rows, drain, 0)


def _dispatch(dest_t, h, xb_init):
    n = h.shape[0]
    return pl.pallas_call(
        _dispatch_kernel,
        grid=(n // TR,),
        in_specs=[pl.BlockSpec((TOP_K, TR), lambda i: (0, i)),
                  pl.BlockSpec((TR, D), lambda i: (i, 0)),
                  pl.BlockSpec(memory_space=pl.ANY)],
        out_specs=pl.BlockSpec(memory_space=pl.ANY),
        out_shape=jax.ShapeDtypeStruct(xb_init.shape, F32),
        input_output_aliases={2: 0},
        scratch_shapes=[pltpu.SMEM((TOP_K, TR), jnp.int32), pltpu.SemaphoreType.DMA, pltpu.SemaphoreType.DMA],
        compiler_params=_cparams(("arbitrary",)),
        name="moe_dispatch",
    )(dest_t, h, xb_init)


def _silu(x):
    return x * _sigmoid(x)


def _expert_kernel(be_ref, nu_ref, x_ref, wu_ref, wd_ref, o_ref, wu_bf, wd_bf):
    b = pl.program_id(0)
    changed = jnp.logical_or(b == 0, be_ref[b] != be_ref[jnp.maximum(b - 1, 0)])

    @pl.when(changed)
    def _():
        wu_bf[...] = wu_ref[0].astype(BF16)
        wd_bf[...] = wd_ref[0].astype(BF16)

    @pl.when(b < nu_ref[0])
    def _():
        up = _dot(x_ref[...], wu_bf[...])
        act = _silu(up[:, :EXPERT_FF]) * up[:, EXPERT_FF:]
        o_ref[...] = _dot(act, wd_bf[...])

    @pl.when(b >= nu_ref[0])
    def _():
        o_ref[...] = jnp.zeros_like(o_ref)


def _experts(block_e, n_used, xb, w_up, w_down):
    rows = xb.shape[0]
    nb = rows // EXPERT_BM

    def xmap(b, be, nu):
        return (jnp.minimum(b, nu[0] - 1), 0)

    grid_spec = pltpu.PrefetchScalarGridSpec(
        num_scalar_prefetch=2,
        grid=(nb,),
        in_specs=[pl.BlockSpec((EXPERT_BM, D), xmap),
                  pl.BlockSpec((1, D, 2 * EXPERT_FF), lambda b, be, nu: (be[b], 0, 0)),
                  pl.BlockSpec((1, EXPERT_FF, D), lambda b, be, nu: (be[b], 0, 0))],
        out_specs=pl.BlockSpec((EXPERT_BM, D), lambda b, be, nu: (b, 0)),
        scratch_shapes=[pltpu.VMEM((D, 2 * EXPERT_FF), BF16), pltpu.VMEM((EXPERT_FF, D), BF16)],
    )
    return pl.pallas_call(
        _expert_kernel,
        grid_spec=grid_spec,
        out_shape=jax.ShapeDtypeStruct((rows, D), F32),
        compiler_params=_cparams(("arbitrary",)),
        name="moe_experts",
    )(block_e, n_used, xb, w_up, w_down)


SHARED_TILE = 512


def _shared_kernel(h_ref, wu_ref, wd_ref, o_ref):
    up = _dot(h_ref[...], wu_ref[...])
    act = _silu(up[:, :SHARED_FF]) * up[:, SHARED_FF:]
    o_ref[...] = _dot(act, wd_ref[...])


def _shared_ffn(h, w_up, w_down):
    n = h.shape[0]
    row = pl.BlockSpec((SHARED_TILE, D), lambda i: (i, 0))
    return pl.pallas_call(
        _shared_kernel,
        grid=(n // SHARED_TILE,),
        in_specs=[row, pl.BlockSpec((D, 2 * SHARED_FF), lambda i: (0, 0)),
                  pl.BlockSpec((SHARED_FF, D), lambda i: (0, 0))],
        out_specs=row,
        out_shape=jax.ShapeDtypeStruct((n, D), F32),
        compiler_params=_cparams(("parallel",)),
        name="shared_ffn",
    )(h, w_up, w_down)


def _combine_kernel(dest_ref, w_ref, h_ref, sh_ref, yb_ref, l2g_ref, l2b_ref, y_o, buf, dest_smem, sem, idx_sem):
    rows = h_ref.shape[0]
    cp = pltpu.make_async_copy(dest_ref, dest_smem, idx_sem)
    cp.start()
    cp.wait()

    def row_copy(t, k):
        return pltpu.make_async_copy(yb_ref.at[pl.ds(dest_smem[k, t], 1), :],
                                     buf.at[k, pl.ds(t, 1), :], sem)

    def issue(t, c):
        for k in range(TOP_K):
            row_copy(t, k).start()
        return c

    def drain(t, c):
        for k in range(TOP_K):
            row_copy(t, k).wait()
        return c

    lax.fori_loop(0, rows, issue, 0)
    lax.fori_loop(0, rows, drain, 0)
    w = w_ref[...]
    f = sh_ref[...]
    for k in range(TOP_K):
        f = f + w[:, k:k + 1] * buf[k]
    z = ALPHA * h_ref[...] + f
    y_o[...] = _layer_norm(z, l2g_ref[...], l2b_ref[...])


def _combine_ln2(dest_t, w_tok, h, shared, yb, ln_g, ln_b):
    n = h.shape[0]
    row = pl.BlockSpec((TR, D), lambda i: (i, 0))
    vec = pl.BlockSpec((1, D), lambda i: (0, 0))
    return pl.pallas_call(
        _combine_kernel,
        grid=(n // TR,),
        in_specs=[pl.BlockSpec((TOP_K, TR), lambda i: (0, i)),
                  pl.BlockSpec((TR, TOP_K), lambda i: (i, 0)),
                  row, row, pl.BlockSpec(memory_space=pl.ANY), vec, vec],
        out_specs=row,
        out_shape=jax.ShapeDtypeStruct((n, D), F32),
        scratch_shapes=[pltpu.VMEM((TOP_K, TR, D), F32), pltpu.SMEM((TOP_K, TR), jnp.int32),
                        pltpu.SemaphoreType.DMA, pltpu.SemaphoreType.DMA],
        compiler_params=_cparams(("arbitrary",)),
        name="moe_combine_ln2",
    )(dest_t, w_tok, h, shared, yb, ln_g, ln_b)


def _reorder_cols(w):
    pad = jnp.zeros(w.shape[:-1] + (LORA_PAD - LORA_W,), w.dtype)
    rw_end = RW_COLS
    cv_end = rw_end + 3 * CONV_DIM
    q_end = cv_end + MEM_DIM
    return jnp.concatenate(
        [w[..., 0:3 * D], w[..., q_end:q_end + 3 * D], w[..., rw_end:cv_end], w[..., cv_end:q_end],
         w[..., 3 * D:3 * D + LORA_W], pad,
         w[..., 3 * D + LORA_W:3 * D + LORA_W + LORA_A], pad,
         w[..., 3 * D + LORA_W + LORA_A:rw_end]], axis=-1)


def _rw_cols_split(v):
    pad = jnp.zeros(v.shape[:-1] + (LORA_PAD - LORA_W,), v.dtype)
    lora = jnp.concatenate([v[..., 3 * D:3 * D + LORA_W], pad,
                            v[..., 3 * D + LORA_W:3 * D + LORA_W + LORA_A], pad,
                            v[..., 3 * D + LORA_W + LORA_A:]], axis=-1)
    return v[..., 0:3 * D], lora


def _pad_rows(w, rows):
    return jnp.concatenate([w, jnp.zeros((rows - w.shape[0],) + w.shape[1:], w.dtype)], axis=0)


def kernel(x_prompt, x_sample, mem_prompt, state_rwkv, state_shift, state_conv, cache_mem_k, cache_mem_v,
           w_in, mu_shift, rw_w0, rw_w2, rw_a0, rw_a2, rw_g2, rw_k_k, rw_k_a, rw_r_k, rw_gn_g, rw_gn_b,
           conv_w, w_conv_out, w_mem_k, w_mem_v, w_mem_o, w_o, ln1_g, ln1_b, w_router, router_bias,
           w_exp_up, w_exp_down, w_sh_up, w_sh_down, ln2_g, ln2_b):
    n_prompt = x_prompt.shape[0] * x_prompt.shape[1]
    n_seq_s, seq_s = x_sample.shape[0], x_sample.shape[1]
    n_sample = n_seq_s * seq_s
    n = n_prompt + n_sample
    assert x_prompt.shape[0] == 1 and seq_s == SEQ_S and n_prompt % TR == 0 and n_sample % TR == 0
    assert n % SHARED_TILE == 0 and n % ROUTE_TILE == 0 and w_in.shape[0] == 1

    x = jnp.concatenate([x_prompt.reshape(n_prompt, D), x_sample.reshape(n_sample, D)], axis=0)

    def vec(v):
        return v.reshape(1, -1).astype(F32)

    w_in_r = _reorder_cols(w_in[0]).astype(BF16)
    p = _matmul(x, w_in_r, 512, 512, "in_proj")

    w_kv = jnp.concatenate([w_mem_k[0], w_mem_v[0]], axis=1).astype(BF16)
    kv = _matmul(mem_prompt[0], w_kv, N_MEM, 512, "mem_kv")
    mem_k_p, mem_v_p = kv[:, :MEM_DIM], kv[:, MEM_DIM:]

    mu_rkv, mu_lora = _rw_cols_split(vec(mu_shift[0]))
    sh_rkv, sh_lora = _rw_cols_split(state_shift[0, :, 0, :])
    bnd_rkv = jnp.repeat(sh_rkv, seq_s, axis=0)
    bnd_lora = jnp.repeat(sh_lora, seq_s, axis=0)
    hi = lax.broadcasted_iota(jnp.int32, (GROUP_LANES, GROUP_LANES), 0) // HEAD
    hj = lax.broadcasted_iota(jnp.int32, (GROUP_LANES, GROUP_LANES), 1) // HEAD
    bd = (hi == hj).astype(BF16)
    r, k, v, kk, b, lw, g, bonus = _rwkv_prep(
        p, bnd_rkv, bnd_lora, mu_rkv, mu_lora, vec(rw_w0[0]), vec(rw_a0[0]), vec(rw_k_k[0]), vec(rw_k_a[0]),
        vec(rw_r_k[0]), _pad_rows(rw_w2[0], LORA_PAD).astype(BF16), _pad_rows(rw_a2[0], LORA_PAD).astype(BF16),
        rw_g2[0].astype(BF16), bd, n_prompt)

    s_sample = jnp.transpose(state_rwkv[0], (0, 2, 1, 3)).reshape(n_seq_s, HEAD, D)
    s_in = jnp.concatenate([jnp.zeros((STATE_SLOTS, HEAD, D), F32), s_sample.astype(F32)], axis=0)
    y_raw, s_out = _rwkv_scan(r, k, v, kk, b, lw, s_in, n_prompt)

    bnd1 = jnp.repeat(state_conv[0, :, 1, :], seq_s, axis=0)
    bnd2 = jnp.repeat(state_conv[0, :, 0, :], seq_s, axis=0)
    o_cv, u = _short_conv(p, bnd1, bnd2, _pad_rows(conv_w[0], 8), w_conv_out[0].astype(BF16), n_prompt)

    w_mem_o_b = w_mem_o[0].astype(BF16)
    o_mem_s = _mem_sample(p, cache_mem_k[0].reshape(n_seq_s, N_MEM, MEM_DIM),
                          cache_mem_v[0].reshape(n_seq_s, N_MEM, MEM_DIM), w_mem_o_b, n_prompt, n_seq_s)
    o_mem = _mem_attention(p, mem_k_p[None], mem_v_p[None], w_mem_o_b, o_mem_s, n_prompt)

    h, logits_t = _merge_ln1(x, p, y_raw, bonus, g, o_cv, o_mem, vec(rw_gn_g[0]), vec(rw_gn_b[0]), bd,
                             w_o[0].astype(BF16), vec(ln1_g[0]), vec(ln1_b[0]), w_router[0].T)

    ti = lax.broadcasted_iota(jnp.int32, (ROUTE_TILE, ROUTE_TILE), 0)
    tj = lax.broadcasted_iota(jnp.int32, (ROUTE_TILE, ROUTE_TILE), 1)
    tri = (ti < tj).astype(BF16)
    idx_t, w_t, pos_t, counts = _routing(logits_t, router_bias[0].reshape(N_EXPERTS, 1).astype(F32), tri)

    counts = counts[:, 0].astype(jnp.int32)
    padded = (counts + EXPERT_BM - 1) // EXPERT_BM * EXPERT_BM
    seg_end = jnp.cumsum(padded)
    seg_start = seg_end - padded
    dest_t = seg_start[idx_t] + pos_t
    nb = (n * TOP_K) // EXPERT_BM + N_EXPERTS
    block_e = jnp.minimum(
        jnp.searchsorted(seg_end, jnp.arange(nb, dtype=jnp.int32) * EXPERT_BM, side='right'),
        N_EXPERTS - 1).astype(jnp.int32)
    n_used = (seg_end[-1:] // EXPERT_BM).astype(jnp.int32)

    xb = _dispatch(dest_t, h, jnp.zeros((nb * EXPERT_BM, D), F32))
    yb = _experts(block_e, n_used, xb, w_exp_up[0], w_exp_down[0])
    shared = _shared_ffn(h, w_sh_up[0].astype(BF16), w_sh_down[0].astype(BF16))
    y = _combine_ln2(dest_t, w_t.T, h, shared, yb, vec(ln2_g[0]), vec(ln2_b[0]))

    dt = x_prompt.dtype
    y_p = y[:n_prompt].reshape(x_prompt.shape)
    y_s = y[n_prompt:].reshape(x_sample.shape)

    def state_out(s):
        q = s.reshape(s.shape[0], HEAD, N_HEADS, HEAD)
        return jnp.transpose(q, (0, 2, 1, 3))[None].astype(dt)

    rw_p = state_out(s_out[0:1])
    rw_s = state_out(s_out[STATE_SLOTS:])

    last_rows = jnp.concatenate([jnp.array([n_prompt - 1], jnp.int32),
                                 n_prompt + seq_s - 1 + seq_s * jnp.arange(n_seq_s, dtype=jnp.int32)])
    p_last = p[last_rows]
    shift = jnp.concatenate([p_last[:, 0:3 * D],
                             p_last[:, COL_LORA:COL_LORA + LORA_W],
                             p_last[:, COL_LORA + LORA_PAD:COL_LORA + LORA_PAD + LORA_A],
                             p_last[:, COL_LORA + 2 * LORA_PAD:]], axis=1)
    sh_p = shift[0:1].reshape(1, 1, 1, RW_COLS)
    sh_s = shift[1:].reshape(1, n_seq_s, 1, RW_COLS)

    cv_p = u[n_prompt - 2:n_prompt].reshape(1, 1, 2, CONV_DIM)
    cv_s = u[n_prompt:].reshape(n_seq_s, seq_s, CONV_DIM)[:, seq_s - 2:, :][None]

    mk_p = mem_k_p.reshape(1, 1, N_MEM, MEM_HEADS, MEM_HEAD_DIM)
    mv_p = mem_v_p.reshape(1, 1, N_MEM, MEM_HEADS, MEM_HEAD_DIM)
    return (y_p, y_s, rw_p, sh_p, cv_p, mk_p, mv_p, rw_s, sh_s, cv_s)
```

```python
import functools

import jax
import jax.numpy as jnp
from jax import lax
from jax.experimental import pallas as pl
from jax.experimental.pallas import tpu as pltpu

F32 = jnp.float32
BF16 = jnp.bfloat16

D = 2048
HEAD = 64
N_HEADS = D // HEAD
LORA_W = 96
LORA_A = 96
LORA_G = 256
DECAY_SCALE = 0.6065306597126334
GN_EPS = HEAD * 1e-5
CONV_DIM = D // 2
N_MEM = 256
MEM_HEADS = 4
MEM_HEAD_DIM = 256
MEM_DIM = MEM_HEADS * MEM_HEAD_DIM
N_EXPERTS = 64
N_GROUPS = 8
GROUP_SIZE = N_EXPERTS // N_GROUPS
TOPK_GROUPS = 4
TOP_K = 8
EXPERT_FF = 512
SHARED_FF = 512
ROUTED_SCALE = 2.5
LN_EPS = 1e-5
DEPTH = 1
ALPHA = (2 * DEPTH) ** 0.25
RW_COLS = 3 * D + LORA_W + LORA_A + LORA_G

LORA_PAD = 128
LORA_COLS = 2 * LORA_PAD + LORA_G
COL_RKV = 0
COL_GATE = 3 * D
COL_CONV = 6 * D
COL_Q = COL_CONV + 3 * CONV_DIM
COL_LORA = COL_Q + MEM_DIM
P_COLS = COL_LORA + LORA_COLS

CHUNK = 16
GROUP_HEADS = 4
GROUP_LANES = GROUP_HEADS * HEAD
N_LANE_GROUPS = D // GROUP_LANES
STACK = GROUP_HEADS * CHUNK
SEQ_S = 16
STATE_SLOTS = 8

TR = 128
SCAN_ROWS = STATE_SLOTS * CHUNK
EXPERT_BM = 256
VMEM_LIMIT = 56 * 1024 * 1024


def _cparams(sem):
    return pltpu.CompilerParams(dimension_semantics=sem, vmem_limit_bytes=VMEM_LIMIT)


def _sigmoid(x):
    return 1.0 / (1.0 + jnp.exp(-x))


def _dot(a, b, dims=(((1,), (0,)), ((), ()))):
    return lax.dot_general(a.astype(BF16), b.astype(BF16), dims, preferred_element_type=F32)


_NN = (((1,), (0,)), ((), ()))
_NT = (((1,), (1,)), ((), ()))
_TN = (((0,), (0,)), ((), ()))


def _split2(x):
    hi = x.astype(BF16)
    lo = (x - hi.astype(F32)).astype(BF16)
    return hi, lo


def _split3(x):
    hi = x.astype(BF16)
    r1 = x - hi.astype(F32)
    mid = r1.astype(BF16)
    lo = (r1 - mid.astype(F32)).astype(BF16)
    return hi, mid, lo


def _dot3(a, b, dims=_NN):
    ah, al = _split2(a)
    bh, bl = _split2(b)
    f = functools.partial(lax.dot_general, dimension_numbers=dims, preferred_element_type=F32)
    return f(ah, bh) + (f(ah, bl) + f(al, bh))


def _dot_exact_rhs(a, b_bf16, dims=_NN):
    hi, mid, lo = _split3(a)
    f = functools.partial(lax.dot_general, dimension_numbers=dims, preferred_element_type=F32)
    return f(hi, b_bf16) + (f(mid, b_bf16) + f(lo, b_bf16))


def _dot_exact_lhs(a_bf16, b):
    hi, mid, lo = _split3(b)
    f = functools.partial(lax.dot_general, dimension_numbers=_NN, preferred_element_type=F32)
    return f(a_bf16, hi) + (f(a_bf16, mid) + f(a_bf16, lo))


_sdot = _dot


def _mm_kernel(x_ref, w_ref, o_ref):
    o_ref[...] = _dot(x_ref[...], w_ref[...]).astype(o_ref.dtype)


def _matmul(x, w, tm, tn, name):
    m, k = x.shape
    n = w.shape[1]
    return pl.pallas_call(
        _mm_kernel,
        grid=(m // tm, n // tn),
        in_specs=[pl.BlockSpec((tm, k), lambda i, j: (i, 0)),
                  pl.BlockSpec((k, tn), lambda i, j: (0, j))],
        out_specs=pl.BlockSpec((tm, tn), lambda i, j: (i, j)),
        out_shape=jax.ShapeDtypeStruct((m, n), F32),
        compiler_params=_cparams(("parallel", "arbitrary")),
        name=name,
    )(x, w)


PROJ_TM = 512
PROJ_TN = 512


def _in_proj_kernel(n_prompt_tiles, xp_ref, xs_ref, w_ref, o_ref, x_bf):
    i = pl.program_id(0)

    @pl.when(pl.program_id(1) == 0)
    def _():
        x_bf[...] = jnp.where(i < n_prompt_tiles, xp_ref[...], xs_ref[...]).astype(BF16)

    o_ref[...] = jnp.dot(x_bf[...], w_ref[...], preferred_element_type=F32)


def _in_proj(xp, xs, w):
    k = xp.shape[1]
    n_prompt_tiles = xp.shape[0] // PROJ_TM
    n_tiles = n_prompt_tiles + xs.shape[0] // PROJ_TM
    ncols = w.shape[1]
    return pl.pallas_call(
        functools.partial(_in_proj_kernel, n_prompt_tiles),
        grid=(n_tiles, ncols // PROJ_TN),
        in_specs=[pl.BlockSpec((PROJ_TM, k), lambda i, j: (jnp.minimum(i, n_prompt_tiles - 1), 0)),
                  pl.BlockSpec((PROJ_TM, k), lambda i, j: (jnp.maximum(i - n_prompt_tiles, 0), 0)),
                  pl.BlockSpec((k, PROJ_TN), lambda i, j: (0, j))],
        out_specs=pl.BlockSpec((PROJ_TM, PROJ_TN), lambda i, j: (i, j)),
        out_shape=jax.ShapeDtypeStruct((n_tiles * PROJ_TM, ncols), F32),
        scratch_shapes=[pltpu.VMEM((PROJ_TM, k), BF16)],
        compiler_params=_cparams(("arbitrary", "arbitrary")),
        name="in_proj",
    )(xp, xs, w)


def _head_sum(x, bd):
    parts = []
    for g in range(N_LANE_GROUPS):
        parts.append(_dot_exact_rhs(x[:, g * GROUP_LANES:(g + 1) * GROUP_LANES], bd))
    return jnp.concatenate(parts, axis=1)


def _prep_kernel(n_prompt_tiles, rkv_ref, lora_ref, c_rkv_ref, c_lora_ref, b_rkv_ref, b_lora_ref,
                 mu_rkv_ref, mu_lora_ref, w0_ref, a0_ref, kk_ref, ka_ref, rk_ref,
                 w2_ref, a2_ref, g2_ref, bd_ref,
                 r_o, k_o, v_o, kk_o, b_o, lw_o, g_o, bonus_o):
    i = pl.program_id(0)
    rows = rkv_ref.shape[0]
    row = lax.broadcasted_iota(jnp.int32, (rows, 1), 0)
    is_sample = i >= n_prompt_tiles
    seq_start = jnp.logical_and(is_sample, (row % SEQ_S) == 0)

    def mixed(x, carry_row, bnd, mu):
        prev = pltpu.roll(x, 1, 0)
        carry_row = jnp.where(i == 0, 0.0, carry_row)
        prev = jnp.where(row == 0, carry_row, prev)
        prev = jnp.where(seq_start, bnd, prev)
        return x + (prev - x) * mu

    def section(s):
        sl = slice(s * D, (s + 1) * D)
        return mixed(rkv_ref[:, sl], c_rkv_ref[7:8, sl], b_rkv_ref[:, sl], mu_rkv_ref[:, sl])

    lo = mixed(lora_ref[...], c_lora_ref[7:8, :], b_lora_ref[...], mu_lora_ref[...])
    w_lo = lo[:, 0:LORA_PAD]
    a_lo = lo[:, LORA_PAD:2 * LORA_PAD]
    g_lo = lo[:, 2 * LORA_PAD:]
    log_w = -DECAY_SCALE * _sigmoid(w0_ref[...] + _dot(jnp.tanh(w_lo), w2_ref[...]))
    a = _sigmoid(a0_ref[...] + _dot(a_lo, a2_ref[...]))
    g_o[...] = _dot(_sigmoid(g_lo), g2_ref[...])
    lw_o[...] = log_w

    bd = bd_ref[...]
    k = section(1)
    kk = k * kk_ref[...]
    ss = _head_sum(kk * kk, bd)
    kk = kk * lax.rsqrt(jnp.maximum(ss, 1e-24))
    kk_o[...] = kk
    b_o[...] = kk * a
    k = k * (1.0 + (a - 1.0) * ka_ref[...])
    k_o[...] = k
    r = section(0)
    r_o[...] = r
    v = section(2)
    v_o[...] = v
    bonus_o[...] = _head_sum(r * k * rk_ref[...], bd) * v


def _rwkv_prep(p, bnd_rkv, bnd_lora, mu_rkv, mu_lora, w0, a0, k_k, k_a, r_k, w2p, a2p, g2, bd, n_prompt):
    n = p.shape[0]
    n_prompt_tiles = n_prompt // TR
    carry_blk = TR // 8
    lora_blk = COL_LORA // LORA_COLS

    def row_spec(cols, cb=0):
        return pl.BlockSpec((TR, cols), lambda i: (i, cb))

    def carry_spec(cols, cb=0):
        return pl.BlockSpec((8, cols), lambda i: (jnp.maximum(i * carry_blk - 1, 0), cb))

    def bnd_spec(cols):
        return pl.BlockSpec((TR, cols), lambda i: (jnp.maximum(i - n_prompt_tiles, 0), 0))

    def const_spec(shape):
        return pl.BlockSpec(shape, lambda i: (0,) * len(shape))

    out = jax.ShapeDtypeStruct((n, D), F32)
    return pl.pallas_call(
        functools.partial(_prep_kernel, n_prompt_tiles),
        grid=(n // TR,),
        in_specs=[row_spec(3 * D), row_spec(LORA_COLS, lora_blk),
                  carry_spec(3 * D), carry_spec(LORA_COLS, lora_blk),
                  bnd_spec(3 * D), bnd_spec(LORA_COLS),
                  const_spec((1, 3 * D)), const_spec((1, LORA_COLS)),
                  const_spec((1, D)), const_spec((1, D)), const_spec((1, D)), const_spec((1, D)),
                  const_spec((1, D)),
                  const_spec((LORA_PAD, D)), const_spec((LORA_PAD, D)), const_spec((LORA_G, D)),
                  const_spec((GROUP_LANES, GROUP_LANES))],
        out_specs=[row_spec(D)] * 8,
        out_shape=[out] * 8,
        compiler_params=_cparams(("arbitrary",)),
        name="rwkv_prep",
    )(p, p, p, p, bnd_rkv, bnd_lora, mu_rkv, mu_lora, w0, a0, k_k, k_a, r_k, w2p, a2p, g2, bd)


def _scan_kernel(n_prompt_tiles, r_ref, k_ref, v_ref, kk_ref, b_ref, lw_ref, s_in_ref, y_ref, s_out_ref, s_scr):
    i = pl.program_id(0)
    is_sample = i >= n_prompt_tiles
    n_chunks = r_ref.shape[0] // CHUNK

    lane = lax.broadcasted_iota(jnp.int32, (1, GROUP_LANES), 1)
    head_masks = [(lane // HEAD == h).astype(F32) for h in range(GROUP_HEADS)]
    ri = lax.broadcasted_iota(jnp.int32, (STACK, 2 * STACK), 0)
    ci = lax.broadcasted_iota(jnp.int32, (STACK, 2 * STACK), 1)
    same_head = (ri // CHUNK) == ((ci % STACK) // CHUNK)
    strict_lower = jnp.logical_and(same_head, (ci % CHUNK) < (ri % CHUNK))
    mask_incl = jnp.logical_and(same_head, (ci % CHUNK) <= (ri % CHUNK)).astype(F32)
    mask_strict_b = jnp.logical_and(strict_lower, ci < STACK).astype(F32)
    mask_strict_k = jnp.logical_and(strict_lower, ci >= STACK).astype(F32)
    eye = (ri == ci).astype(F32)
    ti = lax.broadcasted_iota(jnp.int32, (CHUNK, CHUNK), 0)
    si = lax.broadcasted_iota(jnp.int32, (CHUNK, CHUNK), 1)
    tri_incl = (si <= ti).astype(BF16)
    rb = lax.broadcasted_iota(jnp.int32, (GROUP_LANES, GROUP_LANES), 0)
    cb = lax.broadcasted_iota(jnp.int32, (GROUP_LANES, GROUP_LANES), 1)
    block_diag = ((rb // HEAD) == (cb // HEAD)).astype(F32)

    def stack(x):
        return jnp.concatenate([x * m for m in head_masks], axis=0)

    def unstack(x):
        out = x[0:CHUNK]
        for h in range(1, GROUP_HEADS):
            out = out + x[h * CHUNK:(h + 1) * CHUNK]
        return out

    def compact(s):
        out = s[0:HEAD]
        for h in range(1, GROUP_HEADS):
            out = out + s[h * HEAD:(h + 1) * HEAD]
        return out

    @pl.when(i == 0)
    def _():
        s_out_ref[...] = jnp.zeros_like(s_out_ref)

    groups = range(N_LANE_GROUPS)
    lanes = [slice(g * GROUP_LANES, (g + 1) * GROUP_LANES) for g in groups]

    def chunk_body(c, carry):
        row0 = pl.multiple_of(c * CHUNK, CHUNK)
        rows = pl.ds(row0, CHUNK)
        load_state = jnp.logical_or(is_sample, jnp.logical_and(i == 0, c == 0))
        slot = jnp.where(is_sample, c, 0)

        lw = [lw_ref[rows, lanes[g]] for g in groups]
        cum = [_dot_exact_lhs(tri_incl, lw[g]) for g in groups]
        lhs_s, bk_s, v_s, kr_t, p_end = [], [], [], [], []
        for g in groups:
            e_incl = jnp.exp(cum[g])
            e_excl = jnp.exp(cum[g] - lw[g])
            e_neg = jnp.exp(-cum[g])
            p_end.append(e_incl[CHUNK - 1:CHUNK, :])
            r_t = r_ref[rows, lanes[g]] * e_incl
            kk_t = kk_ref[rows, lanes[g]] * e_excl
            b_t = b_ref[rows, lanes[g]] * e_neg
            k_t = k_ref[rows, lanes[g]] * e_neg
            kr_t.append(jnp.concatenate([kk_t, r_t], axis=0))
            lhs_s.append(jnp.concatenate([stack(kk_t), stack(r_t)], axis=0))
            bk_s.append(jnp.concatenate([stack(b_t), stack(k_t)], axis=0))
            v_s.append(stack(v_ref[rows, lanes[g]]))
        mn = [_sdot(lhs_s[g], bk_s[g], _NT) for g in groups]
        m_ab = [mn[g][0:STACK] * mask_strict_b for g in groups]
        m_k = [mn[g][0:STACK] * mask_strict_k for g in groups]
        n_bk = [mn[g][STACK:] * mask_incl for g in groups]

        def twice(x):
            return jnp.concatenate([x, x], axis=0)

        m2 = [_sdot(m_ab[g], twice(m_ab[g])) for g in groups]
        m4 = [_sdot(m2[g], twice(m2[g])) for g in groups]
        t_inv = [_sdot(eye - m_ab[g], twice(eye + m2[g])) for g in groups]
        m8 = [_sdot(m4[g], twice(m4[g])) for g in groups]
        t_inv = [_sdot(t_inv[g], twice(eye + m4[g])) for g in groups]
        t_inv = [_sdot(t_inv[g], twice(eye + m8[g])) for g in groups]
        mv = [_sdot(m_k[g], twice(v_s[g])) for g in groups]

        s0 = []
        for g in groups:
            s_loaded = jnp.concatenate([s_in_ref[slot, :, lanes[g]]] * GROUP_HEADS, axis=0) * block_diag
            s0.append(jnp.where(load_state, s_loaded, s_scr[g]))
        gr = [_sdot(kr_t[g], s0[g], _NT) for g in groups]
        u_s = [-_sdot(t_inv[g], twice(stack(gr[g][0:CHUNK]) + mv[g])) for g in groups]
        uv = [jnp.concatenate([u_s[g], v_s[g]], axis=0) for g in groups]
        for g in groups:
            y_ref[rows, lanes[g]] = gr[g][CHUNK:] + unstack(_sdot(n_bk[g], uv[g]))
        for g in groups:
            s_new = s0[g] * p_end[g] + _sdot(uv[g], bk_s[g] * p_end[g], _TN)
            s_scr[g] = s_new
            s_out_ref[slot, :, lanes[g]] = compact(s_new)
        return carry

    lax.fori_loop(0, n_chunks, chunk_body, 0)


def _rwkv_scan(r, k, v, kk, b, lw, s_in, n_prompt):
    n = r.shape[0]
    n_prompt_tiles = n_prompt // SCAN_ROWS
    row_spec = pl.BlockSpec((SCAN_ROWS, D), lambda i: (i, 0))
    state_spec = pl.BlockSpec((STATE_SLOTS, HEAD, D),
                              lambda i: (jnp.maximum(i - n_prompt_tiles + 1, 0), 0, 0))
    return pl.pallas_call(
        functools.partial(_scan_kernel, n_prompt_tiles),
        grid=(n // SCAN_ROWS,),
        in_specs=[row_spec] * 6 + [state_spec],
        out_specs=[row_spec, state_spec],
        out_shape=[jax.ShapeDtypeStruct((n, D), F32), jax.ShapeDtypeStruct(s_in.shape, F32)],
        scratch_shapes=[pltpu.VMEM((N_LANE_GROUPS, GROUP_LANES, GROUP_LANES), F32)],
        compiler_params=_cparams(("arbitrary",)),
        name="rwkv_scan",
    )(r, k, v, kk, b, lw, s_in)


def _conv_kernel(n_prompt_tiles, cb_ref, cc_ref, ch_ref, ccc_ref, cch_ref, bnd1_ref, bnd2_ref,
                 cw_ref, wout_ref, o_ref, u_ref):
    i = pl.program_id(0)
    rows = cb_ref.shape[0]
    row = lax.broadcasted_iota(jnp.int32, (rows, 1), 0)
    is_sample = i >= n_prompt_tiles
    pos = row % SEQ_S
    u = cc_ref[...] * ch_ref[...]
    u_ref[...] = u
    u_prev = jnp.where(i == 0, 0.0, ccc_ref[...] * cch_ref[...])
    prev1 = pltpu.roll(u, 1, 0)
    prev1 = jnp.where(row == 0, u_prev[7:8, :], prev1)
    prev2 = pltpu.roll(u, 2, 0)
    prev2 = jnp.where(row == 0, u_prev[6:7, :], prev2)
    prev2 = jnp.where(row == 1, u_prev[7:8, :], prev2)
    bnd1 = bnd1_ref[...]
    prev1 = jnp.where(jnp.logical_and(is_sample, pos == 0), bnd1, prev1)
    prev2 = jnp.where(jnp.logical_and(is_sample, pos == 0), bnd2_ref[...], prev2)
    prev2 = jnp.where(jnp.logical_and(is_sample, pos == 1), bnd1, prev2)
    cw = cw_ref[...]
    conv = prev2 * cw[0:1, :] + prev1 * cw[1:2, :] + u * cw[2:3, :]
    o_ref[...] = _dot(cb_ref[...] * conv, wout_ref[...])


def _short_conv(p, bnd1, bnd2, conv_w, w_out, n_prompt):
    n = p.shape[0]
    n_prompt_tiles = n_prompt // TR
    cblk = COL_CONV // CONV_DIM
    carry_blk = TR // 8

    def row_spec(cb):
        return pl.BlockSpec((TR, CONV_DIM), lambda i: (i, cb))

    def carry_spec(cb):
        return pl.BlockSpec((8, CONV_DIM), lambda i: (jnp.maximum(i * carry_blk - 1, 0), cb))

    bnd_spec = pl.BlockSpec((TR, CONV_DIM), lambda i: (jnp.maximum(i - n_prompt_tiles, 0), 0))
    return pl.pallas_call(
        functools.partial(_conv_kernel, n_prompt_tiles),
        grid=(n // TR,),
        in_specs=[row_spec(cblk), row_spec(cblk + 1), row_spec(cblk + 2),
                  carry_spec(cblk + 1), carry_spec(cblk + 2), bnd_spec, bnd_spec,
                  pl.BlockSpec((8, CONV_DIM), lambda i: (0, 0)),
                  pl.BlockSpec((CONV_DIM, D), lambda i: (0, 0))],
        out_specs=[pl.BlockSpec((TR, D), lambda i: (i, 0)), pl.BlockSpec((TR, CONV_DIM), lambda i: (i, 0))],
        out_shape=[jax.ShapeDtypeStruct((n, D), F32), jax.ShapeDtypeStruct((n, CONV_DIM), F32)],
        compiler_params=_cparams(("arbitrary",)),
        name="short_conv",
    )(p, p, p, p, p, bnd1, bnd2, conv_w, w_out)


def _mem_kernel(q_ref, k_ref, v_ref, wo_ref, o_ref):
    q = q_ref[...]
    k = k_ref[0]
    v = v_ref[0]
    outs = []
    for h in range(MEM_HEADS):
        sl = slice(h * MEM_HEAD_DIM, (h + 1) * MEM_HEAD_DIM)
        s = _dot(q[:, sl], k[:, sl], _NT) * (MEM_HEAD_DIM ** -0.5)
        s = s - jnp.max(s, axis=-1, keepdims=True)
        e = jnp.exp(s)
        pr = e / jnp.sum(e, axis=-1, keepdims=True)
        outs.append(_dot(pr, v[:, sl]))
    o_ref[...] = _dot(jnp.concatenate(outs, axis=1), wo_ref[...])


def _mem_sample(p, mem_k, mem_v, w_o, row_start, n_seq):
    qblk = COL_Q // MEM_DIM
    rb0 = row_start // SEQ_S
    return pl.pallas_call(
        _mem_kernel,
        grid=(n_seq,),
        in_specs=[pl.BlockSpec((SEQ_S, MEM_DIM), lambda i: (rb0 + i, qblk)),
                  pl.BlockSpec((1, N_MEM, MEM_DIM), lambda i: (i, 0, 0)),
                  pl.BlockSpec((1, N_MEM, MEM_DIM), lambda i: (i, 0, 0)),
                  pl.BlockSpec((MEM_DIM, D), lambda i: (0, 0))],
        out_specs=pl.BlockSpec((SEQ_S, D), lambda i: (i, 0)),
        out_shape=jax.ShapeDtypeStruct((n_seq * SEQ_S, D), F32),
        compiler_params=_cparams(("arbitrary",)),
        name="mem_attention_sample",
    )(p, mem_k, mem_v, w_o)


MEM_TILE = 256


def _mem_prompt_kernel(n_prompt_tiles, q_ref, k_ref, v_ref, wo_ref, tail_ref, o_ref):
    i = pl.program_id(0)

    @pl.when(i < n_prompt_tiles)
    def _():
        _mem_kernel(q_ref, k_ref, v_ref, wo_ref, o_ref)

    @pl.when(i >= n_prompt_tiles)
    def _():
        o_ref[...] = tail_ref[...]


def _mem_attention(p, mem_k, mem_v, w_o, o_sample, n_prompt):
    n = p.shape[0]
    qblk = COL_Q // MEM_DIM
    n_prompt_tiles = n_prompt // MEM_TILE
    return pl.pallas_call(
        functools.partial(_mem_prompt_kernel, n_prompt_tiles),
        grid=(n // MEM_TILE,),
        in_specs=[pl.BlockSpec((MEM_TILE, MEM_DIM), lambda i: (jnp.minimum(i, n_prompt_tiles - 1), qblk)),
                  pl.BlockSpec((1, N_MEM, MEM_DIM), lambda i: (0, 0, 0)),
                  pl.BlockSpec((1, N_MEM, MEM_DIM), lambda i: (0, 0, 0)),
                  pl.BlockSpec((MEM_DIM, D), lambda i: (0, 0)),
                  pl.BlockSpec((MEM_TILE, D), lambda i: (jnp.maximum(i - n_prompt_tiles, 0), 0))],
        out_specs=pl.BlockSpec((MEM_TILE, D), lambda i: (i, 0)),
        out_shape=jax.ShapeDtypeStruct((n, D), F32),
        compiler_params=_cparams(("arbitrary",)),
        name="mem_attention",
    )(p, mem_k, mem_v, w_o, o_sample)


def _layer_norm(z, g, b):
    mu = jnp.mean(z, axis=-1, keepdims=True)
    d = z - mu
    var = jnp.mean(d * d, axis=-1, keepdims=True)
    return d * lax.rsqrt(var + LN_EPS) * g + b


def _merge_kernel(n_prompt_tiles, xp_ref, xs_ref, ga_ref, gb_ref, gm_ref, y_ref, bonus_ref, g_ref, ocv_ref, omem_ref,
                  gng_ref, gnb_ref, bd_ref, wo_ref, l1g_ref, l1b_ref, wr_ref, h_o, lt_o):
    x = jnp.where(pl.program_id(0) < n_prompt_tiles, xp_ref[...], xs_ref[...])
    bd = bd_ref[...]
    y = y_ref[...]
    mean = _head_sum(y, bd) * (1.0 / HEAD)
    d = y - mean
    var = _head_sum(d * d, bd) * (1.0 / HEAD)
    yn = d * lax.rsqrt(var + GN_EPS) * gng_ref[...] + gnb_ref[...]
    o_rw = (yn + bonus_ref[...]) * g_ref[...]
    merged = (_sigmoid(ga_ref[...]) * o_rw + _sigmoid(gb_ref[...]) * ocv_ref[...]
              + _sigmoid(gm_ref[...]) * omem_ref[...])
    z = ALPHA * x + _dot(merged, wo_ref[...])
    h = _layer_norm(z, l1g_ref[...], l1b_ref[...])
    h_o[...] = h
    lt_o[...] = _dot3(wr_ref[...], h, _NT)


def _merge_ln1(xp, xs, p, y_raw, bonus, g, o_cv, o_mem, gn_g, gn_b, bd, w_o, ln_g, ln_b, w_router_t):
    n = p.shape[0]
    n_prompt_tiles = xp.shape[0] // TR
    gblk = COL_GATE // D
    row = pl.BlockSpec((TR, D), lambda i: (i, 0))
    xp_spec = pl.BlockSpec((TR, D), lambda i: (jnp.minimum(i, n_prompt_tiles - 1), 0))
    xs_spec = pl.BlockSpec((TR, D), lambda i: (jnp.maximum(i - n_prompt_tiles, 0), 0))

    def gate_spec(j):
        return pl.BlockSpec((TR, D), lambda i: (i, gblk + j))

    def const_spec(shape):
        return pl.BlockSpec(shape, lambda i: (0,) * len(shape))

    vec = const_spec((1, D))
    return pl.pallas_call(
        functools.partial(_merge_kernel, n_prompt_tiles),
        grid=(n // TR,),
        in_specs=[xp_spec, xs_spec, gate_spec(0), gate_spec(1), gate_spec(2), row, row, row, row, row,
                  vec, vec, const_spec((GROUP_LANES, GROUP_LANES)), const_spec((D, D)), vec, vec,
                  const_spec((N_EXPERTS, D))],
        out_specs=[row, pl.BlockSpec((N_EXPERTS, TR), lambda i: (0, i))],
        out_shape=[jax.ShapeDtypeStruct((n, D), F32), jax.ShapeDtypeStruct((N_EXPERTS, n), F32)],
        compiler_params=_cparams(("arbitrary",)),
        name="merge_ln1",
    )(xp, xs, p, p, p, y_raw, bonus, g, o_cv, o_mem, gn_g, gn_b, bd, w_o, ln_g, ln_b, w_router_t)


ROUTE_TILE = 256


def _routing_kernel(lt_ref, bias_ref, tri_ref, idx_o, w_o, pos_o, cnt_o, carry):
    i = pl.program_id(0)
    tile = lt_ref.shape[1]

    @pl.when(i == 0)
    def _():
        carry[...] = jnp.zeros_like(carry)

    neg_inf = -jnp.inf
    scores = _sigmoid(lt_ref[...])
    choice = scores + bias_ref[...]
    row = lax.broadcasted_iota(jnp.int32, (N_EXPERTS, tile), 0)
    rowf = row.astype(F32)
    grpf = (row // GROUP_SIZE).astype(F32)

    def group_allreduce(x, op):
        for s in (1, 2, 4):
            up = pltpu.roll(x, N_EXPERTS - s, 0)
            dn = pltpu.roll(x, s, 0)
            x = op(x, jnp.where((row & s) == 0, up, dn))
        return x

    m1 = group_allreduce(choice, jnp.maximum)
    first = group_allreduce(jnp.where(choice == m1, rowf, float(N_EXPERTS)), jnp.minimum)
    m2 = group_allreduce(jnp.where(rowf == first, neg_inf, choice), jnp.maximum)
    gscore = m1 + m2

    gsel = jnp.zeros_like(choice)
    for _ in range(TOPK_GROUPS):
        gmax = jnp.max(gscore, axis=0, keepdims=True)
        pick = jnp.min(jnp.where(gscore == gmax, grpf, float(N_GROUPS)), axis=0, keepdims=True)
        hit = grpf == pick
        gsel = jnp.where(hit, 1.0, gsel)
        gscore = jnp.where(hit, neg_inf, gscore)

    masked = jnp.where(gsel > 0.0, choice, neg_inf)
    row8 = lax.broadcasted_iota(jnp.int32, (TOP_K, tile), 0)
    idx_acc = jnp.zeros((TOP_K, tile), F32)
    w_acc = jnp.zeros((TOP_K, tile), F32)
    sel_all = jnp.zeros_like(choice)
    for kk in range(TOP_K):
        mx = jnp.max(masked, axis=0, keepdims=True)
        pick = jnp.min(jnp.where(masked == mx, rowf, float(N_EXPERTS)), axis=0, keepdims=True)
        hit = rowf == pick
        wk = jnp.sum(jnp.where(hit, scores, 0.0), axis=0, keepdims=True)
        idx_acc = jnp.where(row8 == kk, pick, idx_acc)
        w_acc = jnp.where(row8 == kk, wk, w_acc)
        sel_all = jnp.where(hit, 1.0, sel_all)
        masked = jnp.where(hit, neg_inf, masked)

    w_sum = jnp.sum(w_acc, axis=0, keepdims=True)
    w_o[...] = w_acc / w_sum * ROUTED_SCALE
    idx_o[...] = idx_acc.astype(jnp.int32)

    prefix = lax.dot_general(sel_all.astype(BF16), tri_ref[...], _NN, preferred_element_type=F32) + carry[...]
    pos_acc = jnp.zeros((TOP_K, tile), F32)
    for kk in range(TOP_K):
        hit = rowf == idx_acc[kk:kk + 1, :]
        pk = jnp.sum(jnp.where(hit, prefix, 0.0), axis=0, keepdims=True)
        pos_acc = jnp.where(row8 == kk, pk, pos_acc)
    pos_o[...] = pos_acc.astype(jnp.int32)
    carry[...] = carry[...] + jnp.sum(sel_all, axis=1, keepdims=True)
    cnt_o[...] = carry[...]


def _routing(logits_t, bias_col, tri):
    n = logits_t.shape[1]
    tile = ROUTE_TILE
    tok = pl.BlockSpec((TOP_K, tile), lambda i: (0, i))
    return pl.pallas_call(
        _routing_kernel,
        grid=(n // tile,),
        in_specs=[pl.BlockSpec((N_EXPERTS, tile), lambda i: (0, i)),
                  pl.BlockSpec((N_EXPERTS, 1), lambda i: (0, 0)),
                  pl.BlockSpec((tile, tile), lambda i: (0, 0))],
        out_specs=[tok, tok, tok, pl.BlockSpec((N_EXPERTS, 1), lambda i: (0, 0))],
        out_shape=[jax.ShapeDtypeStruct((TOP_K, n), jnp.int32), jax.ShapeDtypeStruct((TOP_K, n), F32),
                   jax.ShapeDtypeStruct((TOP_K, n), jnp.int32), jax.ShapeDtypeStruct((N_EXPERTS, 1), F32)],
        scratch_shapes=[pltpu.VMEM((N_EXPERTS, 1), F32)],
        compiler_params=_cparams(("arbitrary",)),
        name="routing",
    )(logits_t, bias_col, tri)


def _dispatch_kernel(dest_ref, h_ref, xb_in, xb_out, dest_smem, sem, idx_sem):
    del xb_in
    rows = h_ref.shape[0]
    cp = pltpu.make_async_copy(dest_ref, dest_smem, idx_sem)
    cp.start()
    cp.wait()

    def row_copy(t, k):
        return pltpu.make_async_copy(h_ref.at[pl.ds(t, 1), :],
                                     xb_out.at[pl.ds(dest_smem[k, t], 1), :], sem)

    def issue(t, c):
        for k in range(TOP_K):
            row_copy(t, k).start()
        return c

    def drain(t, c):
        for k in range(TOP_K):
            row_copy(t, k).wait()
        return c

    lax.fori_loop(0, rows, issue, 0)
    lax.fori_loop(0, rows, drain, 0)


def _dispatch(dest_t, h, xb_init):
    n = h.shape[0]
    return pl.pallas_call(
        _dispatch_kernel,
        grid=(n // TR,),
        in_specs=[pl.BlockSpec((TOP_K, TR), lambda i: (0, i)),
                  pl.BlockSpec((TR, D), lambda i: (i, 0)),
                  pl.BlockSpec(memory_space=pl.ANY)],
        out_specs=pl.BlockSpec(memory_space=pl.ANY),
        out_shape=jax.ShapeDtypeStruct(xb_init.shape, F32),
        input_output_aliases={2: 0},
        scratch_shapes=[pltpu.SMEM((TOP_K, TR), jnp.int32), pltpu.SemaphoreType.DMA, pltpu.SemaphoreType.DMA],
        compiler_params=_cparams(("arbitrary",)),
        name="moe_dispatch",
    )(dest_t, h, xb_init)


def _silu(x):
    return x * _sigmoid(x)


def _expert_kernel(be_ref, nu_ref, x_ref, wu_ref, wd_ref, o_ref, wu_bf, wd_bf):
    b = pl.program_id(0)
    changed = jnp.logical_or(b == 0, be_ref[b] != be_ref[jnp.maximum(b - 1, 0)])

    @pl.when(changed)
    def _():
        wu_bf[...] = wu_ref[0].astype(BF16)
        wd_bf[...] = wd_ref[0].astype(BF16)

    @pl.when(b < nu_ref[0])
    def _():
        up = _dot(x_ref[...], wu_bf[...])
        act = _silu(up[:, :EXPERT_FF]) * up[:, EXPERT_FF:]
        o_ref[...] = _dot(act, wd_bf[...])

    @pl.when(b >= nu_ref[0])
    def _():
        o_ref[...] = jnp.zeros_like(o_ref)


def _experts(block_e, n_used, xb, w_up, w_down):
    rows = xb.shape[0]
    nb = rows // EXPERT_BM

    def xmap(b, be, nu):
        return (jnp.minimum(b, nu[0] - 1), 0)

    grid_spec = pltpu.PrefetchScalarGridSpec(
        num_scalar_prefetch=2,
        grid=(nb,),
        in_specs=[pl.BlockSpec((EXPERT_BM, D), xmap),
                  pl.BlockSpec((1, D, 2 * EXPERT_FF), lambda b, be, nu: (be[b], 0, 0)),
                  pl.BlockSpec((1, EXPERT_FF, D), lambda b, be, nu: (be[b], 0, 0))],
        out_specs=pl.BlockSpec((EXPERT_BM, D), lambda b, be, nu: (b, 0)),
        scratch_shapes=[pltpu.VMEM((D, 2 * EXPERT_FF), BF16), pltpu.VMEM((EXPERT_FF, D), BF16)],
    )
    return pl.pallas_call(
        _expert_kernel,
        grid_spec=grid_spec,
        out_shape=jax.ShapeDtypeStruct((rows, D), F32),
        compiler_params=_cparams(("arbitrary",)),
        name="moe_experts",
    )(block_e, n_used, xb, w_up, w_down)


SHARED_TILE = 512


def _shared_kernel(h_ref, wu_ref, wd_ref, o_ref):
    up = _dot(h_ref[...], wu_ref[...])
    act = _silu(up[:, :SHARED_FF]) * up[:, SHARED_FF:]
    o_ref[...] = _dot(act, wd_ref[...])


def _shared_ffn(h, w_up, w_down):
    n = h.shape[0]
    row = pl.BlockSpec((SHARED_TILE, D), lambda i: (i, 0))
    return pl.pallas_call(
        _shared_kernel,
        grid=(n // SHARED_TILE,),
        in_specs=[row, pl.BlockSpec((D, 2 * SHARED_FF), lambda i: (0, 0)),
                  pl.BlockSpec((SHARED_FF, D), lambda i: (0, 0))],
        out_specs=row,
        out_shape=jax.ShapeDtypeStruct((n, D), F32),
        compiler_params=_cparams(("parallel",)),
        name="shared_ffn",
    )(h, w_up, w_down)


def _combine_kernel(dest_ref, w_ref, h_ref, sh_ref, yb_ref, l2g_ref, l2b_ref, y_o, buf, dest_smem, sem, idx_sem):
    rows = h_ref.shape[0]
    cp = pltpu.make_async_copy(dest_ref, dest_smem, idx_sem)
    cp.start()
    cp.wait()

    def row_copy(t, k):
        return pltpu.make_async_copy(yb_ref.at[pl.ds(dest_smem[k, t], 1), :],
                                     buf.at[k, pl.ds(t, 1), :], sem)

    def issue(t, c):
        for k in range(TOP_K):
            row_copy(t, k).start()
        return c

    def drain(t, c):
        for k in range(TOP_K):
            row_copy(t, k).wait()
        return c

    lax.fori_loop(0, rows, issue, 0)
    lax.fori_loop(0, rows, drain, 0)
    w = w_ref[...]
    f = sh_ref[...]
    for k in range(TOP_K):
        f = f + w[:, k:k + 1] * buf[k]
    z = ALPHA * h_ref[...] + f
    y_o[...] = _layer_norm(z, l2g_ref[...], l2b_ref[...])


def _combine_ln2(dest_t, w_tok, h, shared, yb, ln_g, ln_b):
    n = h.shape[0]
    row = pl.BlockSpec((TR, D), lambda i: (i, 0))
    vec = pl.BlockSpec((1, D), lambda i: (0, 0))
    return pl.pallas_call(
        _combine_kernel,
        grid=(n // TR,),
        in_specs=[pl.BlockSpec((TOP_K, TR), lambda i: (0, i)),
                  pl.BlockSpec((TR, TOP_K), lambda i: (i, 0)),
                  row, row, pl.BlockSpec(memory_space=pl.ANY), vec, vec],
        out_specs=row,
        out_shape=jax.ShapeDtypeStruct((n, D), F32),
        scratch_shapes=[pltpu.VMEM((TOP_K, TR, D), F32), pltpu.SMEM((TOP_K, TR), jnp.int32),
                        pltpu.SemaphoreType.DMA, pltpu.SemaphoreType.DMA],
        compiler_params=_cparams(("arbitrary",)),
        name="moe_combine_ln2",
    )(dest_t, w_tok, h, shared, yb, ln_g, ln_b)


def _reorder_cols(w):
    pad = jnp.zeros(w.shape[:-1] + (LORA_PAD - LORA_W,), w.dtype)
    rw_end = RW_COLS
    cv_end = rw_end + 3 * CONV_DIM
    q_end = cv_end + MEM_DIM
    return jnp.concatenate(
        [w[..., 0:3 * D], w[..., q_end:q_end + 3 * D], w[..., rw_end:cv_end], w[..., cv_end:q_end],
         w[..., 3 * D:3 * D + LORA_W], pad,
         w[..., 3 * D + LORA_W:3 * D + LORA_W + LORA_A], pad,
         w[..., 3 * D + LORA_W + LORA_A:rw_end]], axis=-1)


def _rw_cols_split(v):
    pad = jnp.zeros(v.shape[:-1] + (LORA_PAD - LORA_W,), v.dtype)
    lora = jnp.concatenate([v[..., 3 * D:3 * D + LORA_W], pad,
                            v[..., 3 * D + LORA_W:3 * D + LORA_W + LORA_A], pad,
                            v[..., 3 * D + LORA_W + LORA_A:]], axis=-1)
    return v[..., 0:3 * D], lora


def _pad_rows(w, rows):
    return jnp.concatenate([w, jnp.zeros((rows - w.shape[0],) + w.shape[1:], w.dtype)], axis=0)


def kernel(x_prompt, x_sample, mem_prompt, state_rwkv, state_shift, state_conv, cache_mem_k, cache_mem_v,
           w_in, mu_shift, rw_w0, rw_w2, rw_a0, rw_a2, rw_g2, rw_k_k, rw_k_a, rw_r_k, rw_gn_g, rw_gn_b,
           conv_w, w_conv_out, w_mem_k, w_mem_v, w_mem_o, w_o, ln1_g, ln1_b, w_router, router_bias,
           w_exp_up, w_exp_down, w_sh_up, w_sh_down, ln2_g, ln2_b):
    n_prompt = x_prompt.shape[0] * x_prompt.shape[1]
    n_seq_s, seq_s = x_sample.shape[0], x_sample.shape[1]
    n_sample = n_seq_s * seq_s
    n = n_prompt + n_sample
    assert x_prompt.shape[0] == 1 and seq_s == SEQ_S and n_prompt % TR == 0 and n_sample % TR == 0
    assert n % SHARED_TILE == 0 and n % ROUTE_TILE == 0 and w_in.shape[0] == 1
    assert n_prompt % PROJ_TM == 0 and n_sample % PROJ_TM == 0

    xp = x_prompt.reshape(n_prompt, D)
    xs = x_sample.reshape(n_sample, D)

    def vec(v):
        return v.reshape(1, -1).astype(F32)

    w_in_r = _reorder_cols(w_in[0]).astype(BF16)
    p = _in_proj(xp, xs, w_in_r)

    w_kv = jnp.concatenate([w_mem_k[0], w_mem_v[0]], axis=1).astype(BF16)
    kv = _matmul(mem_prompt[0], w_kv, N_MEM, 512, "mem_kv")
    mem_k_p, mem_v_p = kv[:, :MEM_DIM], kv[:, MEM_DIM:]

    mu_rkv, mu_lora = _rw_cols_split(vec(mu_shift[0]))
    sh_rkv, sh_lora = _rw_cols_split(state_shift[0, :, 0, :])
    bnd_rkv = jnp.repeat(sh_rkv, seq_s, axis=0)
    bnd_lora = jnp.repeat(sh_lora, seq_s, axis=0)
    hi = lax.broadcasted_iota(jnp.int32, (GROUP_LANES, GROUP_LANES), 0) // HEAD
    hj = lax.broadcasted_iota(jnp.int32, (GROUP_LANES, GROUP_LANES), 1) // HEAD
    bd = (hi == hj).astype(BF16)
    r, k, v, kk, b, lw, g, bonus = _rwkv_prep(
        p, bnd_rkv, bnd_lora, mu_rkv, mu_lora, vec(rw_w0[0]), vec(rw_a0[0]), vec(rw_k_k[0]), vec(rw_k_a[0]),
        vec(rw_r_k[0]), _pad_rows(rw_w2[0], LORA_PAD).astype(BF16), _pad_rows(rw_a2[0], LORA_PAD).astype(BF16),
        rw_g2[0].astype(BF16), bd, n_prompt)

    s_sample = jnp.transpose(state_rwkv[0], (0, 2, 1, 3)).reshape(n_seq_s, HEAD, D)
    s_in = jnp.concatenate([jnp.zeros((STATE_SLOTS, HEAD, D), F32), s_sample.astype(F32)], axis=0)
    y_raw, s_out = _rwkv_scan(r, k, v, kk, b, lw, s_in, n_prompt)

    bnd1 = jnp.repeat(state_conv[0, :, 1, :], seq_s, axis=0)
    bnd2 = jnp.repeat(state_conv[0, :, 0, :], seq_s, axis=0)
    o_cv, u = _short_conv(p, bnd1, bnd2, _pad_rows(conv_w[0], 8), w_conv_out[0].astype(BF16), n_prompt)

    w_mem_o_b = w_mem_o[0].astype(BF16)
    o_mem_s = _mem_sample(p, cache_mem_k[0].reshape(n_seq_s, N_MEM, MEM_DIM),
                          cache_mem_v[0].reshape(n_seq_s, N_MEM, MEM_DIM), w_mem_o_b, n_prompt, n_seq_s)
    o_mem = _mem_attention(p, mem_k_p[None], mem_v_p[None], w_mem_o_b, o_mem_s, n_prompt)

    h, logits_t = _merge_ln1(xp, xs, p, y_raw, bonus, g, o_cv, o_mem, vec(rw_gn_g[0]), vec(rw_gn_b[0]), bd,
                             w_o[0].astype(BF16), vec(ln1_g[0]), vec(ln1_b[0]), w_router[0].T)

    ti = lax.broadcasted_iota(jnp.int32, (ROUTE_TILE, ROUTE_TILE), 0)
    tj = lax.broadcasted_iota(jnp.int32, (ROUTE_TILE, ROUTE_TILE), 1)
    tri = (ti < tj).astype(BF16)
    idx_t, w_t, pos_t, counts = _routing(logits_t, router_bias[0].reshape(N_EXPERTS, 1).astype(F32), tri)

    counts = counts[:, 0].astype(jnp.int32)
    padded = (counts + EXPERT_BM - 1) // EXPERT_BM * EXPERT_BM
    seg_end = jnp.cumsum(padded)
    seg_start = seg_end - padded
    expert_ids = jnp.arange(N_EXPERTS, dtype=jnp.int32)
    dest_t = pos_t + jnp.sum(
        jnp.where(idx_t[None] == expert_ids[:, None, None], seg_start[:, None, None], 0), axis=0)
    nb = (n * TOP_K) // EXPERT_BM + N_EXPERTS
    block_rows = jnp.arange(nb, dtype=jnp.int32) * EXPERT_BM
    block_e = jnp.minimum(jnp.sum((seg_end[None, :] <= block_rows[:, None]).astype(jnp.int32), axis=1),
                          N_EXPERTS - 1)
    n_used = (seg_end[-1:] // EXPERT_BM).astype(jnp.int32)

    xb = _dispatch(dest_t, h, jnp.zeros((nb * EXPERT_BM, D), F32))
    yb = _experts(block_e, n_used, xb, w_exp_up[0], w_exp_down[0])
    shared = _shared_ffn(h, w_sh_up[0].astype(BF16), w_sh_down[0].astype(BF16))
    y = _combine_ln2(dest_t, w_t.T, h, shared, yb, vec(ln2_g[0]), vec(ln2_b[0]))

    dt = x_prompt.dtype
    y_p = y[:n_prompt].reshape(x_prompt.shape)
    y_s = y[n_prompt:].reshape(x_sample.shape)

    def state_out(s):
        q = s.reshape(s.shape[0], HEAD, N_HEADS, HEAD)
        return jnp.transpose(q, (0, 2, 1, 3))[None].astype(dt)

    rw_p = state_out(s_out[0:1])
    rw_s = state_out(s_out[STATE_SLOTS:])

    last_rows = jnp.concatenate([jnp.array([n_prompt - 1], jnp.int32),
                                 n_prompt + seq_s - 1 + seq_s * jnp.arange(n_seq_s, dtype=jnp.int32)])
    p_last = p[last_rows]
    shift = jnp.concatenate([p_last[:, 0:3 * D],
                             p_last[:, COL_LORA:COL_LORA + LORA_W],
                             p_last[:, COL_LORA + LORA_PAD:COL_LORA + LORA_PAD + LORA_A],
                             p_last[:, COL_LORA + 2 * LORA_PAD:]], axis=1)
    sh_p = shift[0:1].reshape(1, 1, 1, RW_COLS)
    sh_s = shift[1:].reshape(1, n_seq_s, 1, RW_COLS)

    cv_p = u[n_prompt - 2:n_prompt].reshape(1, 1, 2, CONV_DIM)
    cv_s = u[n_prompt:].reshape(n_seq_s, seq_s, CONV_DIM)[:, seq_s - 2:, :][None]

    mk_p = mem_k_p.reshape(1, 1, N_MEM, MEM_HEADS, MEM_HEAD_DIM)
    mv_p = mem_v_p.reshape(1, 1, N_MEM, MEM_HEADS, MEM_HEAD_DIM)
    return (y_p, y_s, rw_p, sh_p, cv_p, mk_p, mv_p, rw_s, sh_s, cv_s)
```

```python
import functools

import jax
import jax.numpy as jnp
from jax import lax
from jax.experimental import pallas as pl
from jax.experimental.pallas import tpu as pltpu

F32 = jnp.float32
BF16 = jnp.bfloat16

D = 2048
HEAD = 64
N_HEADS = D // HEAD
LORA_W = 96
LORA_A = 96
LORA_G = 256
DECAY_SCALE = 0.6065306597126334
GN_EPS = HEAD * 1e-5
CONV_DIM = D // 2
N_MEM = 256
MEM_HEADS = 4
MEM_HEAD_DIM = 256
MEM_DIM = MEM_HEADS * MEM_HEAD_DIM
N_EXPERTS = 64
N_GROUPS = 8
GROUP_SIZE = N_EXPERTS // N_GROUPS
TOPK_GROUPS = 4
TOP_K = 8
EXPERT_FF = 512
SHARED_FF = 512
ROUTED_SCALE = 2.5
LN_EPS = 1e-5
DEPTH = 1
ALPHA = (2 * DEPTH) ** 0.25
RW_COLS = 3 * D + LORA_W + LORA_A + LORA_G

LORA_PAD = 128
LORA_COLS = 2 * LORA_PAD + LORA_G
COL_RKV = 0
COL_GATE = 3 * D
COL_CONV = 6 * D
COL_Q = COL_CONV + 3 * CONV_DIM
COL_LORA = COL_Q + MEM_DIM
P_COLS = COL_LORA + LORA_COLS

CHUNK = 16
GROUP_HEADS = 4
GROUP_LANES = GROUP_HEADS * HEAD
N_LANE_GROUPS = D // GROUP_LANES
STACK = GROUP_HEADS * CHUNK
SEQ_S = 16
STATE_SLOTS = 8

TR = 128
SCAN_ROWS = STATE_SLOTS * CHUNK
EXPERT_BM = 512
VMEM_LIMIT = 56 * 1024 * 1024


def _cparams(sem):
    return pltpu.CompilerParams(dimension_semantics=sem, vmem_limit_bytes=VMEM_LIMIT)


def _sigmoid(x):
    return 1.0 / (1.0 + jnp.exp(-x))


def _dot(a, b, dims=(((1,), (0,)), ((), ()))):
    return lax.dot_general(a.astype(BF16), b.astype(BF16), dims, preferred_element_type=F32)


_NN = (((1,), (0,)), ((), ()))
_NT = (((1,), (1,)), ((), ()))
_TN = (((0,), (0,)), ((), ()))


def _split2(x):
    hi = x.astype(BF16)
    lo = (x - hi.astype(F32)).astype(BF16)
    return hi, lo


def _split3(x):
    hi = x.astype(BF16)
    r1 = x - hi.astype(F32)
    mid = r1.astype(BF16)
    lo = (r1 - mid.astype(F32)).astype(BF16)
    return hi, mid, lo


def _dot3(a, b, dims=_NN):
    ah, al = _split2(a)
    bh, bl = _split2(b)
    f = functools.partial(lax.dot_general, dimension_numbers=dims, preferred_element_type=F32)
    return f(ah, bh) + (f(ah, bl) + f(al, bh))


def _dot_exact_rhs(a, b_bf16, dims=_NN):
    hi, mid, lo = _split3(a)
    f = functools.partial(lax.dot_general, dimension_numbers=dims, preferred_element_type=F32)
    return f(hi, b_bf16) + (f(mid, b_bf16) + f(lo, b_bf16))


def _dot_exact_lhs(a_bf16, b):
    hi, mid, lo = _split3(b)
    f = functools.partial(lax.dot_general, dimension_numbers=_NN, preferred_element_type=F32)
    return f(a_bf16, hi) + (f(a_bf16, mid) + f(a_bf16, lo))


_sdot = _dot


def _mm_kernel(x_ref, w_ref, o_ref):
    o_ref[...] = _dot(x_ref[...], w_ref[...]).astype(o_ref.dtype)


def _matmul(x, w, tm, tn, name):
    m, k = x.shape
    n = w.shape[1]
    return pl.pallas_call(
        _mm_kernel,
        grid=(m // tm, n // tn),
        in_specs=[pl.BlockSpec((tm, k), lambda i, j: (i, 0)),
                  pl.BlockSpec((k, tn), lambda i, j: (0, j))],
        out_specs=pl.BlockSpec((tm, tn), lambda i, j: (i, j)),
        out_shape=jax.ShapeDtypeStruct((m, n), F32),
        compiler_params=_cparams(("parallel", "arbitrary")),
        name=name,
    )(x, w)


W_ROWS = 128


def _w_relayout_kernel(w_ref, o_ref):
    rw_end = RW_COLS
    cv_end = rw_end + 3 * CONV_DIM
    q_end = cv_end + MEM_DIM
    lo_w = 3 * D
    lo_a = lo_w + LORA_W
    lo_g = lo_a + LORA_A

    def put(dst, src, width):
        o_ref[:, dst:dst + width] = w_ref[:, src:src + width].astype(BF16)

    put(COL_RKV, 0, 3 * D)
    put(COL_GATE, q_end, 3 * D)
    put(COL_CONV, rw_end, 3 * CONV_DIM)
    put(COL_Q, cv_end, MEM_DIM)
    rows = o_ref.shape[0]
    lane = lax.broadcasted_iota(jnp.int32, (rows, LORA_PAD), 1)
    for dst, src, width in ((COL_LORA, lo_w, LORA_W), (COL_LORA + LORA_PAD, lo_a, LORA_A)):
        tile = w_ref[:, src:src + LORA_PAD]
        o_ref[:, dst:dst + LORA_PAD] = jnp.where(lane < width, tile, 0.0).astype(BF16)
    put(COL_LORA + 2 * LORA_PAD, lo_g, LORA_G)


def _w_relayout(w):
    k, cols = w.shape
    return pl.pallas_call(
        _w_relayout_kernel,
        grid=(k // W_ROWS,),
        in_specs=[pl.BlockSpec((W_ROWS, cols), lambda i: (i, 0))],
        out_specs=pl.BlockSpec((W_ROWS, P_COLS), lambda i: (i, 0)),
        out_shape=jax.ShapeDtypeStruct((k, P_COLS), BF16),
        compiler_params=_cparams(("parallel",)),
        name="w_in_relayout",
    )(w)


PROJ_TM = 512
PROJ_TN = 1536


def _in_proj_kernel(n_prompt_tiles, xp_ref, xs_ref, w_ref, o_ref, x_bf):
    i = pl.program_id(0)

    @pl.when(pl.program_id(1) == 0)
    def _():
        x_bf[...] = jnp.where(i < n_prompt_tiles, xp_ref[...], xs_ref[...]).astype(BF16)

    o_ref[...] = jnp.dot(x_bf[...], w_ref[...], preferred_element_type=F32)


def _in_proj(xp, xs, w):
    k = xp.shape[1]
    n_prompt_tiles = xp.shape[0] // PROJ_TM
    n_tiles = n_prompt_tiles + xs.shape[0] // PROJ_TM
    ncols = w.shape[1]
    return pl.pallas_call(
        functools.partial(_in_proj_kernel, n_prompt_tiles),
        grid=(n_tiles, ncols // PROJ_TN),
        in_specs=[pl.BlockSpec((PROJ_TM, k), lambda i, j: (jnp.minimum(i, n_prompt_tiles - 1), 0)),
                  pl.BlockSpec((PROJ_TM, k), lambda i, j: (jnp.maximum(i - n_prompt_tiles, 0), 0)),
                  pl.BlockSpec((k, PROJ_TN), lambda i, j: (0, j))],
        out_specs=pl.BlockSpec((PROJ_TM, PROJ_TN), lambda i, j: (i, j)),
        out_shape=jax.ShapeDtypeStruct((n_tiles * PROJ_TM, ncols), F32),
        scratch_shapes=[pltpu.VMEM((PROJ_TM, k), BF16)],
        compiler_params=_cparams(("arbitrary", "arbitrary")),
        name="in_proj",
    )(xp, xs, w)


def _head_sum(x, bd):
    parts = []
    for g in range(N_LANE_GROUPS):
        parts.append(_dot_exact_rhs(x[:, g * GROUP_LANES:(g + 1) * GROUP_LANES], bd))
    return jnp.concatenate(parts, axis=1)


def _prep_kernel(n_prompt_tiles, rkv_ref, lora_ref, c_rkv_ref, c_lora_ref, b_rkv_ref, b_lora_ref,
                 mu_rkv_ref, mu_lora_ref, w0_ref, a0_ref, kk_ref, ka_ref, rk_ref,
                 w2_ref, a2_ref, g2_ref, bd_ref,
                 r_o, k_o, v_o, kk_o, b_o, lw_o, g_o, bonus_o):
    i = pl.program_id(0)
    rows = rkv_ref.shape[0]
    row = lax.broadcasted_iota(jnp.int32, (rows, 1), 0)
    is_sample = i >= n_prompt_tiles
    seq_start = jnp.logical_and(is_sample, (row % SEQ_S) == 0)

    def mixed(x, carry_row, bnd, mu):
        prev = pltpu.roll(x, 1, 0)
        carry_row = jnp.where(i == 0, 0.0, carry_row)
        prev = jnp.where(row == 0, carry_row, prev)
        prev = jnp.where(seq_start, bnd, prev)
        return x + (prev - x) * mu

    def section(s):
        sl = slice(s * D, (s + 1) * D)
        return mixed(rkv_ref[:, sl], c_rkv_ref[7:8, sl], b_rkv_ref[:, sl], mu_rkv_ref[:, sl])

    lo = mixed(lora_ref[...], c_lora_ref[7:8, :], b_lora_ref[...], mu_lora_ref[...])
    w_lo = lo[:, 0:LORA_PAD]
    a_lo = lo[:, LORA_PAD:2 * LORA_PAD]
    g_lo = lo[:, 2 * LORA_PAD:]
    log_w = -DECAY_SCALE * _sigmoid(w0_ref[...] + _dot(jnp.tanh(w_lo), w2_ref[...]))
    a = _sigmoid(a0_ref[...] + _dot(a_lo, a2_ref[...]))
    g_o[...] = _dot(_sigmoid(g_lo), g2_ref[...])
    lw_o[...] = log_w

    bd = bd_ref[...]
    k = section(1)
    kk = k * kk_ref[...]
    ss = _head_sum(kk * kk, bd)
    kk = kk * lax.rsqrt(jnp.maximum(ss, 1e-24))
    kk_o[...] = kk
    b_o[...] = kk * a
    k = k * (1.0 + (a - 1.0) * ka_ref[...])
    k_o[...] = k
    r = section(0)
    r_o[...] = r
    v = section(2)
    v_o[...] = v
    bonus_o[...] = _head_sum(r * k * rk_ref[...], bd) * v


def _rwkv_prep(p, bnd_rkv, bnd_lora, mu_rkv, mu_lora, w0, a0, k_k, k_a, r_k, w2p, a2p, g2, bd, n_prompt):
    n = p.shape[0]
    n_prompt_tiles = n_prompt // TR
    carry_blk = TR // 8
    lora_blk = COL_LORA // LORA_COLS

    def row_spec(cols, cb=0):
        return pl.BlockSpec((TR, cols), lambda i: (i, cb))

    def carry_spec(cols, cb=0):
        return pl.BlockSpec((8, cols), lambda i: (jnp.maximum(i * carry_blk - 1, 0), cb))

    def bnd_spec(cols):
        return pl.BlockSpec((TR, cols), lambda i: (jnp.maximum(i - n_prompt_tiles, 0), 0))

    def const_spec(shape):
        return pl.BlockSpec(shape, lambda i: (0,) * len(shape))

    out = jax.ShapeDtypeStruct((n, D), F32)
    return pl.pallas_call(
        functools.partial(_prep_kernel, n_prompt_tiles),
        grid=(n // TR,),
        in_specs=[row_spec(3 * D), row_spec(LORA_COLS, lora_blk),
                  carry_spec(3 * D), carry_spec(LORA_COLS, lora_blk),
                  bnd_spec(3 * D), bnd_spec(LORA_COLS),
                  const_spec((1, 3 * D)), const_spec((1, LORA_COLS)),
                  const_spec((1, D)), const_spec((1, D)), const_spec((1, D)), const_spec((1, D)),
                  const_spec((1, D)),
                  const_spec((LORA_PAD, D)), const_spec((LORA_PAD, D)), const_spec((LORA_G, D)),
                  const_spec((GROUP_LANES, GROUP_LANES))],
        out_specs=[row_spec(D)] * 8,
        out_shape=[out] * 8,
        compiler_params=_cparams(("arbitrary",)),
        name="rwkv_prep",
    )(p, p, p, p, bnd_rkv, bnd_lora, mu_rkv, mu_lora, w0, a0, k_k, k_a, r_k, w2p, a2p, g2, bd)


def _scan_kernel(n_prompt_tiles, r_ref, k_ref, v_ref, kk_ref, b_ref, lw_ref, s_in_ref, y_ref, s_out_ref, s_scr):
    i = pl.program_id(0)
    is_sample = i >= n_prompt_tiles
    n_chunks = r_ref.shape[0] // CHUNK

    lane = lax.broadcasted_iota(jnp.int32, (1, GROUP_LANES), 1)
    head_masks = [(lane // HEAD == h).astype(F32) for h in range(GROUP_HEADS)]
    ri = lax.broadcasted_iota(jnp.int32, (STACK, 2 * STACK), 0)
    ci = lax.broadcasted_iota(jnp.int32, (STACK, 2 * STACK), 1)
    same_head = (ri // CHUNK) == ((ci % STACK) // CHUNK)
    strict_lower = jnp.logical_and(same_head, (ci % CHUNK) < (ri % CHUNK))
    mask_incl = jnp.logical_and(same_head, (ci % CHUNK) <= (ri % CHUNK)).astype(F32)
    mask_strict_b = jnp.logical_and(strict_lower, ci < STACK).astype(F32)
    mask_strict_k = jnp.logical_and(strict_lower, ci >= STACK).astype(F32)
    eye = (ri == ci).astype(F32)
    ti = lax.broadcasted_iota(jnp.int32, (CHUNK, CHUNK), 0)
    si = lax.broadcasted_iota(jnp.int32, (CHUNK, CHUNK), 1)
    tri_incl = (si <= ti).astype(BF16)
    rb = lax.broadcasted_iota(jnp.int32, (GROUP_LANES, GROUP_LANES), 0)
    cb = lax.broadcasted_iota(jnp.int32, (GROUP_LANES, GROUP_LANES), 1)
    block_diag = ((rb // HEAD) == (cb // HEAD)).astype(F32)

    def stack(x):
        return jnp.concatenate([x * m for m in head_masks], axis=0)

    def unstack(x):
        out = x[0:CHUNK]
        for h in range(1, GROUP_HEADS):
            out = out + x[h * CHUNK:(h + 1) * CHUNK]
        return out

    def compact(s):
        out = s[0:HEAD]
        for h in range(1, GROUP_HEADS):
            out = out + s[h * HEAD:(h + 1) * HEAD]
        return out

    @pl.when(i == 0)
    def _():
        s_out_ref[...] = jnp.zeros_like(s_out_ref)

    groups = range(N_LANE_GROUPS)
    lanes = [slice(g * GROUP_LANES, (g + 1) * GROUP_LANES) for g in groups]

    def chunk_body(c, carry):
        row0 = pl.multiple_of(c * CHUNK, CHUNK)
        rows = pl.ds(row0, CHUNK)
        load_state = jnp.logical_or(is_sample, jnp.logical_and(i == 0, c == 0))
        slot = jnp.where(is_sample, c, 0)

        lw = [lw_ref[rows, lanes[g]] for g in groups]
        cum = [_dot_exact_lhs(tri_incl, lw[g]) for g in groups]
        lhs_s, bk_s, v_s, kr_t, p_end = [], [], [], [], []
        for g in groups:
            e_incl = jnp.exp(cum[g])
            e_excl = jnp.exp(cum[g] - lw[g])
            e_neg = jnp.exp(-cum[g])
            p_end.append(e_incl[CHUNK - 1:CHUNK, :])
            r_t = r_ref[rows, lanes[g]] * e_incl
            kk_t = kk_ref[rows, lanes[g]] * e_excl
            b_t = b_ref[rows, lanes[g]] * e_neg
            k_t = k_ref[rows, lanes[g]] * e_neg
            kr_t.append(jnp.concatenate([kk_t, r_t], axis=0))
            lhs_s.append(jnp.concatenate([stack(kk_t), stack(r_t)], axis=0))
            bk_s.append(jnp.concatenate([stack(b_t), stack(k_t)], axis=0))
            v_s.append(stack(v_ref[rows, lanes[g]]))
        mn = [_sdot(lhs_s[g], bk_s[g], _NT) for g in groups]
        m_ab = [mn[g][0:STACK] * mask_strict_b for g in groups]
        m_k = [mn[g][0:STACK] * mask_strict_k for g in groups]
        n_bk = [mn[g][STACK:] * mask_incl for g in groups]

        def twice(x):
            return jnp.concatenate([x, x], axis=0)

        m2 = [_sdot(m_ab[g], twice(m_ab[g])) for g in groups]
        m4 = [_sdot(m2[g], twice(m2[g])) for g in groups]
        t_inv = [_sdot(eye - m_ab[g], twice(eye + m2[g])) for g in groups]
        m8 = [_sdot(m4[g], twice(m4[g])) for g in groups]
        t_inv = [_sdot(t_inv[g], twice(eye + m4[g])) for g in groups]
        t_inv = [_sdot(t_inv[g], twice(eye + m8[g])) for g in groups]
        mv = [_sdot(m_k[g], twice(v_s[g])) for g in groups]

        s0 = []
        for g in groups:
            s_loaded = jnp.concatenate([s_in_ref[slot, :, lanes[g]]] * GROUP_HEADS, axis=0) * block_diag
            s0.append(jnp.where(load_state, s_loaded, s_scr[g]))
        gr = [_sdot(kr_t[g], s0[g], _NT) for g in groups]
        u_s = [-_sdot(t_inv[g], twice(stack(gr[g][0:CHUNK]) + mv[g])) for g in groups]
        uv = [jnp.concatenate([u_s[g], v_s[g]], axis=0) for g in groups]
        for g in groups:
            y_ref[rows, lanes[g]] = gr[g][CHUNK:] + unstack(_sdot(n_bk[g], uv[g]))
        for g in groups:
            s_new = s0[g] * p_end[g] + _sdot(uv[g], bk_s[g] * p_end[g], _TN)
            s_scr[g] = s_new
            s_out_ref[slot, :, lanes[g]] = compact(s_new)
        return carry

    lax.fori_loop(0, n_chunks, chunk_body, 0)


def _rwkv_scan(r, k, v, kk, b, lw, s_in, n_prompt):
    n = r.shape[0]
    n_prompt_tiles = n_prompt // SCAN_ROWS
    row_spec = pl.BlockSpec((SCAN_ROWS, D), lambda i: (i, 0))
    state_spec = pl.BlockSpec((STATE_SLOTS, HEAD, D),
                              lambda i: (jnp.maximum(i - n_prompt_tiles + 1, 0), 0, 0))
    return pl.pallas_call(
        functools.partial(_scan_kernel, n_prompt_tiles),
        grid=(n // SCAN_ROWS,),
        in_specs=[row_spec] * 6 + [state_spec],
        out_specs=[row_spec, state_spec],
        out_shape=[jax.ShapeDtypeStruct((n, D), F32), jax.ShapeDtypeStruct(s_in.shape, F32)],
        scratch_shapes=[pltpu.VMEM((N_LANE_GROUPS, GROUP_LANES, GROUP_LANES), F32)],
        compiler_params=_cparams(("arbitrary",)),
        name="rwkv_scan",
    )(r, k, v, kk, b, lw, s_in)


def _conv_kernel(n_prompt_tiles, cb_ref, cc_ref, ch_ref, ccc_ref, cch_ref, bnd1_ref, bnd2_ref,
                 cw_ref, wout_ref, o_ref, u_ref):
    i = pl.program_id(0)
    rows = cb_ref.shape[0]
    row = lax.broadcasted_iota(jnp.int32, (rows, 1), 0)
    is_sample = i >= n_prompt_tiles
    pos = row % SEQ_S
    u = cc_ref[...] * ch_ref[...]
    u_ref[...] = u
    u_prev = jnp.where(i == 0, 0.0, ccc_ref[...] * cch_ref[...])
    prev1 = pltpu.roll(u, 1, 0)
    prev1 = jnp.where(row == 0, u_prev[7:8, :], prev1)
    prev2 = pltpu.roll(u, 2, 0)
    prev2 = jnp.where(row == 0, u_prev[6:7, :], prev2)
    prev2 = jnp.where(row == 1, u_prev[7:8, :], prev2)
    bnd1 = bnd1_ref[...]
    prev1 = jnp.where(jnp.logical_and(is_sample, pos == 0), bnd1, prev1)
    prev2 = jnp.where(jnp.logical_and(is_sample, pos == 0), bnd2_ref[...], prev2)
    prev2 = jnp.where(jnp.logical_and(is_sample, pos == 1), bnd1, prev2)
    cw = cw_ref[...]
    conv = prev2 * cw[0:1, :] + prev1 * cw[1:2, :] + u * cw[2:3, :]
    o_ref[...] = _dot(cb_ref[...] * conv, wout_ref[...])


def _short_conv(p, bnd1, bnd2, conv_w, w_out, n_prompt):
    n = p.shape[0]
    n_prompt_tiles = n_prompt // TR
    cblk = COL_CONV // CONV_DIM
    carry_blk = TR // 8

    def row_spec(cb):
        return pl.BlockSpec((TR, CONV_DIM), lambda i: (i, cb))

    def carry_spec(cb):
        return pl.BlockSpec((8, CONV_DIM), lambda i: (jnp.maximum(i * carry_blk - 1, 0), cb))

    bnd_spec = pl.BlockSpec((TR, CONV_DIM), lambda i: (jnp.maximum(i - n_prompt_tiles, 0), 0))
    return pl.pallas_call(
        functools.partial(_conv_kernel, n_prompt_tiles),
        grid=(n // TR,),
        in_specs=[row_spec(cblk), row_spec(cblk + 1), row_spec(cblk + 2),
                  carry_spec(cblk + 1), carry_spec(cblk + 2), bnd_spec, bnd_spec,
                  pl.BlockSpec((8, CONV_DIM), lambda i: (0, 0)),
                  pl.BlockSpec((CONV_DIM, D), lambda i: (0, 0))],
        out_specs=[pl.BlockSpec((TR, D), lambda i: (i, 0)), pl.BlockSpec((TR, CONV_DIM), lambda i: (i, 0))],
        out_shape=[jax.ShapeDtypeStruct((n, D), F32), jax.ShapeDtypeStruct((n, CONV_DIM), F32)],
        compiler_params=_cparams(("arbitrary",)),
        name="short_conv",
    )(p, p, p, p, p, bnd1, bnd2, conv_w, w_out)


def _mem_kernel(q_ref, k_ref, v_ref, wo_ref, o_ref):
    q = q_ref[...]
    k = k_ref[0]
    v = v_ref[0]
    outs = []
    for h in range(MEM_HEADS):
        sl = slice(h * MEM_HEAD_DIM, (h + 1) * MEM_HEAD_DIM)
        s = _dot(q[:, sl], k[:, sl], _NT) * (MEM_HEAD_DIM ** -0.5)
        s = s - jnp.max(s, axis=-1, keepdims=True)
        e = jnp.exp(s)
        pr = e / jnp.sum(e, axis=-1, keepdims=True)
        outs.append(_dot(pr, v[:, sl]))
    o_ref[...] = _dot(jnp.concatenate(outs, axis=1), wo_ref[...])


def _mem_sample(p, mem_k, mem_v, w_o, row_start, n_seq):
    qblk = COL_Q // MEM_DIM
    rb0 = row_start // SEQ_S
    return pl.pallas_call(
        _mem_kernel,
        grid=(n_seq,),
        in_specs=[pl.BlockSpec((SEQ_S, MEM_DIM), lambda i: (rb0 + i, qblk)),
                  pl.BlockSpec((1, N_MEM, MEM_DIM), lambda i: (i, 0, 0)),
                  pl.BlockSpec((1, N_MEM, MEM_DIM), lambda i: (i, 0, 0)),
                  pl.BlockSpec((MEM_DIM, D), lambda i: (0, 0))],
        out_specs=pl.BlockSpec((SEQ_S, D), lambda i: (i, 0)),
        out_shape=jax.ShapeDtypeStruct((n_seq * SEQ_S, D), F32),
        compiler_params=_cparams(("arbitrary",)),
        name="mem_attention_sample",
    )(p, mem_k, mem_v, w_o)


MEM_TILE = 256


def _mem_prompt_kernel(n_prompt_tiles, q_ref, k_ref, v_ref, wo_ref, tail_ref, o_ref):
    i = pl.program_id(0)

    @pl.when(i < n_prompt_tiles)
    def _():
        _mem_kernel(q_ref, k_ref, v_ref, wo_ref, o_ref)

    @pl.when(i >= n_prompt_tiles)
    def _():
        o_ref[...] = tail_ref[...]


def _mem_attention(p, mem_k, mem_v, w_o, o_sample, n_prompt):
    n = p.shape[0]
    qblk = COL_Q // MEM_DIM
    n_prompt_tiles = n_prompt // MEM_TILE
    return pl.pallas_call(
        functools.partial(_mem_prompt_kernel, n_prompt_tiles),
        grid=(n // MEM_TILE,),
        in_specs=[pl.BlockSpec((MEM_TILE, MEM_DIM), lambda i: (jnp.minimum(i, n_prompt_tiles - 1), qblk)),
                  pl.BlockSpec((1, N_MEM, MEM_DIM), lambda i: (0, 0, 0)),
                  pl.BlockSpec((1, N_MEM, MEM_DIM), lambda i: (0, 0, 0)),
                  pl.BlockSpec((MEM_DIM, D), lambda i: (0, 0)),
                  pl.BlockSpec((MEM_TILE, D), lambda i: (jnp.maximum(i - n_prompt_tiles, 0), 0))],
        out_specs=pl.BlockSpec((MEM_TILE, D), lambda i: (i, 0)),
        out_shape=jax.ShapeDtypeStruct((n, D), F32),
        compiler_params=_cparams(("arbitrary",)),
        name="mem_attention",
    )(p, mem_k, mem_v, w_o, o_sample)


def _layer_norm(z, g, b):
    mu = jnp.mean(z, axis=-1, keepdims=True)
    d = z - mu
    var = jnp.mean(d * d, axis=-1, keepdims=True)
    return d * lax.rsqrt(var + LN_EPS) * g + b


def _merge_kernel(n_prompt_tiles, xp_ref, xs_ref, ga_ref, gb_ref, gm_ref, y_ref, bonus_ref, g_ref, ocv_ref, omem_ref,
                  gng_ref, gnb_ref, bd_ref, wo_ref, l1g_ref, l1b_ref, wr_ref, h_o, lt_o):
    x = jnp.where(pl.program_id(0) < n_prompt_tiles, xp_ref[...], xs_ref[...])
    bd = bd_ref[...]
    y = y_ref[...]
    mean = _head_sum(y, bd) * (1.0 / HEAD)
    d = y - mean
    var = _head_sum(d * d, bd) * (1.0 / HEAD)
    yn = d * lax.rsqrt(var + GN_EPS) * gng_ref[...] + gnb_ref[...]
    o_rw = (yn + bonus_ref[...]) * g_ref[...]
    merged = (_sigmoid(ga_ref[...]) * o_rw + _sigmoid(gb_ref[...]) * ocv_ref[...]
              + _sigmoid(gm_ref[...]) * omem_ref[...])
    z = ALPHA * x + _dot(merged, wo_ref[...])
    h = _layer_norm(z, l1g_ref[...], l1b_ref[...])
    h_o[...] = h
    lt_o[...] = _dot3(wr_ref[...], h, _NT)


def _merge_ln1(xp, xs, p, y_raw, bonus, g, o_cv, o_mem, gn_g, gn_b, bd, w_o, ln_g, ln_b, w_router_t):
    n = p.shape[0]
    n_prompt_tiles = xp.shape[0] // TR
    gblk = COL_GATE // D
    row = pl.BlockSpec((TR, D), lambda i: (i, 0))
    xp_spec = pl.BlockSpec((TR, D), lambda i: (jnp.minimum(i, n_prompt_tiles - 1), 0))
    xs_spec = pl.BlockSpec((TR, D), lambda i: (jnp.maximum(i - n_prompt_tiles, 0), 0))

    def gate_spec(j):
        return pl.BlockSpec((TR, D), lambda i: (i, gblk + j))

    def const_spec(shape):
        return pl.BlockSpec(shape, lambda i: (0,) * len(shape))

    vec = const_spec((1, D))
    return pl.pallas_call(
        functools.partial(_merge_kernel, n_prompt_tiles),
        grid=(n // TR,),
        in_specs=[xp_spec, xs_spec, gate_spec(0), gate_spec(1), gate_spec(2), row, row, row, row, row,
                  vec, vec, const_spec((GROUP_LANES, GROUP_LANES)), const_spec((D, D)), vec, vec,
                  const_spec((N_EXPERTS, D))],
        out_specs=[row, pl.BlockSpec((N_EXPERTS, TR), lambda i: (0, i))],
        out_shape=[jax.ShapeDtypeStruct((n, D), F32), jax.ShapeDtypeStruct((N_EXPERTS, n), F32)],
        compiler_params=_cparams(("arbitrary",)),
        name="merge_ln1",
    )(xp, xs, p, p, p, y_raw, bonus, g, o_cv, o_mem, gn_g, gn_b, bd, w_o, ln_g, ln_b, w_router_t)


ROUTE_TILE = 256


def _routing_kernel(lt_ref, bias_ref, tri_ref, idx_o, w_o, pos_o, cnt_o, carry):
    i = pl.program_id(0)
    tile = lt_ref.shape[1]

    @pl.when(i == 0)
    def _():
        carry[...] = jnp.zeros_like(carry)

    neg_inf = -jnp.inf
    scores = _sigmoid(lt_ref[...])
    choice = scores + bias_ref[...]
    row = lax.broadcasted_iota(jnp.int32, (N_EXPERTS, tile), 0)
    rowf = row.astype(F32)
    grpf = (row // GROUP_SIZE).astype(F32)

    def group_allreduce(x, op):
        for s in (1, 2, 4):
            up = pltpu.roll(x, N_EXPERTS - s, 0)
            dn = pltpu.roll(x, s, 0)
            x = op(x, jnp.where((row & s) == 0, up, dn))
        return x

    m1 = group_allreduce(choice, jnp.maximum)
    first = group_allreduce(jnp.where(choice == m1, rowf, float(N_EXPERTS)), jnp.minimum)
    m2 = group_allreduce(jnp.where(rowf == first, neg_inf, choice), jnp.maximum)
    gscore = m1 + m2

    gsel = jnp.zeros_like(choice)
    for _ in range(TOPK_GROUPS):
        gmax = jnp.max(gscore, axis=0, keepdims=True)
        pick = jnp.min(jnp.where(gscore == gmax, grpf, float(N_GROUPS)), axis=0, keepdims=True)
        hit = grpf == pick
        gsel = jnp.where(hit, 1.0, gsel)
        gscore = jnp.where(hit, neg_inf, gscore)

    masked = jnp.where(gsel > 0.0, choice, neg_inf)
    row8 = lax.broadcasted_iota(jnp.int32, (TOP_K, tile), 0)
    idx_acc = jnp.zeros((TOP_K, tile), F32)
    w_acc = jnp.zeros((TOP_K, tile), F32)
    sel_all = jnp.zeros_like(choice)
    for kk in range(TOP_K):
        mx = jnp.max(masked, axis=0, keepdims=True)
        pick = jnp.min(jnp.where(masked == mx, rowf, float(N_EXPERTS)), axis=0, keepdims=True)
        hit = rowf == pick
        wk = jnp.sum(jnp.where(hit, scores, 0.0), axis=0, keepdims=True)
        idx_acc = jnp.where(row8 == kk, pick, idx_acc)
        w_acc = jnp.where(row8 == kk, wk, w_acc)
        sel_all = jnp.where(hit, 1.0, sel_all)
        masked = jnp.where(hit, neg_inf, masked)

    w_sum = jnp.sum(w_acc, axis=0, keepdims=True)
    w_o[...] = w_acc / w_sum * ROUTED_SCALE
    idx_o[...] = idx_acc.astype(jnp.int32)

    prefix = lax.dot_general(sel_all.astype(BF16), tri_ref[...], _NN, preferred_element_type=F32) + carry[...]
    pos_acc = jnp.zeros((TOP_K, tile), F32)
    for kk in range(TOP_K):
        hit = rowf == idx_acc[kk:kk + 1, :]
        pk = jnp.sum(jnp.where(hit, prefix, 0.0), axis=0, keepdims=True)
        pos_acc = jnp.where(row8 == kk, pk, pos_acc)
    pos_o[...] = pos_acc.astype(jnp.int32)
    carry[...] = carry[...] + jnp.sum(sel_all, axis=1, keepdims=True)
    cnt_o[...] = carry[...]


def _routing(logits_t, bias_col, tri):
    n = logits_t.shape[1]
    tile = ROUTE_TILE
    tok = pl.BlockSpec((TOP_K, tile), lambda i: (0, i))
    return pl.pallas_call(
        _routing_kernel,
        grid=(n // tile,),
        in_specs=[pl.BlockSpec((N_EXPERTS, tile), lambda i: (0, i)),
                  pl.BlockSpec((N_EXPERTS, 1), lambda i: (0, 0)),
                  pl.BlockSpec((tile, tile), lambda i: (0, 0))],
        out_specs=[tok, tok, tok, pl.BlockSpec((N_EXPERTS, 1), lambda i: (0, 0))],
        out_shape=[jax.ShapeDtypeStruct((TOP_K, n), jnp.int32), jax.ShapeDtypeStruct((TOP_K, n), F32),
                   jax.ShapeDtypeStruct((TOP_K, n), jnp.int32), jax.ShapeDtypeStruct((N_EXPERTS, 1), F32)],
        scratch_shapes=[pltpu.VMEM((N_EXPERTS, 1), F32)],
        compiler_params=_cparams(("arbitrary",)),
        name="routing",
    )(logits_t, bias_col, tri)


ZERO_ROWS = 128


def _dispatch_kernel(zs_ref, zc_ref, dest_ref, h_ref, xb_out, dest_smem, zbuf, sem, idx_sem, zsem):
    i = pl.program_id(0)
    rows = h_ref.shape[0]

    @pl.when(i == 0)
    def _():
        zbuf[...] = jnp.zeros_like(zbuf)

        def zero_copy(piece):
            dst0 = pl.multiple_of(piece * ZERO_ROWS, ZERO_ROWS)
            return pltpu.make_async_copy(zbuf, xb_out.at[pl.ds(dst0, ZERO_ROWS), :], zsem)

        def per_range(e, c):
            def issue_piece(j, c2):
                zero_copy(zs_ref[e] + j).start()
                return c2
            lax.fori_loop(0, zc_ref[e], issue_piece, 0)
            return c

        def per_range_wait(e, c):
            def wait_piece(j, c2):
                zero_copy(zs_ref[e] + j).wait()
                return c2
            lax.fori_loop(0, zc_ref[e], wait_piece, 0)
            return c

        lax.fori_loop(0, N_EXPERTS + 1, per_range, 0)
        lax.fori_loop(0, N_EXPERTS + 1, per_range_wait, 0)

    cp = pltpu.make_async_copy(dest_ref, dest_smem, idx_sem)
    cp.start()
    cp.wait()

    def row_copy(t, k):
        return pltpu.make_async_copy(h_ref.at[pl.ds(t, 1), :],
                                     xb_out.at[pl.ds(dest_smem[k, t], 1), :], sem)

    def issue(t, c):
        for k in range(TOP_K):
            row_copy(t, k).start()
        return c

    def drain(t, c):
        for k in range(TOP_K):
            row_copy(t, k).wait()
        return c

    lax.fori_loop(0, rows, issue, 0)
    lax.fori_loop(0, rows, drain, 0)


def _dispatch(zero_start, zero_count, dest_t, h, n_rows):
    n = h.shape[0]
    grid_spec = pltpu.PrefetchScalarGridSpec(
        num_scalar_prefetch=2,
        grid=(n // TR,),
        in_specs=[pl.BlockSpec((TOP_K, TR), lambda i, zs, zc: (0, i)),
                  pl.BlockSpec((TR, D), lambda i, zs, zc: (i, 0))],
        out_specs=pl.BlockSpec(memory_space=pl.ANY),
        scratch_shapes=[pltpu.SMEM((TOP_K, TR), jnp.int32), pltpu.VMEM((ZERO_ROWS, D), F32),
                        pltpu.SemaphoreType.DMA, pltpu.SemaphoreType.DMA, pltpu.SemaphoreType.DMA],
    )
    return pl.pallas_call(
        _dispatch_kernel,
        grid_spec=grid_spec,
        out_shape=jax.ShapeDtypeStruct((n_rows, D), F32),
        compiler_params=_cparams(("arbitrary",)),
        name="moe_dispatch",
    )(zero_start, zero_count, dest_t, h)


def _silu(x):
    return x * _sigmoid(x)


def _expert_kernel(be_ref, nu_ref, x_ref, wu_ref, wd_ref, o_ref, wu_bf, wd_bf):
    b = pl.program_id(0)
    changed = jnp.logical_or(b == 0, be_ref[b] != be_ref[jnp.maximum(b - 1, 0)])

    @pl.when(changed)
    def _():
        wu_bf[...] = wu_ref[0].astype(BF16)
        wd_bf[...] = wd_ref[0].astype(BF16)

    @pl.when(b < nu_ref[0])
    def _():
        up = _dot(x_ref[...], wu_bf[...])
        act = _silu(up[:, :EXPERT_FF]) * up[:, EXPERT_FF:]
        o_ref[...] = _dot(act, wd_bf[...])

    @pl.when(b >= nu_ref[0])
    def _():
        o_ref[...] = jnp.zeros_like(o_ref)


def _experts(block_e, n_used, xb, w_up, w_down):
    rows = xb.shape[0]
    nb = rows // EXPERT_BM

    def xmap(b, be, nu):
        return (jnp.minimum(b, nu[0] - 1), 0)

    grid_spec = pltpu.PrefetchScalarGridSpec(
        num_scalar_prefetch=2,
        grid=(nb,),
        in_specs=[pl.BlockSpec((EXPERT_BM, D), xmap),
                  pl.BlockSpec((1, D, 2 * EXPERT_FF), lambda b, be, nu: (be[b], 0, 0)),
                  pl.BlockSpec((1, EXPERT_FF, D), lambda b, be, nu: (be[b], 0, 0))],
        out_specs=pl.BlockSpec((EXPERT_BM, D), lambda b, be, nu: (b, 0)),
        scratch_shapes=[pltpu.VMEM((D, 2 * EXPERT_FF), BF16), pltpu.VMEM((EXPERT_FF, D), BF16)],
    )
    return pl.pallas_call(
        _expert_kernel,
        grid_spec=grid_spec,
        out_shape=jax.ShapeDtypeStruct((rows, D), F32),
        compiler_params=_cparams(("arbitrary",)),
        name="moe_experts",
    )(block_e, n_used, xb, w_up, w_down)


SHARED_TILE = 512


def _shared_kernel(h_ref, wu_ref, wd_ref, o_ref):
    up = _dot(h_ref[...], wu_ref[...])
    act = _silu(up[:, :SHARED_FF]) * up[:, SHARED_FF:]
    o_ref[...] = _dot(act, wd_ref[...])


def _shared_ffn(h, w_up, w_down):
    n = h.shape[0]
    row = pl.BlockSpec((SHARED_TILE, D), lambda i: (i, 0))
    return pl.pallas_call(
        _shared_kernel,
        grid=(n // SHARED_TILE,),
        in_specs=[row, pl.BlockSpec((D, 2 * SHARED_FF), lambda i: (0, 0)),
                  pl.BlockSpec((SHARED_FF, D), lambda i: (0, 0))],
        out_specs=row,
        out_shape=jax.ShapeDtypeStruct((n, D), F32),
        compiler_params=_cparams(("parallel",)),
        name="shared_ffn",
    )(h, w_up, w_down)


def _combine_kernel(dest_ref, w_ref, h_ref, sh_ref, yb_ref, l2g_ref, l2b_ref, y_o, buf, dest_smem, sem, idx_sem):
    rows = h_ref.shape[0]
    cp = pltpu.make_async_copy(dest_ref, dest_smem, idx_sem)
    cp.start()
    cp.wait()

    def row_copy(t, k):
        return pltpu.make_async_copy(yb_ref.at[pl.ds(dest_smem[k, t], 1), :],
                                     buf.at[k, pl.ds(t, 1), :], sem)

    def issue(t, c):
        for k in range(TOP_K):
            row_copy(t, k).start()
        return c

    def drain(t, c):
        for k in range(TOP_K):
            row_copy(t, k).wait()
        return c

    lax.fori_loop(0, rows, issue, 0)
    lax.fori_loop(0, rows, drain, 0)
    w = w_ref[...]
    f = sh_ref[...]
    for k in range(TOP_K):
        f = f + w[:, k:k + 1] * buf[k]
    z = ALPHA * h_ref[...] + f
    y_o[...] = _layer_norm(z, l2g_ref[...], l2b_ref[...])


def _combine_ln2(dest_t, w_tok, h, shared, yb, ln_g, ln_b):
    n = h.shape[0]
    row = pl.BlockSpec((TR, D), lambda i: (i, 0))
    vec = pl.BlockSpec((1, D), lambda i: (0, 0))
    return pl.pallas_call(
        _combine_kernel,
        grid=(n // TR,),
        in_specs=[pl.BlockSpec((TOP_K, TR), lambda i: (0, i)),
                  pl.BlockSpec((TR, TOP_K), lambda i: (i, 0)),
                  row, row, pl.BlockSpec(memory_space=pl.ANY), vec, vec],
        out_specs=row,
        out_shape=jax.ShapeDtypeStruct((n, D), F32),
        scratch_shapes=[pltpu.VMEM((TOP_K, TR, D), F32), pltpu.SMEM((TOP_K, TR), jnp.int32),
                        pltpu.SemaphoreType.DMA, pltpu.SemaphoreType.DMA],
        compiler_params=_cparams(("arbitrary",)),
        name="moe_combine_ln2",
    )(dest_t, w_tok, h, shared, yb, ln_g, ln_b)


def _reorder_cols(w):
    pad = jnp.zeros(w.shape[:-1] + (LORA_PAD - LORA_W,), w.dtype)
    rw_end = RW_COLS
    cv_end = rw_end + 3 * CONV_DIM
    q_end = cv_end + MEM_DIM
    return jnp.concatenate(
        [w[..., 0:3 * D], w[..., q_end:q_end + 3 * D], w[..., rw_end:cv_end], w[..., cv_end:q_end],
         w[..., 3 * D:3 * D + LORA_W], pad,
         w[..., 3 * D + LORA_W:3 * D + LORA_W + LORA_A], pad,
         w[..., 3 * D + LORA_W + LORA_A:rw_end]], axis=-1)


def _rw_cols_split(v):
    pad = jnp.zeros(v.shape[:-1] + (LORA_PAD - LORA_W,), v.dtype)
    lora = jnp.concatenate([v[..., 3 * D:3 * D + LORA_W], pad,
                            v[..., 3 * D + LORA_W:3 * D + LORA_W + LORA_A], pad,
                            v[..., 3 * D + LORA_W + LORA_A:]], axis=-1)
    return v[..., 0:3 * D], lora


def _pad_rows(w, rows):
    return jnp.concatenate([w, jnp.zeros((rows - w.shape[0],) + w.shape[1:], w.dtype)], axis=0)


def kernel(x_prompt, x_sample, mem_prompt, state_rwkv, state_shift, state_conv, cache_mem_k, cache_mem_v,
           w_in, mu_shift, rw_w0, rw_w2, rw_a0, rw_a2, rw_g2, rw_k_k, rw_k_a, rw_r_k, rw_gn_g, rw_gn_b,
           conv_w, w_conv_out, w_mem_k, w_mem_v, w_mem_o, w_o, ln1_g, ln1_b, w_router, router_bias,
           w_exp_up, w_exp_down, w_sh_up, w_sh_down, ln2_g, ln2_b):
    n_prompt = x_prompt.shape[0] * x_prompt.shape[1]
    n_seq_s, seq_s = x_sample.shape[0], x_sample.shape[1]
    n_sample = n_seq_s * seq_s
    n = n_prompt + n_sample
    assert x_prompt.shape[0] == 1 and seq_s == SEQ_S and n_prompt % TR == 0 and n_sample % TR == 0
    assert n % SHARED_TILE == 0 and n % ROUTE_TILE == 0 and w_in.shape[0] == 1
    assert n_prompt % PROJ_TM == 0 and n_sample % PROJ_TM == 0

    xp = x_prompt.reshape(n_prompt, D)
    xs = x_sample.reshape(n_sample, D)

    def vec(v):
        return v.reshape(1, -1).astype(F32)

    w_in_r = _w_relayout(w_in[0])
    p = _in_proj(xp, xs, w_in_r)

    w_kv = jnp.concatenate([w_mem_k[0], w_mem_v[0]], axis=1).astype(BF16)
    kv = _matmul(mem_prompt[0], w_kv, N_MEM, 512, "mem_kv")
    mem_k_p, mem_v_p = kv[:, :MEM_DIM], kv[:, MEM_DIM:]

    mu_rkv, mu_lora = _rw_cols_split(vec(mu_shift[0]))
    sh_rkv, sh_lora = _rw_cols_split(state_shift[0, :, 0, :])
    bnd_rkv = jnp.repeat(sh_rkv, seq_s, axis=0)
    bnd_lora = jnp.repeat(sh_lora, seq_s, axis=0)
    hi = lax.broadcasted_iota(jnp.int32, (GROUP_LANES, GROUP_LANES), 0) // HEAD
    hj = lax.broadcasted_iota(jnp.int32, (GROUP_LANES, GROUP_LANES), 1) // HEAD
    bd = (hi == hj).astype(BF16)
    r, k, v, kk, b, lw, g, bonus = _rwkv_prep(
        p, bnd_rkv, bnd_lora, mu_rkv, mu_lora, vec(rw_w0[0]), vec(rw_a0[0]), vec(rw_k_k[0]), vec(rw_k_a[0]),
        vec(rw_r_k[0]), _pad_rows(rw_w2[0], LORA_PAD).astype(BF16), _pad_rows(rw_a2[0], LORA_PAD).astype(BF16),
        rw_g2[0].astype(BF16), bd, n_prompt)

    s_sample = jnp.transpose(state_rwkv[0], (0, 2, 1, 3)).reshape(n_seq_s, HEAD, D)
    s_in = jnp.concatenate([jnp.zeros((STATE_SLOTS, HEAD, D), F32), s_sample.astype(F32)], axis=0)
    y_raw, s_out = _rwkv_scan(r, k, v, kk, b, lw, s_in, n_prompt)

    bnd1 = jnp.repeat(state_conv[0, :, 1, :], seq_s, axis=0)
    bnd2 = jnp.repeat(state_conv[0, :, 0, :], seq_s, axis=0)
    o_cv, u = _short_conv(p, bnd1, bnd2, _pad_rows(conv_w[0], 8), w_conv_out[0].astype(BF16), n_prompt)

    w_mem_o_b = w_mem_o[0].astype(BF16)
    o_mem_s = _mem_sample(p, cache_mem_k[0].reshape(n_seq_s, N_MEM, MEM_DIM),
                          cache_mem_v[0].reshape(n_seq_s, N_MEM, MEM_DIM), w_mem_o_b, n_prompt, n_seq_s)
    o_mem = _mem_attention(p, mem_k_p[None], mem_v_p[None], w_mem_o_b, o_mem_s, n_prompt)

    h, logits_t = _merge_ln1(xp, xs, p, y_raw, bonus, g, o_cv, o_mem, vec(rw_gn_g[0]), vec(rw_gn_b[0]), bd,
                             w_o[0].astype(BF16), vec(ln1_g[0]), vec(ln1_b[0]), w_router[0].T)

    ti = lax.broadcasted_iota(jnp.int32, (ROUTE_TILE, ROUTE_TILE), 0)
    tj = lax.broadcasted_iota(jnp.int32, (ROUTE_TILE, ROUTE_TILE), 1)
    tri = (ti < tj).astype(BF16)
    idx_t, w_t, pos_t, counts = _routing(logits_t, router_bias[0].reshape(N_EXPERTS, 1).astype(F32), tri)

    counts = counts[:, 0].astype(jnp.int32)
    padded = (counts + EXPERT_BM - 1) // EXPERT_BM * EXPERT_BM
    seg_end = jnp.cumsum(padded)
    seg_start = seg_end - padded
    expert_ids = jnp.arange(N_EXPERTS, dtype=jnp.int32)
    dest_t = pos_t + jnp.sum(
        jnp.where(idx_t[None] == expert_ids[:, None, None], seg_start[:, None, None], 0), axis=0)
    nb = (n * TOP_K) // EXPERT_BM + N_EXPERTS
    block_rows = jnp.arange(nb, dtype=jnp.int32) * EXPERT_BM
    block_e = jnp.minimum(jnp.sum((seg_end[None, :] <= block_rows[:, None]).astype(jnp.int32), axis=1),
                          N_EXPERTS - 1)
    n_used = (seg_end[-1:] // EXPERT_BM).astype(jnp.int32)

    pieces_per_block = EXPERT_BM // ZERO_ROWS
    valid_last = counts - (padded - EXPERT_BM)
    first_piece = valid_last // ZERO_ROWS
    zero_start = jnp.where(padded > 0, (seg_end - EXPERT_BM) // ZERO_ROWS + first_piece, 0)
    zero_count = jnp.where(padded > 0, pieces_per_block - first_piece, 0)
    total_pieces = nb * pieces_per_block
    zero_start = jnp.concatenate([zero_start, seg_end[-1:] // ZERO_ROWS]).astype(jnp.int32)
    zero_count = jnp.concatenate([zero_count, total_pieces - seg_end[-1:] // ZERO_ROWS]).astype(jnp.int32)

    xb = _dispatch(zero_start, zero_count, dest_t, h, nb * EXPERT_BM)
    yb = _experts(block_e, n_used, xb, w_exp_up[0], w_exp_down[0])
    shared = _shared_ffn(h, w_sh_up[0].astype(BF16), w_sh_down[0].astype(BF16))
    y = _combine_ln2(dest_t, w_t.T, h, shared, yb, vec(ln2_g[0]), vec(ln2_b[0]))

    dt = x_prompt.dtype
    y_p = y[:n_prompt].reshape(x_prompt.shape)
    y_s = y[n_prompt:].reshape(x_sample.shape)

    def state_out(s):
        q = s.reshape(s.shape[0], HEAD, N_HEADS, HEAD)
        return jnp.transpose(q, (0, 2, 1, 3))[None].astype(dt)

    rw_p = state_out(s_out[0:1])
    rw_s = state_out(s_out[STATE_SLOTS:])

    last_rows = jnp.concatenate([jnp.array([n_prompt - 1], jnp.int32),
                                 n_prompt + seq_s - 1 + seq_s * jnp.arange(n_seq_s, dtype=jnp.int32)])
    p_last = p[last_rows]
    shift = jnp.concatenate([p_last[:, 0:3 * D],
                             p_last[:, COL_LORA:COL_LORA + LORA_W],
                             p_last[:, COL_LORA + LORA_PAD:COL_LORA + LORA_PAD + LORA_A],
                             p_last[:, COL_LORA + 2 * LORA_PAD:]], axis=1)
    sh_p = shift[0:1].reshape(1, 1, 1, RW_COLS)
    sh_s = shift[1:].reshape(1, n_seq_s, 1, RW_COLS)

    cv_p = u[n_prompt - 2:n_prompt].reshape(1, 1, 2, CONV_DIM)
    cv_s = u[n_prompt:].reshape(n_seq_s, seq_s, CONV_DIM)[:, seq_s - 2:, :][None]

    mk_p = mem_k_p.reshape(1, 1, N_MEM, MEM_HEADS, MEM_HEAD_DIM)
    mv_p = mem_v_p.reshape(1, 1, N_MEM, MEM_HEADS, MEM_HEAD_DIM)
    return (y_p, y_s, rw_p, sh_p, cv_p, mk_p, mv_p, rw_s, sh_s, cv_s)
```

```python
import functools

import jax
import jax.numpy as jnp
from jax import lax
from jax.experimental import pallas as pl
from jax.experimental.pallas import tpu as pltpu

F32 = jnp.float32
BF16 = jnp.bfloat16

D = 2048
HEAD = 64
N_HEADS = D // HEAD
LORA_W = 96
LORA_A = 96
LORA_G = 256
DECAY_SCALE = 0.6065306597126334
GN_EPS = HEAD * 1e-5
CONV_DIM = D // 2
N_MEM = 256
MEM_HEADS = 4
MEM_HEAD_DIM = 256
MEM_DIM = MEM_HEADS * MEM_HEAD_DIM
N_EXPERTS = 64
N_GROUPS = 8
GROUP_SIZE = N_EXPERTS // N_GROUPS
TOPK_GROUPS = 4
TOP_K = 8
EXPERT_FF = 512
SHARED_FF = 512
ROUTED_SCALE = 2.5
LN_EPS = 1e-5
DEPTH = 1
ALPHA = (2 * DEPTH) ** 0.25
RW_COLS = 3 * D + LORA_W + LORA_A + LORA_G

LORA_PAD = 128
LORA_COLS = 2 * LORA_PAD + LORA_G
COL_RKV = 0
COL_GATE = 3 * D
COL_CONV = 6 * D
COL_Q = COL_CONV + 3 * CONV_DIM
COL_LORA = COL_Q + MEM_DIM
P_COLS = COL_LORA + LORA_COLS

CHUNK = 16
GROUP_HEADS = 4
GROUP_LANES = GROUP_HEADS * HEAD
N_LANE_GROUPS = D // GROUP_LANES
STACK = GROUP_HEADS * CHUNK
SEQ_S = 16
STATE_SLOTS = 8

TR = 128
SCAN_ROWS = STATE_SLOTS * CHUNK
EXPERT_BM = 256
VMEM_LIMIT = 56 * 1024 * 1024


def _cparams(sem):
    return pltpu.CompilerParams(dimension_semantics=sem, vmem_limit_bytes=VMEM_LIMIT)


def _sigmoid(x):
    return 1.0 / (1.0 + jnp.exp(-x))


def _dot(a, b, dims=(((1,), (0,)), ((), ()))):
    return lax.dot_general(a.astype(BF16), b.astype(BF16), dims, preferred_element_type=F32)


_NN = (((1,), (0,)), ((), ()))
_NT = (((1,), (1,)), ((), ()))
_TN = (((0,), (0,)), ((), ()))


def _split2(x):
    hi = x.astype(BF16)
    lo = (x - hi.astype(F32)).astype(BF16)
    return hi, lo


def _split3(x):
    hi = x.astype(BF16)
    r1 = x - hi.astype(F32)
    mid = r1.astype(BF16)
    lo = (r1 - mid.astype(F32)).astype(BF16)
    return hi, mid, lo


def _dot3(a, b, dims=_NN):
    ah, al = _split2(a)
    bh, bl = _split2(b)
    f = functools.partial(lax.dot_general, dimension_numbers=dims, preferred_element_type=F32)
    return f(ah, bh) + (f(ah, bl) + f(al, bh))


def _dot_exact_rhs(a, b_bf16, dims=_NN):
    hi, mid, lo = _split3(a)
    f = functools.partial(lax.dot_general, dimension_numbers=dims, preferred_element_type=F32)
    return f(hi, b_bf16) + (f(mid, b_bf16) + f(lo, b_bf16))


def _dot_exact_lhs(a_bf16, b):
    hi, mid, lo = _split3(b)
    f = functools.partial(lax.dot_general, dimension_numbers=_NN, preferred_element_type=F32)
    return f(a_bf16, hi) + (f(a_bf16, mid) + f(a_bf16, lo))


_sdot = _dot


def _mm_kernel(x_ref, w_ref, o_ref):
    o_ref[...] = _dot(x_ref[...], w_ref[...]).astype(o_ref.dtype)


def _matmul(x, w, tm, tn, name):
    m, k = x.shape
    n = w.shape[1]
    return pl.pallas_call(
        _mm_kernel,
        grid=(m // tm, n // tn),
        in_specs=[pl.BlockSpec((tm, k), lambda i, j: (i, 0)),
                  pl.BlockSpec((k, tn), lambda i, j: (0, j))],
        out_specs=pl.BlockSpec((tm, tn), lambda i, j: (i, j)),
        out_shape=jax.ShapeDtypeStruct((m, n), F32),
        compiler_params=_cparams(("parallel", "arbitrary")),
        name=name,
    )(x, w)


WT_BLOCK = 512
WT_PIECE = 128


def _wt_relayout_kernel(src_ref, wt_hbm, o_ref, buf, sems):
    j = pl.program_id(0)
    n_plain = pl.num_programs(0) - 1
    n_pieces = WT_BLOCK // WT_PIECE

    def emit():
        for s in range(n_pieces):
            rows = slice(s * WT_PIECE, (s + 1) * WT_PIECE)
            o_ref[:, rows] = buf[rows, :].T.astype(BF16)

    @pl.when(j < n_plain)
    def _():
        row0 = pl.multiple_of(src_ref[j], 8)
        copies = [pltpu.make_async_copy(wt_hbm.at[pl.ds(row0 + s * WT_PIECE, WT_PIECE), :],
                                        buf.at[pl.ds(s * WT_PIECE, WT_PIECE), :], sems.at[s])
                  for s in range(n_pieces)]
        for c in copies:
            c.start()
        for c in copies:
            c.wait()
        emit()

    @pl.when(j == n_plain)
    def _():
        lo_w = 3 * D
        lo_a = lo_w + LORA_W
        lo_g = lo_a + LORA_A
        pieces = ((lo_w, 0, LORA_W), (lo_a, LORA_PAD, LORA_A), (lo_g, 2 * LORA_PAD, LORA_G))
        for _, dst, width in pieces[:2]:
            buf[dst + width:dst + LORA_PAD, :] = jnp.zeros((LORA_PAD - width, buf.shape[1]), F32)
        copies = [pltpu.make_async_copy(wt_hbm.at[pl.ds(src, width), :], buf.at[pl.ds(dst, width), :], sems.at[n])
                  for n, (src, dst, width) in enumerate(pieces)]
        for c in copies:
            c.start()
        for c in copies:
            c.wait()
        emit()


def _wt_relayout(wt):
    k = wt.shape[1]
    rw_end = RW_COLS
    cv_end = rw_end + 3 * CONV_DIM
    q_end = cv_end + MEM_DIM
    src = []
    for dst0, src0, width in ((COL_RKV, 0, 3 * D), (COL_GATE, q_end, 3 * D), (COL_CONV, rw_end, 3 * CONV_DIM),
                              (COL_Q, cv_end, MEM_DIM)):
        assert dst0 == len(src) * WT_BLOCK and width % WT_BLOCK == 0
        src += [src0 + b * WT_BLOCK for b in range(width // WT_BLOCK)]
    assert len(src) * WT_BLOCK == COL_LORA and LORA_COLS == WT_BLOCK
    grid_spec = pltpu.PrefetchScalarGridSpec(
        num_scalar_prefetch=1,
        grid=(len(src) + 1,),
        in_specs=[pl.BlockSpec(memory_space=pl.ANY)],
        out_specs=pl.BlockSpec((k, WT_BLOCK), lambda j, src_rows: (0, j)),
        scratch_shapes=[pltpu.VMEM((WT_BLOCK, k), F32), pltpu.SemaphoreType.DMA((WT_BLOCK // WT_PIECE,))],
    )
    return pl.pallas_call(
        _wt_relayout_kernel,
        grid_spec=grid_spec,
        out_shape=jax.ShapeDtypeStruct((k, P_COLS), BF16),
        compiler_params=_cparams(("arbitrary",)),
        name="w_in_relayout",
    )(jnp.asarray(src, jnp.int32), wt)


PROJ_TM = 512
PROJ_TN = 1536


def _in_proj_kernel(n_prompt_tiles, xp_ref, xs_ref, w_ref, o_ref, x_bf):
    i = pl.program_id(0)

    @pl.when(pl.program_id(1) == 0)
    def _():
        x_bf[...] = jnp.where(i < n_prompt_tiles, xp_ref[...], xs_ref[...]).astype(BF16)

    o_ref[...] = jnp.dot(x_bf[...], w_ref[...], preferred_element_type=F32)


def _in_proj(xp, xs, w):
    k = xp.shape[1]
    n_prompt_tiles = xp.shape[0] // PROJ_TM
    n_tiles = n_prompt_tiles + xs.shape[0] // PROJ_TM
    ncols = w.shape[1]
    return pl.pallas_call(
        functools.partial(_in_proj_kernel, n_prompt_tiles),
        grid=(n_tiles, ncols // PROJ_TN),
        in_specs=[pl.BlockSpec((PROJ_TM, k), lambda i, j: (jnp.minimum(i, n_prompt_tiles - 1), 0)),
                  pl.BlockSpec((PROJ_TM, k), lambda i, j: (jnp.maximum(i - n_prompt_tiles, 0), 0)),
                  pl.BlockSpec((k, PROJ_TN), lambda i, j: (0, j))],
        out_specs=pl.BlockSpec((PROJ_TM, PROJ_TN), lambda i, j: (i, j)),
        out_shape=jax.ShapeDtypeStruct((n_tiles * PROJ_TM, ncols), F32),
        scratch_shapes=[pltpu.VMEM((PROJ_TM, k), BF16)],
        compiler_params=_cparams(("arbitrary", "arbitrary")),
        name="in_proj",
    )(xp, xs, w)


def _head_sum(x, bd):
    parts = []
    for g in range(N_LANE_GROUPS):
        parts.append(_dot_exact_rhs(x[:, g * GROUP_LANES:(g + 1) * GROUP_LANES], bd))
    return jnp.concatenate(parts, axis=1)


def _prep_kernel(n_prompt_tiles, rkv_ref, lora_ref, c_rkv_ref, c_lora_ref, b_rkv_ref, b_lora_ref,
                 mu_rkv_ref, mu_lora_ref, w0_ref, a0_ref, kk_ref, ka_ref, rk_ref,
                 w2_ref, a2_ref, g2_ref, bd_ref,
                 r_o, k_o, v_o, kk_o, b_o, lw_o, g_o, bonus_o):
    i = pl.program_id(0)
    rows = rkv_ref.shape[0]
    row = lax.broadcasted_iota(jnp.int32, (rows, 1), 0)
    is_sample = i >= n_prompt_tiles
    seq_start = jnp.logical_and(is_sample, (row % SEQ_S) == 0)

    def mixed(x, carry_row, bnd, mu):
        prev = pltpu.roll(x, 1, 0)
        carry_row = jnp.where(i == 0, 0.0, carry_row)
        prev = jnp.where(row == 0, carry_row, prev)
        prev = jnp.where(seq_start, bnd, prev)
        return x + (prev - x) * mu

    def section(s):
        sl = slice(s * D, (s + 1) * D)
        return mixed(rkv_ref[:, sl], c_rkv_ref[7:8, sl], b_rkv_ref[:, sl], mu_rkv_ref[:, sl])

    lo = mixed(lora_ref[...], c_lora_ref[7:8, :], b_lora_ref[...], mu_lora_ref[...])
    w_lo = lo[:, 0:LORA_PAD]
    a_lo = lo[:, LORA_PAD:2 * LORA_PAD]
    g_lo = lo[:, 2 * LORA_PAD:]
    log_w = -DECAY_SCALE * _sigmoid(w0_ref[...] + _dot(jnp.tanh(w_lo), w2_ref[...]))
    a = _sigmoid(a0_ref[...] + _dot(a_lo, a2_ref[...]))
    g_o[...] = _dot(_sigmoid(g_lo), g2_ref[...])
    lw_o[...] = log_w

    bd = bd_ref[...]
    k = section(1)
    kk = k * kk_ref[...]
    ss = _head_sum(kk * kk, bd)
    kk = kk * lax.rsqrt(jnp.maximum(ss, 1e-24))
    kk_o[...] = kk
    b_o[...] = kk * a
    k = k * (1.0 + (a - 1.0) * ka_ref[...])
    k_o[...] = k
    r = section(0)
    r_o[...] = r
    v = section(2)
    v_o[...] = v
    bonus_o[...] = _head_sum(r * k * rk_ref[...], bd) * v


def _rwkv_prep(p, bnd_rkv, bnd_lora, mu_rkv, mu_lora, w0, a0, k_k, k_a, r_k, w2p, a2p, g2, bd, n_prompt):
    n = p.shape[0]
    n_prompt_tiles = n_prompt // TR
    carry_blk = TR // 8
    lora_blk = COL_LORA // LORA_COLS

    def row_spec(cols, cb=0):
        return pl.BlockSpec((TR, cols), lambda i: (i, cb))

    def carry_spec(cols, cb=0):
        return pl.BlockSpec((8, cols), lambda i: (jnp.maximum(i * carry_blk - 1, 0), cb))

    def bnd_spec(cols):
        return pl.BlockSpec((TR, cols), lambda i: (jnp.maximum(i - n_prompt_tiles, 0), 0))

    def const_spec(shape):
        return pl.BlockSpec(shape, lambda i: (0,) * len(shape))

    out = jax.ShapeDtypeStruct((n, D), F32)
    return pl.pallas_call(
        functools.partial(_prep_kernel, n_prompt_tiles),
        grid=(n // TR,),
        in_specs=[row_spec(3 * D), row_spec(LORA_COLS, lora_blk),
                  carry_spec(3 * D), carry_spec(LORA_COLS, lora_blk),
                  bnd_spec(3 * D), bnd_spec(LORA_COLS),
                  const_spec((1, 3 * D)), const_spec((1, LORA_COLS)),
                  const_spec((1, D)), const_spec((1, D)), const_spec((1, D)), const_spec((1, D)),
                  const_spec((1, D)),
                  const_spec((LORA_PAD, D)), const_spec((LORA_PAD, D)), const_spec((LORA_G, D)),
                  const_spec((GROUP_LANES, GROUP_LANES))],
        out_specs=[row_spec(D)] * 8,
        out_shape=[out] * 8,
        compiler_params=_cparams(("arbitrary",)),
        name="rwkv_prep",
    )(p, p, p, p, bnd_rkv, bnd_lora, mu_rkv, mu_lora, w0, a0, k_k, k_a, r_k, w2p, a2p, g2, bd)


def _scan_kernel(n_prompt_tiles, r_ref, k_ref, v_ref, kk_ref, b_ref, lw_ref, s_in_ref, y_ref, s_out_ref, s_scr):
    i = pl.program_id(0)
    is_sample = i >= n_prompt_tiles
    n_chunks = r_ref.shape[0] // CHUNK

    lane = lax.broadcasted_iota(jnp.int32, (1, GROUP_LANES), 1)
    head_masks = [(lane // HEAD == h).astype(F32) for h in range(GROUP_HEADS)]
    ri = lax.broadcasted_iota(jnp.int32, (STACK, 2 * STACK), 0)
    ci = lax.broadcasted_iota(jnp.int32, (STACK, 2 * STACK), 1)
    same_head = (ri // CHUNK) == ((ci % STACK) // CHUNK)
    strict_lower = jnp.logical_and(same_head, (ci % CHUNK) < (ri % CHUNK))
    mask_incl = jnp.logical_and(same_head, (ci % CHUNK) <= (ri % CHUNK)).astype(F32)
    mask_strict_b = jnp.logical_and(strict_lower, ci < STACK).astype(F32)
    mask_strict_k = jnp.logical_and(strict_lower, ci >= STACK).astype(F32)
    eye = (ri == ci).astype(F32)
    ti = lax.broadcasted_iota(jnp.int32, (CHUNK, CHUNK), 0)
    si = lax.broadcasted_iota(jnp.int32, (CHUNK, CHUNK), 1)
    tri_incl = (si <= ti).astype(BF16)
    rb = lax.broadcasted_iota(jnp.int32, (GROUP_LANES, GROUP_LANES), 0)
    cb = lax.broadcasted_iota(jnp.int32, (GROUP_LANES, GROUP_LANES), 1)
    block_diag = ((rb // HEAD) == (cb // HEAD)).astype(F32)

    def stack(x):
        return jnp.concatenate([x * m for m in head_masks], axis=0)

    def unstack(x):
        out = x[0:CHUNK]
        for h in range(1, GROUP_HEADS):
            out = out + x[h * CHUNK:(h + 1) * CHUNK]
        return out

    def compact(s):
        out = s[0:HEAD]
        for h in range(1, GROUP_HEADS):
            out = out + s[h * HEAD:(h + 1) * HEAD]
        return out

    @pl.when(i == 0)
    def _():
        s_out_ref[...] = jnp.zeros_like(s_out_ref)

    groups = range(N_LANE_GROUPS)
    lanes = [slice(g * GROUP_LANES, (g + 1) * GROUP_LANES) for g in groups]

    def twice(x):
        return jnp.concatenate([x, x], axis=0)

    def a0(c):
        rows = slice(c * CHUNK, (c + 1) * CHUNK)
        st = {"rows": rows}
        st["lw"] = [lw_ref[rows, lanes[g]] for g in groups]
        st["cum"] = [_dot_exact_lhs(tri_incl, st["lw"][g]) for g in groups]
        return st

    def a1(st):
        rows = st["rows"]
        lhs_s, bk_s, v_s, kr_t, p_end = [], [], [], [], []
        for g in groups:
            cum, lw = st["cum"][g], st["lw"][g]
            e_incl = jnp.exp(cum)
            e_excl = jnp.exp(cum - lw)
            e_neg = jnp.exp(-cum)
            p_end.append(e_incl[CHUNK - 1:CHUNK, :])
            r_t = r_ref[rows, lanes[g]] * e_incl
            kk_t = kk_ref[rows, lanes[g]] * e_excl
            b_t = b_ref[rows, lanes[g]] * e_neg
            k_t = k_ref[rows, lanes[g]] * e_neg
            kr_t.append(jnp.concatenate([kk_t, r_t], axis=0))
            lhs_s.append(jnp.concatenate([stack(kk_t), stack(r_t)], axis=0))
            bk_s.append(jnp.concatenate([stack(b_t), stack(k_t)], axis=0))
            v_s.append(stack(v_ref[rows, lanes[g]]))
        st.update(bk_s=bk_s, v_s=v_s, kr_t=kr_t, p_end=p_end)
        st["mn"] = [_sdot(lhs_s[g], bk_s[g], _NT) for g in groups]

    def a2(st):
        mn = st.pop("mn")
        st["m1"] = [mn[g][0:STACK] * mask_strict_b for g in groups]
        m_k = [mn[g][0:STACK] * mask_strict_k for g in groups]
        st["n_bk"] = [mn[g][STACK:] * mask_incl for g in groups]
        st["m2"] = [_sdot(st["m1"][g], twice(st["m1"][g])) for g in groups]
        st["mv"] = [_sdot(m_k[g], twice(st["v_s"][g])) for g in groups]

    def a3(st):
        st["m4"] = [_sdot(st["m2"][g], twice(st["m2"][g])) for g in groups]
        st["t_inv"] = [_sdot(eye - st["m1"][g], twice(eye + st["m2"][g])) for g in groups]

    def a4(st):
        st["m8"] = [_sdot(st["m4"][g], twice(st["m4"][g])) for g in groups]
        st["t_inv"] = [_sdot(st["t_inv"][g], twice(eye + st["m4"][g])) for g in groups]

    def a5(st):
        st["t_inv"] = [_sdot(st["t_inv"][g], twice(eye + st["m8"][g])) for g in groups]

    def b1(c, st, s_prev):
        load_state = is_sample if c > 0 else jnp.logical_or(is_sample, i == 0)
        slot = jnp.where(is_sample, c, 0)
        s0 = []
        for g in groups:
            s_loaded = jnp.concatenate([s_in_ref[slot, :, lanes[g]]] * GROUP_HEADS, axis=0) * block_diag
            s0.append(jnp.where(load_state, s_loaded, s_prev[g]))
        st["s0"] = s0
        st["gr"] = [_sdot(st["kr_t"][g], s0[g], _NT) for g in groups]

    def b2(st):
        u_s = [-_sdot(st["t_inv"][g], twice(stack(st["gr"][g][0:CHUNK]) + st["mv"][g])) for g in groups]
        st["uv"] = [jnp.concatenate([u_s[g], st["v_s"][g]], axis=0) for g in groups]

    def b3(c, st):
        slot = jnp.where(is_sample, c, 0)
        for g in groups:
            y_ref[st["rows"], lanes[g]] = st["gr"][g][CHUNK:] + unstack(_sdot(st["n_bk"][g], st["uv"][g]))
        s_new = []
        for g in groups:
            s_new.append(st["s0"][g] * st["p_end"][g] + _sdot(st["uv"][g], st["bk_s"][g] * st["p_end"][g], _TN))
            s_out_ref[slot, :, lanes[g]] = compact(s_new[g])
        return s_new

    s_cur = [s_scr[g] for g in groups]
    cur = a0(0)
    for lvl in (a1, a2, a3, a4, a5):
        lvl(cur)
    for c in range(n_chunks):
        last = c == n_chunks - 1
        nxt = None if last else a0(c + 1)
        b1(c, cur, s_cur)
        if not last:
            a1(nxt)
        b2(cur)
        if not last:
            a2(nxt)
        s_cur = b3(c, cur)
        if not last:
            for lvl in (a3, a4, a5):
                lvl(nxt)
        cur = nxt
    for g in groups:
        s_scr[g] = s_cur[g]


def _rwkv_scan(r, k, v, kk, b, lw, s_in, n_prompt):
    n = r.shape[0]
    n_prompt_tiles = n_prompt // SCAN_ROWS
    row_spec = pl.BlockSpec((SCAN_ROWS, D), lambda i: (i, 0))
    state_spec = pl.BlockSpec((STATE_SLOTS, HEAD, D),
                              lambda i: (jnp.maximum(i - n_prompt_tiles + 1, 0), 0, 0))
    return pl.pallas_call(
        functools.partial(_scan_kernel, n_prompt_tiles),
        grid=(n // SCAN_ROWS,),
        in_specs=[row_spec] * 6 + [state_spec],
        out_specs=[row_spec, state_spec],
        out_shape=[jax.ShapeDtypeStruct((n, D), F32), jax.ShapeDtypeStruct(s_in.shape, F32)],
        scratch_shapes=[pltpu.VMEM((N_LANE_GROUPS, GROUP_LANES, GROUP_LANES), F32)],
        compiler_params=_cparams(("arbitrary",)),
        name="rwkv_scan",
    )(r, k, v, kk, b, lw, s_in)


def _conv_kernel(n_prompt_tiles, cb_ref, cc_ref, ch_ref, ccc_ref, cch_ref, bnd1_ref, bnd2_ref,
                 cw_ref, wout_ref, o_ref, u_ref):
    i = pl.program_id(0)
    rows = cb_ref.shape[0]
    row = lax.broadcasted_iota(jnp.int32, (rows, 1), 0)
    is_sample = i >= n_prompt_tiles
    pos = row % SEQ_S
    u = cc_ref[...] * ch_ref[...]
    u_ref[...] = u
    u_prev = jnp.where(i == 0, 0.0, ccc_ref[...] * cch_ref[...])
    prev1 = pltpu.roll(u, 1, 0)
    prev1 = jnp.where(row == 0, u_prev[7:8, :], prev1)
    prev2 = pltpu.roll(u, 2, 0)
    prev2 = jnp.where(row == 0, u_prev[6:7, :], prev2)
    prev2 = jnp.where(row == 1, u_prev[7:8, :], prev2)
    bnd1 = bnd1_ref[...]
    prev1 = jnp.where(jnp.logical_and(is_sample, pos == 0), bnd1, prev1)
    prev2 = jnp.where(jnp.logical_and(is_sample, pos == 0), bnd2_ref[...], prev2)
    prev2 = jnp.where(jnp.logical_and(is_sample, pos == 1), bnd1, prev2)
    cw = cw_ref[...]
    conv = prev2 * cw[0:1, :] + prev1 * cw[1:2, :] + u * cw[2:3, :]
    o_ref[...] = _dot(cb_ref[...] * conv, wout_ref[...])


def _short_conv(p, bnd1, bnd2, conv_w, w_out, n_prompt):
    n = p.shape[0]
    n_prompt_tiles = n_prompt // TR
    cblk = COL_CONV // CONV_DIM
    carry_blk = TR // 8

    def row_spec(cb):
        return pl.BlockSpec((TR, CONV_DIM), lambda i: (i, cb))

    def carry_spec(cb):
        return pl.BlockSpec((8, CONV_DIM), lambda i: (jnp.maximum(i * carry_blk - 1, 0), cb))

    bnd_spec = pl.BlockSpec((TR, CONV_DIM), lambda i: (jnp.maximum(i - n_prompt_tiles, 0), 0))
    return pl.pallas_call(
        functools.partial(_conv_kernel, n_prompt_tiles),
        grid=(n // TR,),
        in_specs=[row_spec(cblk), row_spec(cblk + 1), row_spec(cblk + 2),
                  carry_spec(cblk + 1), carry_spec(cblk + 2), bnd_spec, bnd_spec,
                  pl.BlockSpec((8, CONV_DIM), lambda i: (0, 0)),
                  pl.BlockSpec((CONV_DIM, D), lambda i: (0, 0))],
        out_specs=[pl.BlockSpec((TR, D), lambda i: (i, 0)), pl.BlockSpec((TR, CONV_DIM), lambda i: (i, 0))],
        out_shape=[jax.ShapeDtypeStruct((n, D), F32), jax.ShapeDtypeStruct((n, CONV_DIM), F32)],
        compiler_params=_cparams(("arbitrary",)),
        name="short_conv",
    )(p, p, p, p, p, bnd1, bnd2, conv_w, w_out)


def _mem_kernel(q_ref, k_ref, v_ref, wo_ref, o_ref):
    n_seq = k_ref.shape[0]
    rows = q_ref.shape[0] // n_seq
    per_seq = []
    for s_i in range(n_seq):
        q = q_ref[s_i * rows:(s_i + 1) * rows, :]
        k = k_ref[s_i]
        v = v_ref[s_i]
        outs = []
        for h in range(MEM_HEADS):
            sl = slice(h * MEM_HEAD_DIM, (h + 1) * MEM_HEAD_DIM)
            s = _dot(q[:, sl], k[:, sl], _NT) * (MEM_HEAD_DIM ** -0.5)
            s = s - jnp.max(s, axis=-1, keepdims=True)
            e = jnp.exp(s)
            pr = e / jnp.sum(e, axis=-1, keepdims=True)
            outs.append(_dot(pr, v[:, sl]))
        per_seq.append(jnp.concatenate(outs, axis=1))
    o_ref[...] = _dot(jnp.concatenate(per_seq, axis=0), wo_ref[...])


MEM_SEQS = 4


def _mem_sample(p, mem_k, mem_v, w_o, row_start, n_seq):
    qblk = COL_Q // MEM_DIM
    rows = MEM_SEQS * SEQ_S
    rb0 = row_start // rows
    return pl.pallas_call(
        _mem_kernel,
        grid=(n_seq // MEM_SEQS,),
        in_specs=[pl.BlockSpec((rows, MEM_DIM), lambda i: (rb0 + i, qblk)),
                  pl.BlockSpec((MEM_SEQS, N_MEM, MEM_DIM), lambda i: (i, 0, 0)),
                  pl.BlockSpec((MEM_SEQS, N_MEM, MEM_DIM), lambda i: (i, 0, 0)),
                  pl.BlockSpec((MEM_DIM, D), lambda i: (0, 0))],
        out_specs=pl.BlockSpec((rows, D), lambda i: (i, 0)),
        out_shape=jax.ShapeDtypeStruct((n_seq * SEQ_S, D), F32),
        compiler_params=_cparams(("arbitrary",)),
        name="mem_attention_sample",
    )(p, mem_k, mem_v, w_o)


MEM_TILE = 256


def _mem_prompt_kernel(n_prompt_tiles, q_ref, k_ref, v_ref, wo_ref, tail_ref, o_ref):
    i = pl.program_id(0)

    @pl.when(i < n_prompt_tiles)
    def _():
        _mem_kernel(q_ref, k_ref, v_ref, wo_ref, o_ref)

    @pl.when(i >= n_prompt_tiles)
    def _():
        o_ref[...] = tail_ref[...]


def _mem_attention(p, mem_k, mem_v, w_o, o_sample, n_prompt):
    n = p.shape[0]
    qblk = COL_Q // MEM_DIM
    n_prompt_tiles = n_prompt // MEM_TILE
    return pl.pallas_call(
        functools.partial(_mem_prompt_kernel, n_prompt_tiles),
        grid=(n // MEM_TILE,),
        in_specs=[pl.BlockSpec((MEM_TILE, MEM_DIM), lambda i: (jnp.minimum(i, n_prompt_tiles - 1), qblk)),
                  pl.BlockSpec((1, N_MEM, MEM_DIM), lambda i: (0, 0, 0)),
                  pl.BlockSpec((1, N_MEM, MEM_DIM), lambda i: (0, 0, 0)),
                  pl.BlockSpec((MEM_DIM, D), lambda i: (0, 0)),
                  pl.BlockSpec((MEM_TILE, D), lambda i: (jnp.maximum(i - n_prompt_tiles, 0), 0))],
        out_specs=pl.BlockSpec((MEM_TILE, D), lambda i: (i, 0)),
        out_shape=jax.ShapeDtypeStruct((n, D), F32),
        compiler_params=_cparams(("arbitrary",)),
        name="mem_attention",
    )(p, mem_k, mem_v, w_o, o_sample)


def _layer_norm(z, g, b):
    mu = jnp.mean(z, axis=-1, keepdims=True)
    d = z - mu
    var = jnp.mean(d * d, axis=-1, keepdims=True)
    return d * lax.rsqrt(var + LN_EPS) * g + b


def _merge_kernel(n_prompt_tiles, xp_ref, xs_ref, ga_ref, gb_ref, gm_ref, y_ref, bonus_ref, g_ref, ocv_ref, omem_ref,
                  gng_ref, gnb_ref, bd_ref, wo_ref, l1g_ref, l1b_ref, wr_ref, h_o, lt_o):
    x = jnp.where(pl.program_id(0) < n_prompt_tiles, xp_ref[...], xs_ref[...])
    bd = bd_ref[...]
    y = y_ref[...]
    mean = _head_sum(y, bd) * (1.0 / HEAD)
    d = y - mean
    var = _head_sum(d * d, bd) * (1.0 / HEAD)
    yn = d * lax.rsqrt(var + GN_EPS) * gng_ref[...] + gnb_ref[...]
    o_rw = (yn + bonus_ref[...]) * g_ref[...]
    merged = (_sigmoid(ga_ref[...]) * o_rw + _sigmoid(gb_ref[...]) * ocv_ref[...]
              + _sigmoid(gm_ref[...]) * omem_ref[...])
    z = ALPHA * x + _dot(merged, wo_ref[...])
    h = _layer_norm(z, l1g_ref[...], l1b_ref[...])
    h_o[...] = h
    lt_o[...] = _dot3(wr_ref[...], h, _NT)


def _merge_ln1(xp, xs, p, y_raw, bonus, g, o_cv, o_mem, gn_g, gn_b, bd, w_o, ln_g, ln_b, w_router_t):
    n = p.shape[0]
    n_prompt_tiles = xp.shape[0] // TR
    gblk = COL_GATE // D
    row = pl.BlockSpec((TR, D), lambda i: (i, 0))
    xp_spec = pl.BlockSpec((TR, D), lambda i: (jnp.minimum(i, n_prompt_tiles - 1), 0))
    xs_spec = pl.BlockSpec((TR, D), lambda i: (jnp.maximum(i - n_prompt_tiles, 0), 0))

    def gate_spec(j):
        return pl.BlockSpec((TR, D), lambda i: (i, gblk + j))

    def const_spec(shape):
        return pl.BlockSpec(shape, lambda i: (0,) * len(shape))

    vec = const_spec((1, D))
    return pl.pallas_call(
        functools.partial(_merge_kernel, n_prompt_tiles),
        grid=(n // TR,),
        in_specs=[xp_spec, xs_spec, gate_spec(0), gate_spec(1), gate_spec(2), row, row, row, row, row,
                  vec, vec, const_spec((GROUP_LANES, GROUP_LANES)), const_spec((D, D)), vec, vec,
                  const_spec((N_EXPERTS, D))],
        out_specs=[row, pl.BlockSpec((N_EXPERTS, TR), lambda i: (0, i))],
        out_shape=[jax.ShapeDtypeStruct((n, D), F32), jax.ShapeDtypeStruct((N_EXPERTS, n), F32)],
        compiler_params=_cparams(("arbitrary",)),
        name="merge_ln1",
    )(xp, xs, p, p, p, y_raw, bonus, g, o_cv, o_mem, gn_g, gn_b, bd, w_o, ln_g, ln_b, w_router_t)


ROUTE_TILE = 256


def _routing_kernel(lt_ref, bias_ref, tri_ref, idx_o, w_o, pos_o, cnt_o, carry):
    i = pl.program_id(0)
    tile = lt_ref.shape[1]

    @pl.when(i == 0)
    def _():
        carry[...] = jnp.zeros_like(carry)

    neg_inf = -jnp.inf
    scores = _sigmoid(lt_ref[...])
    choice = scores + bias_ref[...]
    row = lax.broadcasted_iota(jnp.int32, (N_EXPERTS, tile), 0)
    rowf = row.astype(F32)
    grpf = (row // GROUP_SIZE).astype(F32)

    def group_allreduce(x, op):
        for s in (1, 2, 4):
            up = pltpu.roll(x, N_EXPERTS - s, 0)
            dn = pltpu.roll(x, s, 0)
            x = op(x, jnp.where((row & s) == 0, up, dn))
        return x

    m1 = group_allreduce(choice, jnp.maximum)
    first = group_allreduce(jnp.where(choice == m1, rowf, float(N_EXPERTS)), jnp.minimum)
    m2 = group_allreduce(jnp.where(rowf == first, neg_inf, choice), jnp.maximum)
    gscore = m1 + m2

    gsel = jnp.zeros_like(choice)
    for _ in range(TOPK_GROUPS):
        gmax = jnp.max(gscore, axis=0, keepdims=True)
        pick = jnp.min(jnp.where(gscore == gmax, grpf, float(N_GROUPS)), axis=0, keepdims=True)
        hit = grpf == pick
        gsel = jnp.where(hit, 1.0, gsel)
        gscore = jnp.where(hit, neg_inf, gscore)

    masked = jnp.where(gsel > 0.0, choice, neg_inf)
    row8 = lax.broadcasted_iota(jnp.int32, (TOP_K, tile), 0)
    idx_acc = jnp.zeros((TOP_K, tile), F32)
    w_acc = jnp.zeros((TOP_K, tile), F32)
    sel_all = jnp.zeros_like(choice)
    for kk in range(TOP_K):
        mx = jnp.max(masked, axis=0, keepdims=True)
        pick = jnp.min(jnp.where(masked == mx, rowf, float(N_EXPERTS)), axis=0, keepdims=True)
        hit = rowf == pick
        wk = jnp.sum(jnp.where(hit, scores, 0.0), axis=0, keepdims=True)
        idx_acc = jnp.where(row8 == kk, pick, idx_acc)
        w_acc = jnp.where(row8 == kk, wk, w_acc)
        sel_all = jnp.where(hit, 1.0, sel_all)
        masked = jnp.where(hit, neg_inf, masked)

    w_sum = jnp.sum(w_acc, axis=0, keepdims=True)
    w_o[...] = w_acc / w_sum * ROUTED_SCALE
    idx_o[...] = idx_acc.astype(jnp.int32)

    prefix = lax.dot_general(sel_all.astype(BF16), tri_ref[...], _NN, preferred_element_type=F32) + carry[...]
    pos_acc = jnp.zeros((TOP_K, tile), F32)
    for kk in range(TOP_K):
        hit = rowf == idx_acc[kk:kk + 1, :]
        pk = jnp.sum(jnp.where(hit, prefix, 0.0), axis=0, keepdims=True)
        pos_acc = jnp.where(row8 == kk, pk, pos_acc)
    pos_o[...] = pos_acc.astype(jnp.int32)
    carry[...] = carry[...] + jnp.sum(sel_all, axis=1, keepdims=True)
    cnt_o[...] = carry[...]


def _routing(logits_t, bias_col, tri):
    n = logits_t.shape[1]
    tile = ROUTE_TILE
    tok = pl.BlockSpec((TOP_K, tile), lambda i: (0, i))
    return pl.pallas_call(
        _routing_kernel,
        grid=(n // tile,),
        in_specs=[pl.BlockSpec((N_EXPERTS, tile), lambda i: (0, i)),
                  pl.BlockSpec((N_EXPERTS, 1), lambda i: (0, 0)),
                  pl.BlockSpec((tile, tile), lambda i: (0, 0))],
        out_specs=[tok, tok, tok, pl.BlockSpec((N_EXPERTS, 1), lambda i: (0, 0))],
        out_shape=[jax.ShapeDtypeStruct((TOP_K, n), jnp.int32), jax.ShapeDtypeStruct((TOP_K, n), F32),
                   jax.ShapeDtypeStruct((TOP_K, n), jnp.int32), jax.ShapeDtypeStruct((N_EXPERTS, 1), F32)],
        scratch_shapes=[pltpu.VMEM((N_EXPERTS, 1), F32)],
        compiler_params=_cparams(("arbitrary",)),
        name="routing",
    )(logits_t, bias_col, tri)


ZERO_ROWS = 128


def _dispatch_kernel(zs_ref, zc_ref, dest_ref, h_ref, xb_out, dest_smem, zbuf, sem, idx_sem, zsem):
    i = pl.program_id(0)
    rows = h_ref.shape[0]

    @pl.when(i == 0)
    def _():
        zbuf[...] = jnp.zeros_like(zbuf)

        def zero_copy(piece):
            dst0 = pl.multiple_of(piece * ZERO_ROWS, ZERO_ROWS)
            return pltpu.make_async_copy(zbuf, xb_out.at[pl.ds(dst0, ZERO_ROWS), :], zsem)

        def per_range(e, c):
            def issue_piece(j, c2):
                zero_copy(zs_ref[e] + j).start()
                return c2
            lax.fori_loop(0, zc_ref[e], issue_piece, 0)
            return c

        def per_range_wait(e, c):
            def wait_piece(j, c2):
                zero_copy(zs_ref[e] + j).wait()
                return c2
            lax.fori_loop(0, zc_ref[e], wait_piece, 0)
            return c

        lax.fori_loop(0, N_EXPERTS + 1, per_range, 0)
        lax.fori_loop(0, N_EXPERTS + 1, per_range_wait, 0)

    cp = pltpu.make_async_copy(dest_ref, dest_smem, idx_sem)
    cp.start()
    cp.wait()

    def row_copy(t, k):
        return pltpu.make_async_copy(h_ref.at[pl.ds(t, 1), :],
                                     xb_out.at[pl.ds(dest_smem[k, t], 1), :], sem)

    def issue(t, c):
        for k in range(TOP_K):
            row_copy(t, k).start()
        return c

    def drain(t, c):
        for k in range(TOP_K):
            row_copy(t, k).wait()
        return c

    lax.fori_loop(0, rows, issue, 0)
    lax.fori_loop(0, rows, drain, 0)


def _dispatch(zero_start, zero_count, dest_t, h, n_rows):
    n = h.shape[0]
    grid_spec = pltpu.PrefetchScalarGridSpec(
        num_scalar_prefetch=2,
        grid=(n // TR,),
        in_specs=[pl.BlockSpec((TOP_K, TR), lambda i, zs, zc: (0, i)),
                  pl.BlockSpec((TR, D), lambda i, zs, zc: (i, 0))],
        out_specs=pl.BlockSpec(memory_space=pl.ANY),
        scratch_shapes=[pltpu.SMEM((TOP_K, TR), jnp.int32), pltpu.VMEM((ZERO_ROWS, D), F32),
                        pltpu.SemaphoreType.DMA, pltpu.SemaphoreType.DMA, pltpu.SemaphoreType.DMA],
    )
    return pl.pallas_call(
        _dispatch_kernel,
        grid_spec=grid_spec,
        out_shape=jax.ShapeDtypeStruct((n_rows, D), F32),
        compiler_params=_cparams(("arbitrary",)),
        name="moe_dispatch",
    )(zero_start, zero_count, dest_t, h)


def _silu(x):
    return x * _sigmoid(x)


def _expert_kernel(be_ref, nu_ref, x_ref, wu_ref, wd_ref, o_ref, wu_bf, wd_bf):
    b = pl.program_id(0)
    changed = jnp.logical_or(b == 0, be_ref[b] != be_ref[jnp.maximum(b - 1, 0)])

    @pl.when(changed)
    def _():
        wu_bf[...] = wu_ref[0].astype(BF16)
        wd_bf[...] = wd_ref[0].astype(BF16)

    @pl.when(b < nu_ref[0])
    def _():
        up = _dot(x_ref[...], wu_bf[...])
        act = _silu(up[:, :EXPERT_FF]) * up[:, EXPERT_FF:]
        o_ref[...] = _dot(act, wd_bf[...])

    @pl.when(b >= nu_ref[0])
    def _():
        o_ref[...] = jnp.zeros_like(o_ref)


def _experts(block_e, n_used, xb, w_up, w_down):
    rows = xb.shape[0]
    nb = rows // EXPERT_BM

    def xmap(b, be, nu):
        return (jnp.minimum(b, nu[0] - 1), 0)

    grid_spec = pltpu.PrefetchScalarGridSpec(
        num_scalar_prefetch=2,
        grid=(nb,),
        in_specs=[pl.BlockSpec((EXPERT_BM, D), xmap),
                  pl.BlockSpec((1, D, 2 * EXPERT_FF), lambda b, be, nu: (be[b], 0, 0)),
                  pl.BlockSpec((1, EXPERT_FF, D), lambda b, be, nu: (be[b], 0, 0))],
        out_specs=pl.BlockSpec((EXPERT_BM, D), lambda b, be, nu: (b, 0)),
        scratch_shapes=[pltpu.VMEM((D, 2 * EXPERT_FF), BF16), pltpu.VMEM((EXPERT_FF, D), BF16)],
    )
    return pl.pallas_call(
        _expert_kernel,
        grid_spec=grid_spec,
        out_shape=jax.ShapeDtypeStruct((rows, D), F32),
        compiler_params=_cparams(("arbitrary",)),
        name="moe_experts",
    )(block_e, n_used, xb, w_up, w_down)


SHARED_TILE = 512


def _shared_kernel(h_ref, wu_ref, wd_ref, o_ref):
    up = _dot(h_ref[...], wu_ref[...])
    act = _silu(up[:, :SHARED_FF]) * up[:, SHARED_FF:]
    o_ref[...] = _dot(act, wd_ref[...])


def _shared_ffn(h, w_up, w_down):
    n = h.shape[0]
    row = pl.BlockSpec((SHARED_TILE, D), lambda i: (i, 0))
    return pl.pallas_call(
        _shared_kernel,
        grid=(n // SHARED_TILE,),
        in_specs=[row, pl.BlockSpec((D, 2 * SHARED_FF), lambda i: (0, 0)),
                  pl.BlockSpec((SHARED_FF, D), lambda i: (0, 0))],
        out_specs=row,
        out_shape=jax.ShapeDtypeStruct((n, D), F32),
        compiler_params=_cparams(("parallel",)),
        name="shared_ffn",
    )(h, w_up, w_down)


def _combine_kernel(n_prompt_tiles, dest_ref, w_ref, h_ref, sh_ref, yb_ref, l2g_ref, l2b_ref, yp_o, ys_o,
                    buf, dest_smem, sem, idx_sem):
    i = pl.program_id(0)
    rows = h_ref.shape[0]
    cp = pltpu.make_async_copy(dest_ref, dest_smem, idx_sem)
    cp.start()
    cp.wait()

    def row_copy(t, k):
        return pltpu.make_async_copy(yb_ref.at[pl.ds(dest_smem[k, t], 1), :],
                                     buf.at[k, pl.ds(t, 1), :], sem)

    def issue(t, c):
        for k in range(TOP_K):
            row_copy(t, k).start()
        return c

    def drain(t, c):
        for k in range(TOP_K):
            row_copy(t, k).wait()
        return c

    lax.fori_loop(0, rows, issue, 0)
    lax.fori_loop(0, rows, drain, 0)
    w = w_ref[...]
    f = sh_ref[...]
    for k in range(TOP_K):
        f = f + w[:, k:k + 1] * buf[k]
    z = ALPHA * h_ref[...] + f
    y = _layer_norm(z, l2g_ref[...], l2b_ref[...])

    @pl.when(i < n_prompt_tiles)
    def _():
        yp_o[...] = y

    @pl.when(i >= n_prompt_tiles)
    def _():
        ys_o[...] = y


def _combine_ln2(dest_t, w_tok, h, shared, yb, ln_g, ln_b, n_prompt):
    n = h.shape[0]
    n_prompt_tiles = n_prompt // TR
    row = pl.BlockSpec((TR, D), lambda i: (i, 0))
    vec = pl.BlockSpec((1, D), lambda i: (0, 0))
    return pl.pallas_call(
        functools.partial(_combine_kernel, n_prompt_tiles),
        grid=(n // TR,),
        in_specs=[pl.BlockSpec((TOP_K, TR), lambda i: (0, i)),
                  pl.BlockSpec((TR, TOP_K), lambda i: (i, 0)),
                  row, row, pl.BlockSpec(memory_space=pl.ANY), vec, vec],
        out_specs=[pl.BlockSpec((TR, D), lambda i: (jnp.minimum(i, n_prompt_tiles - 1), 0)),
                   pl.BlockSpec((TR, D), lambda i: (jnp.maximum(i - n_prompt_tiles, 0), 0))],
        out_shape=[jax.ShapeDtypeStruct((n_prompt, D), F32), jax.ShapeDtypeStruct((n - n_prompt, D), F32)],
        scratch_shapes=[pltpu.VMEM((TOP_K, TR, D), F32), pltpu.SMEM((TOP_K, TR), jnp.int32),
                        pltpu.SemaphoreType.DMA, pltpu.SemaphoreType.DMA],
        compiler_params=_cparams(("arbitrary",)),
        name="moe_combine_ln2",
    )(dest_t, w_tok, h, shared, yb, ln_g, ln_b)


def _rw_cols_split(v):
    pad = jnp.zeros(v.shape[:-1] + (LORA_PAD - LORA_W,), v.dtype)
    lora = jnp.concatenate([v[..., 3 * D:3 * D + LORA_W], pad,
                            v[..., 3 * D + LORA_W:3 * D + LORA_W + LORA_A], pad,
                            v[..., 3 * D + LORA_W + LORA_A:]], axis=-1)
    return v[..., 0:3 * D], lora


def _pad_rows(w, rows):
    return jnp.concatenate([w, jnp.zeros((rows - w.shape[0],) + w.shape[1:], w.dtype)], axis=0)


def kernel(x_prompt, x_sample, mem_prompt, state_rwkv, state_shift, state_conv, cache_mem_k, cache_mem_v,
           w_in, mu_shift, rw_w0, rw_w2, rw_a0, rw_a2, rw_g2, rw_k_k, rw_k_a, rw_r_k, rw_gn_g, rw_gn_b,
           conv_w, w_conv_out, w_mem_k, w_mem_v, w_mem_o, w_o, ln1_g, ln1_b, w_router, router_bias,
           w_exp_up, w_exp_down, w_sh_up, w_sh_down, ln2_g, ln2_b):
    n_prompt = x_prompt.shape[0] * x_prompt.shape[1]
    n_seq_s, seq_s = x_sample.shape[0], x_sample.shape[1]
    n_sample = n_seq_s * seq_s
    n = n_prompt + n_sample
    assert x_prompt.shape[0] == 1 and seq_s == SEQ_S and n_prompt % TR == 0 and n_sample % TR == 0
    assert n % SHARED_TILE == 0 and n % ROUTE_TILE == 0 and w_in.shape[0] == 1
    assert n_prompt % PROJ_TM == 0 and n_sample % PROJ_TM == 0

    xp = x_prompt.reshape(n_prompt, D)
    xs = x_sample.reshape(n_sample, D)

    def vec(v):
        return v.reshape(1, -1).astype(F32)

    w_in_r = _wt_relayout(w_in[0].T)
    p = _in_proj(xp, xs, w_in_r)

    w_kv = jnp.concatenate([w_mem_k[0], w_mem_v[0]], axis=1).astype(BF16)
    kv = _matmul(mem_prompt[0], w_kv, N_MEM, 512, "mem_kv")
    mem_k_p, mem_v_p = kv[:, :MEM_DIM], kv[:, MEM_DIM:]

    mu_rkv, mu_lora = _rw_cols_split(vec(mu_shift[0]))
    sh_rkv, sh_lora = _rw_cols_split(state_shift[0, :, 0, :])
    bnd_rkv = jnp.repeat(sh_rkv, seq_s, axis=0)
    bnd_lora = jnp.repeat(sh_lora, seq_s, axis=0)
    hi = lax.broadcasted_iota(jnp.int32, (GROUP_LANES, GROUP_LANES), 0) // HEAD
    hj = lax.broadcasted_iota(jnp.int32, (GROUP_LANES, GROUP_LANES), 1) // HEAD
    bd = (hi == hj).astype(BF16)
    r, k, v, kk, b, lw, g, bonus = _rwkv_prep(
        p, bnd_rkv, bnd_lora, mu_rkv, mu_lora, vec(rw_w0[0]), vec(rw_a0[0]), vec(rw_k_k[0]), vec(rw_k_a[0]),
        vec(rw_r_k[0]), _pad_rows(rw_w2[0], LORA_PAD).astype(BF16), _pad_rows(rw_a2[0], LORA_PAD).astype(BF16),
        rw_g2[0].astype(BF16), bd, n_prompt)

    s_sample = jnp.transpose(state_rwkv[0], (0, 2, 1, 3)).reshape(n_seq_s, HEAD, D)
    s_in = jnp.concatenate([jnp.zeros((STATE_SLOTS, HEAD, D), F32), s_sample.astype(F32)], axis=0)
    y_raw, s_out = _rwkv_scan(r, k, v, kk, b, lw, s_in, n_prompt)

    bnd1 = jnp.repeat(state_conv[0, :, 1, :], seq_s, axis=0)
    bnd2 = jnp.repeat(state_conv[0, :, 0, :], seq_s, axis=0)
    o_cv, u = _short_conv(p, bnd1, bnd2, _pad_rows(conv_w[0], 8), w_conv_out[0].astype(BF16), n_prompt)

    w_mem_o_b = w_mem_o[0].astype(BF16)
    o_mem_s = _mem_sample(p, cache_mem_k[0].reshape(n_seq_s, N_MEM, MEM_DIM),
                          cache_mem_v[0].reshape(n_seq_s, N_MEM, MEM_DIM), w_mem_o_b, n_prompt, n_seq_s)
    o_mem = _mem_attention(p, mem_k_p[None], mem_v_p[None], w_mem_o_b, o_mem_s, n_prompt)

    h, logits_t = _merge_ln1(xp, xs, p, y_raw, bonus, g, o_cv, o_mem, vec(rw_gn_g[0]), vec(rw_gn_b[0]), bd,
                             w_o[0].astype(BF16), vec(ln1_g[0]), vec(ln1_b[0]), w_router[0].T)

    ti = lax.broadcasted_iota(jnp.int32, (ROUTE_TILE, ROUTE_TILE), 0)
    tj = lax.broadcasted_iota(jnp.int32, (ROUTE_TILE, ROUTE_TILE), 1)
    tri = (ti < tj).astype(BF16)
    idx_t, w_t, pos_t, counts = _routing(logits_t, router_bias[0].reshape(N_EXPERTS, 1).astype(F32), tri)

    counts = counts[:, 0].astype(jnp.int32)
    padded = (counts + EXPERT_BM - 1) // EXPERT_BM * EXPERT_BM
    seg_end = jnp.cumsum(padded)
    seg_start = seg_end - padded
    expert_ids = jnp.arange(N_EXPERTS, dtype=jnp.int32)
    dest_t = pos_t + jnp.sum(
        jnp.where(idx_t[None] == expert_ids[:, None, None], seg_start[:, None, None], 0), axis=0)
    nb = (n * TOP_K) // EXPERT_BM + N_EXPERTS
    block_rows = jnp.arange(nb, dtype=jnp.int32) * EXPERT_BM
    block_e = jnp.minimum(jnp.sum((seg_end[None, :] <= block_rows[:, None]).astype(jnp.int32), axis=1),
                          N_EXPERTS - 1)
    n_used = (seg_end[-1:] // EXPERT_BM).astype(jnp.int32)

    pieces_per_block = EXPERT_BM // ZERO_ROWS
    valid_last = counts - (padded - EXPERT_BM)
    first_piece = valid_last // ZERO_ROWS
    zero_start = jnp.where(padded > 0, (seg_end - EXPERT_BM) // ZERO_ROWS + first_piece, 0)
    zero_count = jnp.where(padded > 0, pieces_per_block - first_piece, 0)
    total_pieces = nb * pieces_per_block
    zero_start = jnp.concatenate([zero_start, seg_end[-1:] // ZERO_ROWS]).astype(jnp.int32)
    zero_count = jnp.concatenate([zero_count, total_pieces - seg_end[-1:] // ZERO_ROWS]).astype(jnp.int32)

    xb = _dispatch(zero_start, zero_count, dest_t, h, nb * EXPERT_BM)
    yb = _experts(block_e, n_used, xb, w_exp_up[0], w_exp_down[0])
    shared = _shared_ffn(h, w_sh_up[0].astype(BF16), w_sh_down[0].astype(BF16))
    y_p, y_s = _combine_ln2(dest_t, w_t.T, h, shared, yb, vec(ln2_g[0]), vec(ln2_b[0]), n_prompt)

    dt = x_prompt.dtype
    y_p = y_p.reshape(x_prompt.shape)
    y_s = y_s.reshape(x_sample.shape)

    def state_out(s):
        q = s.reshape(s.shape[0], HEAD, N_HEADS, HEAD)
        return jnp.transpose(q, (0, 2, 1, 3))[None].astype(dt)

    rw_p = state_out(s_out[0:1])
    rw_s = state_out(s_out[STATE_SLOTS:])

    last_rows = jnp.concatenate([jnp.array([n_prompt - 1], jnp.int32),
                                 n_prompt + seq_s - 1 + seq_s * jnp.arange(n_seq_s, dtype=jnp.int32)])
    p_last = p[last_rows]
    shift = jnp.concatenate([p_last[:, 0:3 * D],
                             p_last[:, COL_LORA:COL_LORA + LORA_W],
                             p_last[:, COL_LORA + LORA_PAD:COL_LORA + LORA_PAD + LORA_A],
                             p_last[:, COL_LORA + 2 * LORA_PAD:]], axis=1)
    sh_p = shift[0:1].reshape(1, 1, 1, RW_COLS)
    sh_s = shift[1:].reshape(1, n_seq_s, 1, RW_COLS)

    cv_p = u[n_prompt - 2:n_prompt].reshape(1, 1, 2, CONV_DIM)
    cv_s = u[n_prompt:].reshape(n_seq_s, seq_s, CONV_DIM)[:, seq_s - 2:, :][None]

    mk_p = mem_k_p.reshape(1, 1, N_MEM, MEM_HEADS, MEM_HEAD_DIM)
    mv_p = mem_v_p.reshape(1, 1, N_MEM, MEM_HEADS, MEM_HEAD_DIM)
    return (y_p, y_s, rw_p, sh_p, cv_p, mk_p, mv_p, rw_s, sh_s, cv_s)
```

```python
import functools

import jax
import jax.numpy as jnp
from jax import lax
from jax.experimental import pallas as pl
from jax.experimental.pallas import tpu as pltpu

F32 = jnp.float32
BF16 = jnp.bfloat16

D = 2048
HEAD = 64
N_HEADS = D // HEAD
LORA_W = 96
LORA_A = 96
LORA_G = 256
DECAY_SCALE = 0.6065306597126334
GN_EPS = HEAD * 1e-5
CONV_DIM = D // 2
N_MEM = 256
MEM_HEADS = 4
MEM_HEAD_DIM = 256
MEM_DIM = MEM_HEADS * MEM_HEAD_DIM
N_EXPERTS = 64
N_GROUPS = 8
GROUP_SIZE = N_EXPERTS // N_GROUPS
TOPK_GROUPS = 4
TOP_K = 8
EXPERT_FF = 512
SHARED_FF = 512
ROUTED_SCALE = 2.5
LN_EPS = 1e-5
DEPTH = 1
ALPHA = (2 * DEPTH) ** 0.25
RW_COLS = 3 * D + LORA_W + LORA_A + LORA_G

LORA_PAD = 128
LORA_COLS = 2 * LORA_PAD + LORA_G
COL_RKV = 0
COL_GATE = 3 * D
COL_CONV = 6 * D
COL_Q = COL_CONV + 3 * CONV_DIM
COL_LORA = COL_Q + MEM_DIM
P_COLS = COL_LORA + LORA_COLS

CHUNK = 16
GROUP_HEADS = 4
GROUP_LANES = GROUP_HEADS * HEAD
N_LANE_GROUPS = D // GROUP_LANES
STACK = GROUP_HEADS * CHUNK
SEQ_S = 16
STATE_SLOTS = 8

TR = 128
SCAN_ROWS = STATE_SLOTS * CHUNK
EXPERT_BM = 256
VMEM_LIMIT = 56 * 1024 * 1024


def _cparams(sem):
    return pltpu.CompilerParams(dimension_semantics=sem, vmem_limit_bytes=VMEM_LIMIT)


def _sigmoid(x):
    return 1.0 / (1.0 + jnp.exp(-x))


def _dot(a, b, dims=(((1,), (0,)), ((), ()))):
    return lax.dot_general(a.astype(BF16), b.astype(BF16), dims, preferred_element_type=F32)


_NN = (((1,), (0,)), ((), ()))
_NT = (((1,), (1,)), ((), ()))
_TN = (((0,), (0,)), ((), ()))


def _split2(x):
    hi = x.astype(BF16)
    lo = (x - hi.astype(F32)).astype(BF16)
    return hi, lo


def _split3(x):
    hi = x.astype(BF16)
    r1 = x - hi.astype(F32)
    mid = r1.astype(BF16)
    lo = (r1 - mid.astype(F32)).astype(BF16)
    return hi, mid, lo


def _dot3(a, b, dims=_NN):
    ah, al = _split2(a)
    bh, bl = _split2(b)
    f = functools.partial(lax.dot_general, dimension_numbers=dims, preferred_element_type=F32)
    return f(ah, bh) + (f(ah, bl) + f(al, bh))


def _dot_exact_rhs(a, b_bf16, dims=_NN):
    hi, mid, lo = _split3(a)
    f = functools.partial(lax.dot_general, dimension_numbers=dims, preferred_element_type=F32)
    return f(hi, b_bf16) + (f(mid, b_bf16) + f(lo, b_bf16))


def _dot_exact_lhs(a_bf16, b):
    hi, mid, lo = _split3(b)
    f = functools.partial(lax.dot_general, dimension_numbers=_NN, preferred_element_type=F32)
    return f(a_bf16, hi) + (f(a_bf16, mid) + f(a_bf16, lo))


_sdot = _dot


def _mm_kernel(x_ref, w_ref, o_ref):
    o_ref[...] = _dot(x_ref[...], w_ref[...]).astype(o_ref.dtype)


def _matmul(x, w, tm, tn, name):
    m, k = x.shape
    n = w.shape[1]
    return pl.pallas_call(
        _mm_kernel,
        grid=(m // tm, n // tn),
        in_specs=[pl.BlockSpec((tm, k), lambda i, j: (i, 0)),
                  pl.BlockSpec((k, tn), lambda i, j: (0, j))],
        out_specs=pl.BlockSpec((tm, tn), lambda i, j: (i, j)),
        out_shape=jax.ShapeDtypeStruct((m, n), F32),
        compiler_params=_cparams(("parallel", "arbitrary")),
        name=name,
    )(x, w)


WT_BLOCK = 512
WT_PIECE = 128


def _wt_relayout_kernel(src_ref, wt_hbm, o_ref, buf, sems):
    j = pl.program_id(0)
    n_plain = pl.num_programs(0) - 1
    n_pieces = WT_BLOCK // WT_PIECE

    def emit():
        for s in range(n_pieces):
            rows = slice(s * WT_PIECE, (s + 1) * WT_PIECE)
            o_ref[:, rows] = buf[rows, :].T.astype(BF16)

    @pl.when(j < n_plain)
    def _():
        row0 = pl.multiple_of(src_ref[j], 8)
        copies = [pltpu.make_async_copy(wt_hbm.at[pl.ds(row0 + s * WT_PIECE, WT_PIECE), :],
                                        buf.at[pl.ds(s * WT_PIECE, WT_PIECE), :], sems.at[s])
                  for s in range(n_pieces)]
        for c in copies:
            c.start()
        for c in copies:
            c.wait()
        emit()

    @pl.when(j == n_plain)
    def _():
        lo_w = 3 * D
        lo_a = lo_w + LORA_W
        lo_g = lo_a + LORA_A
        pieces = ((lo_w, 0, LORA_W), (lo_a, LORA_PAD, LORA_A), (lo_g, 2 * LORA_PAD, LORA_G))
        for _, dst, width in pieces[:2]:
            buf[dst + width:dst + LORA_PAD, :] = jnp.zeros((LORA_PAD - width, buf.shape[1]), F32)
        copies = [pltpu.make_async_copy(wt_hbm.at[pl.ds(src, width), :], buf.at[pl.ds(dst, width), :], sems.at[n])
                  for n, (src, dst, width) in enumerate(pieces)]
        for c in copies:
            c.start()
        for c in copies:
            c.wait()
        emit()


def _wt_relayout(wt):
    k = wt.shape[1]
    rw_end = RW_COLS
    cv_end = rw_end + 3 * CONV_DIM
    q_end = cv_end + MEM_DIM
    src = []
    for dst0, src0, width in ((COL_RKV, 0, 3 * D), (COL_GATE, q_end, 3 * D), (COL_CONV, rw_end, 3 * CONV_DIM),
                              (COL_Q, cv_end, MEM_DIM)):
        assert dst0 == len(src) * WT_BLOCK and width % WT_BLOCK == 0
        src += [src0 + b * WT_BLOCK for b in range(width // WT_BLOCK)]
    assert len(src) * WT_BLOCK == COL_LORA and LORA_COLS == WT_BLOCK
    grid_spec = pltpu.PrefetchScalarGridSpec(
        num_scalar_prefetch=1,
        grid=(len(src) + 1,),
        in_specs=[pl.BlockSpec(memory_space=pl.ANY)],
        out_specs=pl.BlockSpec((k, WT_BLOCK), lambda j, src_rows: (0, j)),
        scratch_shapes=[pltpu.VMEM((WT_BLOCK, k), F32), pltpu.SemaphoreType.DMA((WT_BLOCK // WT_PIECE,))],
    )
    return pl.pallas_call(
        _wt_relayout_kernel,
        grid_spec=grid_spec,
        out_shape=jax.ShapeDtypeStruct((k, P_COLS), BF16),
        compiler_params=_cparams(("arbitrary",)),
        name="w_in_relayout",
    )(jnp.asarray(src, jnp.int32), wt)


PROJ_TM = 512
PROJ_TN = 1536


def _in_proj_kernel(n_prompt_tiles, xp_ref, xs_ref, w_ref, o_ref, x_bf):
    i = pl.program_id(0)

    @pl.when(pl.program_id(1) == 0)
    def _():
        x_bf[...] = jnp.where(i < n_prompt_tiles, xp_ref[...], xs_ref[...]).astype(BF16)

    o_ref[...] = jnp.dot(x_bf[...], w_ref[...], preferred_element_type=F32)


def _in_proj(xp, xs, w):
    k = xp.shape[1]
    n_prompt_tiles = xp.shape[0] // PROJ_TM
    n_tiles = n_prompt_tiles + xs.shape[0] // PROJ_TM
    ncols = w.shape[1]
    return pl.pallas_call(
        functools.partial(_in_proj_kernel, n_prompt_tiles),
        grid=(n_tiles, ncols // PROJ_TN),
        in_specs=[pl.BlockSpec((PROJ_TM, k), lambda i, j: (jnp.minimum(i, n_prompt_tiles - 1), 0)),
                  pl.BlockSpec((PROJ_TM, k), lambda i, j: (jnp.maximum(i - n_prompt_tiles, 0), 0)),
                  pl.BlockSpec((k, PROJ_TN), lambda i, j: (0, j))],
        out_specs=pl.BlockSpec((PROJ_TM, PROJ_TN), lambda i, j: (i, j)),
        out_shape=jax.ShapeDtypeStruct((n_tiles * PROJ_TM, ncols), F32),
        scratch_shapes=[pltpu.VMEM((PROJ_TM, k), BF16)],
        compiler_params=_cparams(("arbitrary", "arbitrary")),
        name="in_proj",
    )(xp, xs, w)


def _head_sum(x, bd):
    parts = []
    for g in range(N_LANE_GROUPS):
        parts.append(_dot_exact_rhs(x[:, g * GROUP_LANES:(g + 1) * GROUP_LANES], bd))
    return jnp.concatenate(parts, axis=1)


def _prep_kernel(n_prompt_tiles, rkv_ref, lora_ref, c_rkv_ref, c_lora_ref, b_rkv_ref, b_lora_ref,
                 mu_rkv_ref, mu_lora_ref, w0_ref, a0_ref, kk_ref, ka_ref, rk_ref,
                 w2_ref, a2_ref, g2_ref, bd_ref,
                 r_o, k_o, v_o, kk_o, b_o, lw_o, g_o, bonus_o):
    i = pl.program_id(0)
    rows = rkv_ref.shape[0]
    row = lax.broadcasted_iota(jnp.int32, (rows, 1), 0)
    is_sample = i >= n_prompt_tiles
    seq_start = jnp.logical_and(is_sample, (row % SEQ_S) == 0)

    def mixed(x, carry_row, bnd, mu):
        prev = pltpu.roll(x, 1, 0)
        carry_row = jnp.where(i == 0, 0.0, carry_row)
        prev = jnp.where(row == 0, carry_row, prev)
        prev = jnp.where(seq_start, bnd, prev)
        return x + (prev - x) * mu

    def section(s):
        sl = slice(s * D, (s + 1) * D)
        return mixed(rkv_ref[:, sl], c_rkv_ref[7:8, sl], b_rkv_ref[:, sl], mu_rkv_ref[:, sl])

    lo = mixed(lora_ref[...], c_lora_ref[7:8, :], b_lora_ref[...], mu_lora_ref[...])
    w_lo = lo[:, 0:LORA_PAD]
    a_lo = lo[:, LORA_PAD:2 * LORA_PAD]
    g_lo = lo[:, 2 * LORA_PAD:]
    log_w = -DECAY_SCALE * _sigmoid(w0_ref[...] + _dot(jnp.tanh(w_lo), w2_ref[...]))
    a = _sigmoid(a0_ref[...] + _dot(a_lo, a2_ref[...]))
    g_o[...] = _dot(_sigmoid(g_lo), g2_ref[...])
    lw_o[...] = log_w

    bd = bd_ref[...]
    k = section(1)
    kk = k * kk_ref[...]
    ss = _head_sum(kk * kk, bd)
    kk = kk * lax.rsqrt(jnp.maximum(ss, 1e-24))
    kk_o[...] = kk
    b_o[...] = kk * a
    k = k * (1.0 + (a - 1.0) * ka_ref[...])
    k_o[...] = k
    r = section(0)
    r_o[...] = r
    v = section(2)
    v_o[...] = v
    bonus_o[...] = _head_sum(r * k * rk_ref[...], bd) * v


def _rwkv_prep(p, bnd_rkv, bnd_lora, mu_rkv, mu_lora, w0, a0, k_k, k_a, r_k, w2p, a2p, g2, bd, n_prompt):
    n = p.shape[0]
    n_prompt_tiles = n_prompt // TR
    carry_blk = TR // 8
    lora_blk = COL_LORA // LORA_COLS

    def row_spec(cols, cb=0):
        return pl.BlockSpec((TR, cols), lambda i: (i, cb))

    def carry_spec(cols, cb=0):
        return pl.BlockSpec((8, cols), lambda i: (jnp.maximum(i * carry_blk - 1, 0), cb))

    def bnd_spec(cols):
        return pl.BlockSpec((TR, cols), lambda i: (jnp.maximum(i - n_prompt_tiles, 0), 0))

    def const_spec(shape):
        return pl.BlockSpec(shape, lambda i: (0,) * len(shape))

    out = jax.ShapeDtypeStruct((n, D), F32)
    return pl.pallas_call(
        functools.partial(_prep_kernel, n_prompt_tiles),
        grid=(n // TR,),
        in_specs=[row_spec(3 * D), row_spec(LORA_COLS, lora_blk),
                  carry_spec(3 * D), carry_spec(LORA_COLS, lora_blk),
                  bnd_spec(3 * D), bnd_spec(LORA_COLS),
                  const_spec((1, 3 * D)), const_spec((1, LORA_COLS)),
                  const_spec((1, D)), const_spec((1, D)), const_spec((1, D)), const_spec((1, D)),
                  const_spec((1, D)),
                  const_spec((LORA_PAD, D)), const_spec((LORA_PAD, D)), const_spec((LORA_G, D)),
                  const_spec((GROUP_LANES, GROUP_LANES))],
        out_specs=[row_spec(D)] * 8,
        out_shape=[out] * 8,
        compiler_params=_cparams(("arbitrary",)),
        name="rwkv_prep",
    )(p, p, p, p, bnd_rkv, bnd_lora, mu_rkv, mu_lora, w0, a0, k_k, k_a, r_k, w2p, a2p, g2, bd)


def _scan_kernel(n_prompt_tiles, r_ref, k_ref, v_ref, kk_ref, b_ref, lw_ref, s_in_ref, y_ref, s_out_ref, s_scr):
    i = pl.program_id(0)
    is_sample = i >= n_prompt_tiles
    n_chunks = r_ref.shape[0] // CHUNK

    lane = lax.broadcasted_iota(jnp.int32, (1, GROUP_LANES), 1)
    head_masks = [(lane // HEAD == h).astype(F32) for h in range(GROUP_HEADS)]
    ri = lax.broadcasted_iota(jnp.int32, (STACK, 2 * STACK), 0)
    ci = lax.broadcasted_iota(jnp.int32, (STACK, 2 * STACK), 1)
    same_head = (ri // CHUNK) == ((ci % STACK) // CHUNK)
    strict_lower = jnp.logical_and(same_head, (ci % CHUNK) < (ri % CHUNK))
    mask_incl = jnp.logical_and(same_head, (ci % CHUNK) <= (ri % CHUNK)).astype(F32)
    mask_strict_b = jnp.logical_and(strict_lower, ci < STACK).astype(F32)
    mask_strict_k = jnp.logical_and(strict_lower, ci >= STACK).astype(F32)
    eye = (ri == ci).astype(F32)
    trow = lax.broadcasted_iota(jnp.int32, (CHUNK, 1), 0)

    def running_sum(x):
        d = 1
        while d < CHUNK:
            x = x + jnp.where(trow >= d, pltpu.roll(x, d, 0), 0.0)
            d *= 2
        return x

    rb = lax.broadcasted_iota(jnp.int32, (GROUP_LANES, GROUP_LANES), 0)
    cb = lax.broadcasted_iota(jnp.int32, (GROUP_LANES, GROUP_LANES), 1)
    block_diag = ((rb // HEAD) == (cb // HEAD)).astype(F32)

    def stack(x):
        return jnp.concatenate([x * m for m in head_masks], axis=0)

    def unstack(x):
        out = x[0:CHUNK]
        for h in range(1, GROUP_HEADS):
            out = out + x[h * CHUNK:(h + 1) * CHUNK]
        return out

    def compact(s):
        out = s[0:HEAD]
        for h in range(1, GROUP_HEADS):
            out = out + s[h * HEAD:(h + 1) * HEAD]
        return out

    @pl.when(i == 0)
    def _():
        s_out_ref[...] = jnp.zeros_like(s_out_ref)

    groups = range(N_LANE_GROUPS)
    lanes = [slice(g * GROUP_LANES, (g + 1) * GROUP_LANES) for g in groups]

    def twice(x):
        return jnp.concatenate([x, x], axis=0)

    def a0(c):
        rows = slice(c * CHUNK, (c + 1) * CHUNK)
        st = {"rows": rows}
        st["lw"] = [lw_ref[rows, lanes[g]] for g in groups]
        st["cum"] = [running_sum(st["lw"][g]) for g in groups]
        return st

    def a1(st):
        rows = st["rows"]
        lhs_s, bk_s, v_s, kr_t, p_end = [], [], [], [], []
        for g in groups:
            cum, lw = st["cum"][g], st["lw"][g]
            e_incl = jnp.exp(cum)
            e_excl = jnp.exp(cum - lw)
            e_neg = jnp.exp(-cum)
            p_end.append(e_incl[CHUNK - 1:CHUNK, :])
            r_t = r_ref[rows, lanes[g]] * e_incl
            kk_t = kk_ref[rows, lanes[g]] * e_excl
            b_t = b_ref[rows, lanes[g]] * e_neg
            k_t = k_ref[rows, lanes[g]] * e_neg
            kr_t.append(jnp.concatenate([kk_t, r_t], axis=0))
            lhs_s.append(jnp.concatenate([stack(kk_t), stack(r_t)], axis=0))
            bk_s.append(jnp.concatenate([stack(b_t), stack(k_t)], axis=0))
            v_s.append(stack(v_ref[rows, lanes[g]]))
        st.update(bk_s=bk_s, v_s=v_s, kr_t=kr_t, p_end=p_end)
        st["mn"] = [_sdot(lhs_s[g], bk_s[g], _NT) for g in groups]

    def a2(st):
        mn = st.pop("mn")
        st["m1"] = [mn[g][0:STACK] * mask_strict_b for g in groups]
        m_k = [mn[g][0:STACK] * mask_strict_k for g in groups]
        st["n_bk"] = [mn[g][STACK:] * mask_incl for g in groups]
        st["m2"] = [_sdot(st["m1"][g], twice(st["m1"][g])) for g in groups]
        st["mv"] = [_sdot(m_k[g], twice(st["v_s"][g])) for g in groups]

    def a3(st):
        st["m4"] = [_sdot(st["m2"][g], twice(st["m2"][g])) for g in groups]
        st["t_inv"] = [_sdot(eye - st["m1"][g], twice(eye + st["m2"][g])) for g in groups]

    def a4(st):
        st["m8"] = [_sdot(st["m4"][g], twice(st["m4"][g])) for g in groups]
        st["t_inv"] = [_sdot(st["t_inv"][g], twice(eye + st["m4"][g])) for g in groups]

    def a5(st):
        st["t_inv"] = [_sdot(st["t_inv"][g], twice(eye + st["m8"][g])) for g in groups]

    def b1(c, st, s_prev):
        load_state = is_sample if c > 0 else jnp.logical_or(is_sample, i == 0)
        slot = jnp.where(is_sample, c, 0)
        s0 = []
        for g in groups:
            s_loaded = jnp.concatenate([s_in_ref[slot, :, lanes[g]]] * GROUP_HEADS, axis=0) * block_diag
            s0.append(jnp.where(load_state, s_loaded, s_prev[g]))
        st["s0"] = s0
        st["gr"] = [_sdot(st["kr_t"][g], s0[g], _NT) for g in groups]

    def b2(st):
        u_s = [-_sdot(st["t_inv"][g], twice(stack(st["gr"][g][0:CHUNK]) + st["mv"][g])) for g in groups]
        st["uv"] = [jnp.concatenate([u_s[g], st["v_s"][g]], axis=0) for g in groups]

    def b3(c, st):
        slot = jnp.where(is_sample, c, 0)
        for g in groups:
            y_ref[st["rows"], lanes[g]] = st["gr"][g][CHUNK:] + unstack(_sdot(st["n_bk"][g], st["uv"][g]))
        s_new = []
        for g in groups:
            s_new.append(st["s0"][g] * st["p_end"][g] + _sdot(st["uv"][g], st["bk_s"][g] * st["p_end"][g], _TN))
            s_out_ref[slot, :, lanes[g]] = compact(s_new[g])
        return s_new

    s_cur = [s_scr[g] for g in groups]
    cur = a0(0)
    for lvl in (a1, a2, a3, a4, a5):
        lvl(cur)
    for c in range(n_chunks):
        last = c == n_chunks - 1
        nxt = None if last else a0(c + 1)
        b1(c, cur, s_cur)
        if not last:
            a1(nxt)
        b2(cur)
        if not last:
            a2(nxt)
        s_cur = b3(c, cur)
        if not last:
            for lvl in (a3, a4, a5):
                lvl(nxt)
        cur = nxt
    for g in groups:
        s_scr[g] = s_cur[g]


def _rwkv_scan(r, k, v, kk, b, lw, s_in, n_prompt):
    n = r.shape[0]
    n_prompt_tiles = n_prompt // SCAN_ROWS
    row_spec = pl.BlockSpec((SCAN_ROWS, D), lambda i: (i, 0))
    state_spec = pl.BlockSpec((STATE_SLOTS, HEAD, D),
                              lambda i: (jnp.maximum(i - n_prompt_tiles + 1, 0), 0, 0))
    return pl.pallas_call(
        functools.partial(_scan_kernel, n_prompt_tiles),
        grid=(n // SCAN_ROWS,),
        in_specs=[row_spec] * 6 + [state_spec],
        out_specs=[row_spec, state_spec],
        out_shape=[jax.ShapeDtypeStruct((n, D), F32), jax.ShapeDtypeStruct(s_in.shape, F32)],
        scratch_shapes=[pltpu.VMEM((N_LANE_GROUPS, GROUP_LANES, GROUP_LANES), F32)],
        compiler_params=_cparams(("arbitrary",)),
        name="rwkv_scan",
    )(r, k, v, kk, b, lw, s_in)


def _conv_kernel(n_prompt_tiles, cb_ref, cc_ref, ch_ref, ccc_ref, cch_ref, bnd1_ref, bnd2_ref,
                 cw_ref, wout_ref, o_ref, u_ref):
    i = pl.program_id(0)
    rows = cb_ref.shape[0]
    row = lax.broadcasted_iota(jnp.int32, (rows, 1), 0)
    is_sample = i >= n_prompt_tiles
    pos = row % SEQ_S
    u = cc_ref[...] * ch_ref[...]
    u_ref[...] = u
    u_prev = jnp.where(i == 0, 0.0, ccc_ref[...] * cch_ref[...])
    prev1 = pltpu.roll(u, 1, 0)
    prev1 = jnp.where(row == 0, u_prev[7:8, :], prev1)
    prev2 = pltpu.roll(u, 2, 0)
    prev2 = jnp.where(row == 0, u_prev[6:7, :], prev2)
    prev2 = jnp.where(row == 1, u_prev[7:8, :], prev2)
    bnd1 = bnd1_ref[...]
    prev1 = jnp.where(jnp.logical_and(is_sample, pos == 0), bnd1, prev1)
    prev2 = jnp.where(jnp.logical_and(is_sample, pos == 0), bnd2_ref[...], prev2)
    prev2 = jnp.where(jnp.logical_and(is_sample, pos == 1), bnd1, prev2)
    cw = cw_ref[...]
    conv = prev2 * cw[0:1, :] + prev1 * cw[1:2, :] + u * cw[2:3, :]
    o_ref[...] = _dot(cb_ref[...] * conv, wout_ref[...])


def _short_conv(p, bnd1, bnd2, conv_w, w_out, n_prompt):
    n = p.shape[0]
    n_prompt_tiles = n_prompt // TR
    cblk = COL_CONV // CONV_DIM
    carry_blk = TR // 8

    def row_spec(cb):
        return pl.BlockSpec((TR, CONV_DIM), lambda i: (i, cb))

    def carry_spec(cb):
        return pl.BlockSpec((8, CONV_DIM), lambda i: (jnp.maximum(i * carry_blk - 1, 0), cb))

    bnd_spec = pl.BlockSpec((TR, CONV_DIM), lambda i: (jnp.maximum(i - n_prompt_tiles, 0), 0))
    return pl.pallas_call(
        functools.partial(_conv_kernel, n_prompt_tiles),
        grid=(n // TR,),
        in_specs=[row_spec(cblk), row_spec(cblk + 1), row_spec(cblk + 2),
                  carry_spec(cblk + 1), carry_spec(cblk + 2), bnd_spec, bnd_spec,
                  pl.BlockSpec((8, CONV_DIM), lambda i: (0, 0)),
                  pl.BlockSpec((CONV_DIM, D), lambda i: (0, 0))],
        out_specs=[pl.BlockSpec((TR, D), lambda i: (i, 0)), pl.BlockSpec((TR, CONV_DIM), lambda i: (i, 0))],
        out_shape=[jax.ShapeDtypeStruct((n, D), F32), jax.ShapeDtypeStruct((n, CONV_DIM), F32)],
        compiler_params=_cparams(("arbitrary",)),
        name="short_conv",
    )(p, p, p, p, p, bnd1, bnd2, conv_w, w_out)


def _mem_kernel(q_ref, k_ref, v_ref, wo_ref, o_ref):
    n_seq = k_ref.shape[0]
    rows = q_ref.shape[0] // n_seq
    per_seq = []
    for s_i in range(n_seq):
        q = q_ref[s_i * rows:(s_i + 1) * rows, :]
        k = k_ref[s_i]
        v = v_ref[s_i]
        outs = []
        for h in range(MEM_HEADS):
            sl = slice(h * MEM_HEAD_DIM, (h + 1) * MEM_HEAD_DIM)
            s = _dot(q[:, sl], k[:, sl], _NT) * (MEM_HEAD_DIM ** -0.5)
            s = s - jnp.max(s, axis=-1, keepdims=True)
            e = jnp.exp(s)
            pr = e / jnp.sum(e, axis=-1, keepdims=True)
            outs.append(_dot(pr, v[:, sl]))
        per_seq.append(jnp.concatenate(outs, axis=1))
    o_ref[...] = _dot(jnp.concatenate(per_seq, axis=0), wo_ref[...])


MEM_SEQS = 4


def _mem_sample(p, mem_k, mem_v, w_o, row_start, n_seq):
    qblk = COL_Q // MEM_DIM
    rows = MEM_SEQS * SEQ_S
    rb0 = row_start // rows
    return pl.pallas_call(
        _mem_kernel,
        grid=(n_seq // MEM_SEQS,),
        in_specs=[pl.BlockSpec((rows, MEM_DIM), lambda i: (rb0 + i, qblk)),
                  pl.BlockSpec((MEM_SEQS, N_MEM, MEM_DIM), lambda i: (i, 0, 0)),
                  pl.BlockSpec((MEM_SEQS, N_MEM, MEM_DIM), lambda i: (i, 0, 0)),
                  pl.BlockSpec((MEM_DIM, D), lambda i: (0, 0))],
        out_specs=pl.BlockSpec((rows, D), lambda i: (i, 0)),
        out_shape=jax.ShapeDtypeStruct((n_seq * SEQ_S, D), F32),
        compiler_params=_cparams(("arbitrary",)),
        name="mem_attention_sample",
    )(p, mem_k, mem_v, w_o)


MEM_TILE = 256


def _mem_prompt_kernel(n_prompt_tiles, q_ref, k_ref, v_ref, wo_ref, tail_ref, o_ref):
    i = pl.program_id(0)

    @pl.when(i < n_prompt_tiles)
    def _():
        _mem_kernel(q_ref, k_ref, v_ref, wo_ref, o_ref)

    @pl.when(i >= n_prompt_tiles)
    def _():
        o_ref[...] = tail_ref[...]


def _mem_attention(p, mem_k, mem_v, w_o, o_sample, n_prompt):
    n = p.shape[0]
    qblk = COL_Q // MEM_DIM
    n_prompt_tiles = n_prompt // MEM_TILE
    return pl.pallas_call(
        functools.partial(_mem_prompt_kernel, n_prompt_tiles),
        grid=(n // MEM_TILE,),
        in_specs=[pl.BlockSpec((MEM_TILE, MEM_DIM), lambda i: (jnp.minimum(i, n_prompt_tiles - 1), qblk)),
                  pl.BlockSpec((1, N_MEM, MEM_DIM), lambda i: (0, 0, 0)),
                  pl.BlockSpec((1, N_MEM, MEM_DIM), lambda i: (0, 0, 0)),
                  pl.BlockSpec((MEM_DIM, D), lambda i: (0, 0)),
                  pl.BlockSpec((MEM_TILE, D), lambda i: (jnp.maximum(i - n_prompt_tiles, 0), 0))],
        out_specs=pl.BlockSpec((MEM_TILE, D), lambda i: (i, 0)),
        out_shape=jax.ShapeDtypeStruct((n, D), F32),
        compiler_params=_cparams(("arbitrary",)),
        name="mem_attention",
    )(p, mem_k, mem_v, w_o, o_sample)


def _layer_norm(z, g, b):
    mu = jnp.mean(z, axis=-1, keepdims=True)
    d = z - mu
    var = jnp.mean(d * d, axis=-1, keepdims=True)
    return d * lax.rsqrt(var + LN_EPS) * g + b


def _merge_kernel(n_prompt_tiles, xp_ref, xs_ref, ga_ref, gb_ref, gm_ref, y_ref, bonus_ref, g_ref, ocv_ref, omem_ref,
                  gng_ref, gnb_ref, bd_ref, wo_ref, l1g_ref, l1b_ref, wr_ref, h_o, lt_o):
    x = jnp.where(pl.program_id(0) < n_prompt_tiles, xp_ref[...], xs_ref[...])
    bd = bd_ref[...]
    y = y_ref[...]
    mean = _head_sum(y, bd) * (1.0 / HEAD)
    d = y - mean
    var = _head_sum(d * d, bd) * (1.0 / HEAD)
    yn = d * lax.rsqrt(var + GN_EPS) * gng_ref[...] + gnb_ref[...]
    o_rw = (yn + bonus_ref[...]) * g_ref[...]
    merged = (_sigmoid(ga_ref[...]) * o_rw + _sigmoid(gb_ref[...]) * ocv_ref[...]
              + _sigmoid(gm_ref[...]) * omem_ref[...])
    z = ALPHA * x + _dot(merged, wo_ref[...])
    h = _layer_norm(z, l1g_ref[...], l1b_ref[...])
    h_o[...] = h
    lt_o[...] = _dot3(wr_ref[...], h, _NT)


def _merge_ln1(xp, xs, p, y_raw, bonus, g, o_cv, o_mem, gn_g, gn_b, bd, w_o, ln_g, ln_b, w_router_t):
    n = p.shape[0]
    n_prompt_tiles = xp.shape[0] // TR
    gblk = COL_GATE // D
    row = pl.BlockSpec((TR, D), lambda i: (i, 0))
    xp_spec = pl.BlockSpec((TR, D), lambda i: (jnp.minimum(i, n_prompt_tiles - 1), 0))
    xs_spec = pl.BlockSpec((TR, D), lambda i: (jnp.maximum(i - n_prompt_tiles, 0), 0))

    def gate_spec(j):
        return pl.BlockSpec((TR, D), lambda i: (i, gblk + j))

    def const_spec(shape):
        return pl.BlockSpec(shape, lambda i: (0,) * len(shape))

    vec = const_spec((1, D))
    return pl.pallas_call(
        functools.partial(_merge_kernel, n_prompt_tiles),
        grid=(n // TR,),
        in_specs=[xp_spec, xs_spec, gate_spec(0), gate_spec(1), gate_spec(2), row, row, row, row, row,
                  vec, vec, const_spec((GROUP_LANES, GROUP_LANES)), const_spec((D, D)), vec, vec,
                  const_spec((N_EXPERTS, D))],
        out_specs=[row, pl.BlockSpec((N_EXPERTS, TR), lambda i: (0, i))],
        out_shape=[jax.ShapeDtypeStruct((n, D), F32), jax.ShapeDtypeStruct((N_EXPERTS, n), F32)],
        compiler_params=_cparams(("arbitrary",)),
        name="merge_ln1",
    )(xp, xs, p, p, p, y_raw, bonus, g, o_cv, o_mem, gn_g, gn_b, bd, w_o, ln_g, ln_b, w_router_t)


ROUTE_TILE = 256


def _routing_kernel(lt_ref, bias_ref, tri_ref, idx_o, w_o, pos_o, cnt_o, carry):
    i = pl.program_id(0)
    tile = lt_ref.shape[1]

    @pl.when(i == 0)
    def _():
        carry[...] = jnp.zeros_like(carry)

    neg_inf = -jnp.inf
    scores = _sigmoid(lt_ref[...])
    choice = scores + bias_ref[...]
    row = lax.broadcasted_iota(jnp.int32, (N_EXPERTS, tile), 0)
    rowf = row.astype(F32)
    grpf = (row // GROUP_SIZE).astype(F32)

    def group_allreduce(x, op):
        for s in (1, 2, 4):
            up = pltpu.roll(x, N_EXPERTS - s, 0)
            dn = pltpu.roll(x, s, 0)
            x = op(x, jnp.where((row & s) == 0, up, dn))
        return x

    m1 = group_allreduce(choice, jnp.maximum)
    first = group_allreduce(jnp.where(choice == m1, rowf, float(N_EXPERTS)), jnp.minimum)
    m2 = group_allreduce(jnp.where(rowf == first, neg_inf, choice), jnp.maximum)
    gscore = m1 + m2

    gsel = jnp.zeros_like(choice)
    for _ in range(TOPK_GROUPS):
        gmax = jnp.max(gscore, axis=0, keepdims=True)
        pick = jnp.min(jnp.where(gscore == gmax, grpf, float(N_GROUPS)), axis=0, keepdims=True)
        hit = grpf == pick
        gsel = jnp.where(hit, 1.0, gsel)
        gscore = jnp.where(hit, neg_inf, gscore)

    masked = jnp.where(gsel > 0.0, choice, neg_inf)
    row8 = lax.broadcasted_iota(jnp.int32, (TOP_K, tile), 0)
    idx_acc = jnp.zeros((TOP_K, tile), F32)
    w_acc = jnp.zeros((TOP_K, tile), F32)
    sel_all = jnp.zeros_like(choice)
    for kk in range(TOP_K):
        mx = jnp.max(masked, axis=0, keepdims=True)
        pick = jnp.min(jnp.where(masked == mx, rowf, float(N_EXPERTS)), axis=0, keepdims=True)
        hit = rowf == pick
        wk = jnp.sum(jnp.where(hit, scores, 0.0), axis=0, keepdims=True)
        idx_acc = jnp.where(row8 == kk, pick, idx_acc)
        w_acc = jnp.where(row8 == kk, wk, w_acc)
        sel_all = jnp.where(hit, 1.0, sel_all)
        masked = jnp.where(hit, neg_inf, masked)

    w_sum = jnp.sum(w_acc, axis=0, keepdims=True)
    w_o[...] = w_acc / w_sum * ROUTED_SCALE
    idx_o[...] = idx_acc.astype(jnp.int32)

    prefix = lax.dot_general(sel_all.astype(BF16), tri_ref[...], _NN, preferred_element_type=F32) + carry[...]
    pos_acc = jnp.zeros((TOP_K, tile), F32)
    for kk in range(TOP_K):
        hit = rowf == idx_acc[kk:kk + 1, :]
        pk = jnp.sum(jnp.where(hit, prefix, 0.0), axis=0, keepdims=True)
        pos_acc = jnp.where(row8 == kk, pk, pos_acc)
    pos_o[...] = pos_acc.astype(jnp.int32)
    carry[...] = carry[...] + jnp.sum(sel_all, axis=1, keepdims=True)
    cnt_o[...] = carry[...]


def _routing(logits_t, bias_col, tri):
    n = logits_t.shape[1]
    tile = ROUTE_TILE
    tok = pl.BlockSpec((TOP_K, tile), lambda i: (0, i))
    return pl.pallas_call(
        _routing_kernel,
        grid=(n // tile,),
        in_specs=[pl.BlockSpec((N_EXPERTS, tile), lambda i: (0, i)),
                  pl.BlockSpec((N_EXPERTS, 1), lambda i: (0, 0)),
                  pl.BlockSpec((tile, tile), lambda i: (0, 0))],
        out_specs=[tok, tok, tok, pl.BlockSpec((N_EXPERTS, 1), lambda i: (0, 0))],
        out_shape=[jax.ShapeDtypeStruct((TOP_K, n), jnp.int32), jax.ShapeDtypeStruct((TOP_K, n), F32),
                   jax.ShapeDtypeStruct((TOP_K, n), jnp.int32), jax.ShapeDtypeStruct((N_EXPERTS, 1), F32)],
        scratch_shapes=[pltpu.VMEM((N_EXPERTS, 1), F32)],
        compiler_params=_cparams(("arbitrary",)),
        name="routing",
    )(logits_t, bias_col, tri)


ZERO_ROWS = 128


def _dispatch_kernel(zs_ref, zc_ref, dest_ref, h_ref, xb_out, dest_smem, zbuf, sem, idx_sem, zsem):
    i = pl.program_id(0)
    rows = h_ref.shape[0]

    @pl.when(i == 0)
    def _():
        zbuf[...] = jnp.zeros_like(zbuf)

        def zero_copy(piece):
            dst0 = pl.multiple_of(piece * ZERO_ROWS, ZERO_ROWS)
            return pltpu.make_async_copy(zbuf, xb_out.at[pl.ds(dst0, ZERO_ROWS), :], zsem)

        def per_range(e, c):
            def issue_piece(j, c2):
                zero_copy(zs_ref[e] + j).start()
                return c2
            lax.fori_loop(0, zc_ref[e], issue_piece, 0)
            return c

        def per_range_wait(e, c):
            def wait_piece(j, c2):
                zero_copy(zs_ref[e] + j).wait()
                return c2
            lax.fori_loop(0, zc_ref[e], wait_piece, 0)
            return c

        lax.fori_loop(0, N_EXPERTS + 1, per_range, 0)
        lax.fori_loop(0, N_EXPERTS + 1, per_range_wait, 0)

    cp = pltpu.make_async_copy(dest_ref, dest_smem, idx_sem)
    cp.start()
    cp.wait()

    def row_copy(t, k):
        return pltpu.make_async_copy(h_ref.at[pl.ds(t, 1), :],
                                     xb_out.at[pl.ds(dest_smem[k, t], 1), :], sem)

    def issue(t, c):
        for k in range(TOP_K):
            row_copy(t, k).start()
        return c

    def drain(t, c):
        for k in range(TOP_K):
            row_copy(t, k).wait()
        return c

    lax.fori_loop(0, rows, issue, 0)
    lax.fori_loop(0, rows, drain, 0)


def _dispatch(zero_start, zero_count, dest_t, h, n_rows):
    n = h.shape[0]
    grid_spec = pltpu.PrefetchScalarGridSpec(
        num_scalar_prefetch=2,
        grid=(n // TR,),
        in_specs=[pl.BlockSpec((TOP_K, TR), lambda i, zs, zc: (0, i)),
                  pl.BlockSpec((TR, D), lambda i, zs, zc: (i, 0))],
        out_specs=pl.BlockSpec(memory_space=pl.ANY),
        scratch_shapes=[pltpu.SMEM((TOP_K, TR), jnp.int32), pltpu.VMEM((ZERO_ROWS, D), F32),
                        pltpu.SemaphoreType.DMA, pltpu.SemaphoreType.DMA, pltpu.SemaphoreType.DMA],
    )
    return pl.pallas_call(
        _dispatch_kernel,
        grid_spec=grid_spec,
        out_shape=jax.ShapeDtypeStruct((n_rows, D), F32),
        compiler_params=_cparams(("arbitrary",)),
        name="moe_dispatch",
    )(zero_start, zero_count, dest_t, h)


def _silu(x):
    return x * _sigmoid(x)


def _expert_kernel(be_ref, nu_ref, first_ref, slot_ref, next_ref, x_ref, wu_hbm, wd_hbm, o_ref,
                   wu_f32, wd_f32, wu_bf, wd_bf, sem_u, sem_d):
    b = pl.program_id(0)

    def weight_copies(e, slot):
        return (pltpu.make_async_copy(wu_hbm.at[e], wu_f32.at[slot], sem_u.at[slot]),
                pltpu.make_async_copy(wd_hbm.at[e], wd_f32.at[slot], sem_d.at[slot]))

    @pl.when(b == 0)
    def _():
        for cp in weight_copies(be_ref[0], 0):
            cp.start()

    @pl.when(first_ref[b] == 1)
    def _():
        slot = slot_ref[b]
        for cp in weight_copies(be_ref[b], slot):
            cp.wait()
        wu_bf[...] = wu_f32[slot].astype(BF16)
        wd_bf[...] = wd_f32[slot].astype(BF16)

        @pl.when(next_ref[b] >= 0)
        def _():
            for cp in weight_copies(next_ref[b], 1 - slot):
                cp.start()

    @pl.when(b < nu_ref[0])
    def _():
        up = _dot(x_ref[...], wu_bf[...])
        act = _silu(up[:, :EXPERT_FF]) * up[:, EXPERT_FF:]
        o_ref[...] = _dot(act, wd_bf[...])

    @pl.when(b >= nu_ref[0])
    def _():
        o_ref[...] = jnp.zeros_like(o_ref)


def _experts(block_e, n_used, xb, w_up, w_down):
    rows = xb.shape[0]
    nb = rows // EXPERT_BM

    bidx = jnp.arange(nb, dtype=jnp.int32)
    prev_e = jnp.concatenate([jnp.full((1,), -1, jnp.int32), block_e[:-1]])
    first = jnp.logical_and(bidx < n_used[0], block_e != prev_e)
    slot = (jnp.cumsum(first.astype(jnp.int32)) - 1) % 2
    first_pos = jnp.where(first, bidx, nb)
    next_first = jnp.concatenate([jnp.flip(lax.cummin(jnp.flip(first_pos)))[1:], jnp.full((1,), nb, jnp.int32)])
    next_e = jnp.where(next_first < nb, block_e[jnp.minimum(next_first, nb - 1)], -1)

    def xmap(b, be, nu, fi, sl, ne):
        return (jnp.minimum(b, nu[0] - 1), 0)

    grid_spec = pltpu.PrefetchScalarGridSpec(
        num_scalar_prefetch=5,
        grid=(nb,),
        in_specs=[pl.BlockSpec((EXPERT_BM, D), xmap),
                  pl.BlockSpec(memory_space=pl.ANY),
                  pl.BlockSpec(memory_space=pl.ANY)],
        out_specs=pl.BlockSpec((EXPERT_BM, D), lambda b, be, nu, fi, sl, ne: (b, 0)),
        scratch_shapes=[pltpu.VMEM((2, D, 2 * EXPERT_FF), F32), pltpu.VMEM((2, EXPERT_FF, D), F32),
                        pltpu.VMEM((D, 2 * EXPERT_FF), BF16), pltpu.VMEM((EXPERT_FF, D), BF16),
                        pltpu.SemaphoreType.DMA((2,)), pltpu.SemaphoreType.DMA((2,))],
    )
    return pl.pallas_call(
        _expert_kernel,
        grid_spec=grid_spec,
        out_shape=jax.ShapeDtypeStruct((rows, D), F32),
        compiler_params=_cparams(("arbitrary",)),
        name="moe_experts",
    )(block_e, n_used, first.astype(jnp.int32), slot.astype(jnp.int32), next_e.astype(jnp.int32), xb, w_up, w_down)


SHARED_TILE = 512


def _shared_kernel(h_ref, wu_ref, wd_ref, o_ref):
    up = _dot(h_ref[...], wu_ref[...])
    act = _silu(up[:, :SHARED_FF]) * up[:, SHARED_FF:]
    o_ref[...] = _dot(act, wd_ref[...])


def _shared_ffn(h, w_up, w_down):
    n = h.shape[0]
    row = pl.BlockSpec((SHARED_TILE, D), lambda i: (i, 0))
    return pl.pallas_call(
        _shared_kernel,
        grid=(n // SHARED_TILE,),
        in_specs=[row, pl.BlockSpec((D, 2 * SHARED_FF), lambda i: (0, 0)),
                  pl.BlockSpec((SHARED_FF, D), lambda i: (0, 0))],
        out_specs=row,
        out_shape=jax.ShapeDtypeStruct((n, D), F32),
        compiler_params=_cparams(("parallel",)),
        name="shared_ffn",
    )(h, w_up, w_down)


def _combine_kernel(n_prompt_tiles, dest_ref, w_ref, h_ref, sh_ref, yb_ref, l2g_ref, l2b_ref, yp_o, ys_o,
                    buf, dest_smem, sem, idx_sem):
    i = pl.program_id(0)
    rows = h_ref.shape[0]
    cp = pltpu.make_async_copy(dest_ref, dest_smem, idx_sem)
    cp.start()
    cp.wait()

    def row_copy(t, k):
        return pltpu.make_async_copy(yb_ref.at[pl.ds(dest_smem[k, t], 1), :],
                                     buf.at[k, pl.ds(t, 1), :], sem)

    def issue(t, c):
        for k in range(TOP_K):
            row_copy(t, k).start()
        return c

    def drain(t, c):
        for k in range(TOP_K):
            row_copy(t, k).wait()
        return c

    lax.fori_loop(0, rows, issue, 0)
    lax.fori_loop(0, rows, drain, 0)
    w = w_ref[...]
    f = sh_ref[...]
    for k in range(TOP_K):
        f = f + w[:, k:k + 1] * buf[k]
    z = ALPHA * h_ref[...] + f
    y = _layer_norm(z, l2g_ref[...], l2b_ref[...])

    @pl.when(i < n_prompt_tiles)
    def _():
        yp_o[...] = y

    @pl.when(i >= n_prompt_tiles)
    def _():
        ys_o[...] = y


def _combine_ln2(dest_t, w_tok, h, shared, yb, ln_g, ln_b, n_prompt):
    n = h.shape[0]
    n_prompt_tiles = n_prompt // TR
    row = pl.BlockSpec((TR, D), lambda i: (i, 0))
    vec = pl.BlockSpec((1, D), lambda i: (0, 0))
    return pl.pallas_call(
        functools.partial(_combine_kernel, n_prompt_tiles),
        grid=(n // TR,),
        in_specs=[pl.BlockSpec((TOP_K, TR), lambda i: (0, i)),
                  pl.BlockSpec((TR, TOP_K), lambda i: (i, 0)),
                  row, row, pl.BlockSpec(memory_space=pl.ANY), vec, vec],
        out_specs=[pl.BlockSpec((TR, D), lambda i: (jnp.minimum(i, n_prompt_tiles - 1), 0)),
                   pl.BlockSpec((TR, D), lambda i: (jnp.maximum(i - n_prompt_tiles, 0), 0))],
        out_shape=[jax.ShapeDtypeStruct((n_prompt, D), F32), jax.ShapeDtypeStruct((n - n_prompt, D), F32)],
        scratch_shapes=[pltpu.VMEM((TOP_K, TR, D), F32), pltpu.SMEM((TOP_K, TR), jnp.int32),
                        pltpu.SemaphoreType.DMA, pltpu.SemaphoreType.DMA],
        compiler_params=_cparams(("arbitrary",)),
        name="moe_combine_ln2",
    )(dest_t, w_tok, h, shared, yb, ln_g, ln_b)


def _rw_cols_split(v):
    pad = jnp.zeros(v.shape[:-1] + (LORA_PAD - LORA_W,), v.dtype)
    lora = jnp.concatenate([v[..., 3 * D:3 * D + LORA_W], pad,
                            v[..., 3 * D + LORA_W:3 * D + LORA_W + LORA_A], pad,
                            v[..., 3 * D + LORA_W + LORA_A:]], axis=-1)
    return v[..., 0:3 * D], lora


def _pad_rows(w, rows):
    return jnp.concatenate([w, jnp.zeros((rows - w.shape[0],) + w.shape[1:], w.dtype)], axis=0)


def kernel(x_prompt, x_sample, mem_prompt, state_rwkv, state_shift, state_conv, cache_mem_k, cache_mem_v,
           w_in, mu_shift, rw_w0, rw_w2, rw_a0, rw_a2, rw_g2, rw_k_k, rw_k_a, rw_r_k, rw_gn_g, rw_gn_b,
           conv_w, w_conv_out, w_mem_k, w_mem_v, w_mem_o, w_o, ln1_g, ln1_b, w_router, router_bias,
           w_exp_up, w_exp_down, w_sh_up, w_sh_down, ln2_g, ln2_b):
    n_prompt = x_prompt.shape[0] * x_prompt.shape[1]
    n_seq_s, seq_s = x_sample.shape[0], x_sample.shape[1]
    n_sample = n_seq_s * seq_s
    n = n_prompt + n_sample
    assert x_prompt.shape[0] == 1 and seq_s == SEQ_S and n_prompt % TR == 0 and n_sample % TR == 0
    assert n % SHARED_TILE == 0 and n % ROUTE_TILE == 0 and w_in.shape[0] == 1
    assert n_prompt % PROJ_TM == 0 and n_sample % PROJ_TM == 0

    xp = x_prompt.reshape(n_prompt, D)
    xs = x_sample.reshape(n_sample, D)

    def vec(v):
        return v.reshape(1, -1).astype(F32)

    w_in_r = _wt_relayout(w_in[0].T)
    p = _in_proj(xp, xs, w_in_r)

    w_kv = jnp.concatenate([w_mem_k[0], w_mem_v[0]], axis=1).astype(BF16)
    kv = _matmul(mem_prompt[0], w_kv, N_MEM, 512, "mem_kv")
    mem_k_p, mem_v_p = kv[:, :MEM_DIM], kv[:, MEM_DIM:]

    mu_rkv, mu_lora = _rw_cols_split(vec(mu_shift[0]))
    sh_rkv, sh_lora = _rw_cols_split(state_shift[0, :, 0, :])
    bnd_rkv = jnp.repeat(sh_rkv, seq_s, axis=0)
    bnd_lora = jnp.repeat(sh_lora, seq_s, axis=0)
    hi = lax.broadcasted_iota(jnp.int32, (GROUP_LANES, GROUP_LANES), 0) // HEAD
    hj = lax.broadcasted_iota(jnp.int32, (GROUP_LANES, GROUP_LANES), 1) // HEAD
    bd = (hi == hj).astype(BF16)
    r, k, v, kk, b, lw, g, bonus = _rwkv_prep(
        p, bnd_rkv, bnd_lora, mu_rkv, mu_lora, vec(rw_w0[0]), vec(rw_a0[0]), vec(rw_k_k[0]), vec(rw_k_a[0]),
        vec(rw_r_k[0]), _pad_rows(rw_w2[0], LORA_PAD).astype(BF16), _pad_rows(rw_a2[0], LORA_PAD).astype(BF16),
        rw_g2[0].astype(BF16), bd, n_prompt)

    s_sample = jnp.transpose(state_rwkv[0], (0, 2, 1, 3)).reshape(n_seq_s, HEAD, D)
    s_in = jnp.concatenate([jnp.zeros((STATE_SLOTS, HEAD, D), F32), s_sample.astype(F32)], axis=0)
    y_raw, s_out = _rwkv_scan(r, k, v, kk, b, lw, s_in, n_prompt)

    bnd1 = jnp.repeat(state_conv[0, :, 1, :], seq_s, axis=0)
    bnd2 = jnp.repeat(state_conv[0, :, 0, :], seq_s, axis=0)
    o_cv, u = _short_conv(p, bnd1, bnd2, _pad_rows(conv_w[0], 8), w_conv_out[0].astype(BF16), n_prompt)

    w_mem_o_b = w_mem_o[0].astype(BF16)
    o_mem_s = _mem_sample(p, cache_mem_k[0].reshape(n_seq_s, N_MEM, MEM_DIM),
                          cache_mem_v[0].reshape(n_seq_s, N_MEM, MEM_DIM), w_mem_o_b, n_prompt, n_seq_s)
    o_mem = _mem_attention(p, mem_k_p[None], mem_v_p[None], w_mem_o_b, o_mem_s, n_prompt)

    h, logits_t = _merge_ln1(xp, xs, p, y_raw, bonus, g, o_cv, o_mem, vec(rw_gn_g[0]), vec(rw_gn_b[0]), bd,
                             w_o[0].astype(BF16), vec(ln1_g[0]), vec(ln1_b[0]), w_router[0].T)

    ti = lax.broadcasted_iota(jnp.int32, (ROUTE_TILE, ROUTE_TILE), 0)
    tj = lax.broadcasted_iota(jnp.int32, (ROUTE_TILE, ROUTE_TILE), 1)
    tri = (ti < tj).astype(BF16)
    idx_t, w_t, pos_t, counts = _routing(logits_t, router_bias[0].reshape(N_EXPERTS, 1).astype(F32), tri)

    counts = counts[:, 0].astype(jnp.int32)
    padded = (counts + EXPERT_BM - 1) // EXPERT_BM * EXPERT_BM
    seg_end = jnp.cumsum(padded)
    seg_start = seg_end - padded
    expert_ids = jnp.arange(N_EXPERTS, dtype=jnp.int32)
    dest_t = pos_t + jnp.sum(
        jnp.where(idx_t[None] == expert_ids[:, None, None], seg_start[:, None, None], 0), axis=0)
    nb = (n * TOP_K) // EXPERT_BM + N_EXPERTS
    block_rows = jnp.arange(nb, dtype=jnp.int32) * EXPERT_BM
    block_e = jnp.minimum(jnp.sum((seg_end[None, :] <= block_rows[:, None]).astype(jnp.int32), axis=1),
                          N_EXPERTS - 1)
    n_used = (seg_end[-1:] // EXPERT_BM).astype(jnp.int32)

    pieces_per_block = EXPERT_BM // ZERO_ROWS
    valid_last = counts - (padded - EXPERT_BM)
    first_piece = valid_last // ZERO_ROWS
    zero_start = jnp.where(padded > 0, (seg_end - EXPERT_BM) // ZERO_ROWS + first_piece, 0)
    zero_count = jnp.where(padded > 0, pieces_per_block - first_piece, 0)
    total_pieces = nb * pieces_per_block
    zero_start = jnp.concatenate([zero_start, seg_end[-1:] // ZERO_ROWS]).astype(jnp.int32)
    zero_count = jnp.concatenate([zero_count, total_pieces - seg_end[-1:] // ZERO_ROWS]).astype(jnp.int32)

    xb = _dispatch(zero_start, zero_count, dest_t, h, nb * EXPERT_BM)
    yb = _experts(block_e, n_used, xb, w_exp_up[0], w_exp_down[0])
    shared = _shared_ffn(h, w_sh_up[0].astype(BF16), w_sh_down[0].astype(BF16))
    y_p, y_s = _combine_ln2(dest_t, w_t.T, h, shared, yb, vec(ln2_g[0]), vec(ln2_b[0]), n_prompt)

    dt = x_prompt.dtype
    y_p = y_p.reshape(x_prompt.shape)
    y_s = y_s.reshape(x_sample.shape)

    def state_out(s):
        q = s.reshape(s.shape[0], HEAD, N_HEADS, HEAD)
        return jnp.transpose(q, (0, 2, 1, 3))[None].astype(dt)

    rw_p = state_out(s_out[0:1])
    rw_s = state_out(s_out[STATE_SLOTS:])

    last_rows = jnp.concatenate([jnp.array([n_prompt - 1], jnp.int32),
                                 n_prompt + seq_s - 1 + seq_s * jnp.arange(n_seq_s, dtype=jnp.int32)])
    p_last = p[last_rows]
    shift = jnp.concatenate([p_last[:, 0:3 * D],
                             p_last[:, COL_LORA:COL_LORA + LORA_W],
                             p_last[:, COL_LORA + LORA_PAD:COL_LORA + LORA_PAD + LORA_A],
                             p_last[:, COL_LORA + 2 * LORA_PAD:]], axis=1)
    sh_p = shift[0:1].reshape(1, 1, 1, RW_COLS)
    sh_s = shift[1:].reshape(1, n_seq_s, 1, RW_COLS)

    cv_p = u[n_prompt - 2:n_prompt].reshape(1, 1, 2, CONV_DIM)
    cv_s = u[n_prompt:].reshape(n_seq_s, seq_s, CONV_DIM)[:, seq_s - 2:, :][None]

    mk_p = mem_k_p.reshape(1, 1, N_MEM, MEM_HEADS, MEM_HEAD_DIM)
    mv_p = mem_v_p.reshape(1, 1, N_MEM, MEM_HEADS, MEM_HEAD_DIM)
    return (y_p, y_s, rw_p, sh_p, cv_p, mk_p, mv_p, rw_s, sh_s, cv_s)
```

```python
import functools

import jax
import jax.numpy as jnp
from jax import lax
from jax.experimental import pallas as pl
from jax.experimental.pallas import tpu as pltpu

F32 = jnp.float32
BF16 = jnp.bfloat16

D = 2048
HEAD = 64
N_HEADS = D // HEAD
LORA_W = 96
LORA_A = 96
LORA_G = 256
DECAY_SCALE = 0.6065306597126334
GN_EPS = HEAD * 1e-5
CONV_DIM = D // 2
N_MEM = 256
MEM_HEADS = 4
MEM_HEAD_DIM = 256
MEM_DIM = MEM_HEADS * MEM_HEAD_DIM
N_EXPERTS = 64
N_GROUPS = 8
GROUP_SIZE = N_EXPERTS // N_GROUPS
TOPK_GROUPS = 4
TOP_K = 8
EXPERT_FF = 512
SHARED_FF = 512
ROUTED_SCALE = 2.5
LN_EPS = 1e-5
DEPTH = 1
ALPHA = (2 * DEPTH) ** 0.25
RW_COLS = 3 * D + LORA_W + LORA_A + LORA_G

LORA_PAD = 128
LORA_COLS = 2 * LORA_PAD + LORA_G
COL_RKV = 0
COL_GATE = 3 * D
COL_CONV = 6 * D
COL_Q = COL_CONV + 3 * CONV_DIM
COL_LORA = COL_Q + MEM_DIM
P_COLS = COL_LORA + LORA_COLS

CHUNK = 16
GROUP_HEADS = 4
GROUP_LANES = GROUP_HEADS * HEAD
N_LANE_GROUPS = D // GROUP_LANES
STACK = GROUP_HEADS * CHUNK
SEQ_S = 16
STATE_SLOTS = 8

TR = 128
SCAN_ROWS = STATE_SLOTS * CHUNK
EXPERT_BM = 256
VMEM_LIMIT = 56 * 1024 * 1024


def _cparams(sem):
    return pltpu.CompilerParams(dimension_semantics=sem, vmem_limit_bytes=VMEM_LIMIT)


def _sigmoid(x):
    return 1.0 / (1.0 + jnp.exp(-x))


def _dot(a, b, dims=(((1,), (0,)), ((), ()))):
    return lax.dot_general(a.astype(BF16), b.astype(BF16), dims, preferred_element_type=F32)


_NN = (((1,), (0,)), ((), ()))
_NT = (((1,), (1,)), ((), ()))
_TN = (((0,), (0,)), ((), ()))


def _split2(x):
    hi = x.astype(BF16)
    lo = (x - hi.astype(F32)).astype(BF16)
    return hi, lo


def _split3(x):
    hi = x.astype(BF16)
    r1 = x - hi.astype(F32)
    mid = r1.astype(BF16)
    lo = (r1 - mid.astype(F32)).astype(BF16)
    return hi, mid, lo


def _dot3(a, b, dims=_NN):
    ah, al = _split2(a)
    bh, bl = _split2(b)
    f = functools.partial(lax.dot_general, dimension_numbers=dims, preferred_element_type=F32)
    return f(ah, bh) + (f(ah, bl) + f(al, bh))


def _dot_exact_rhs(a, b_bf16, dims=_NN):
    hi, mid, lo = _split3(a)
    f = functools.partial(lax.dot_general, dimension_numbers=dims, preferred_element_type=F32)
    return f(hi, b_bf16) + (f(mid, b_bf16) + f(lo, b_bf16))


def _dot_exact_lhs(a_bf16, b):
    hi, mid, lo = _split3(b)
    f = functools.partial(lax.dot_general, dimension_numbers=_NN, preferred_element_type=F32)
    return f(a_bf16, hi) + (f(a_bf16, mid) + f(a_bf16, lo))


_sdot = _dot


def _mm_kernel(x_ref, w_ref, o_ref):
    o_ref[...] = _dot(x_ref[...], w_ref[...]).astype(o_ref.dtype)


def _matmul(x, w, tm, tn, name):
    m, k = x.shape
    n = w.shape[1]
    return pl.pallas_call(
        _mm_kernel,
        grid=(m // tm, n // tn),
        in_specs=[pl.BlockSpec((tm, k), lambda i, j: (i, 0)),
                  pl.BlockSpec((k, tn), lambda i, j: (0, j))],
        out_specs=pl.BlockSpec((tm, tn), lambda i, j: (i, j)),
        out_shape=jax.ShapeDtypeStruct((m, n), F32),
        compiler_params=_cparams(("parallel", "arbitrary")),
        name=name,
    )(x, w)


WT_BLOCK = 512
WT_PIECE = 128


def _wt_relayout_kernel(src_ref, wt_hbm, o_ref, buf, sems):
    j = pl.program_id(0)
    n_plain = pl.num_programs(0) - 1
    n_pieces = WT_BLOCK // WT_PIECE

    def emit():
        for s in range(n_pieces):
            rows = slice(s * WT_PIECE, (s + 1) * WT_PIECE)
            o_ref[:, rows] = buf[rows, :].T.astype(BF16)

    @pl.when(j < n_plain)
    def _():
        row0 = pl.multiple_of(src_ref[j], 8)
        copies = [pltpu.make_async_copy(wt_hbm.at[pl.ds(row0 + s * WT_PIECE, WT_PIECE), :],
                                        buf.at[pl.ds(s * WT_PIECE, WT_PIECE), :], sems.at[s])
                  for s in range(n_pieces)]
        for c in copies:
            c.start()
        for c in copies:
            c.wait()
        emit()

    @pl.when(j == n_plain)
    def _():
        lo_w = 3 * D
        lo_a = lo_w + LORA_W
        lo_g = lo_a + LORA_A
        pieces = ((lo_w, 0, LORA_W), (lo_a, LORA_PAD, LORA_A), (lo_g, 2 * LORA_PAD, LORA_G))
        for _, dst, width in pieces[:2]:
            buf[dst + width:dst + LORA_PAD, :] = jnp.zeros((LORA_PAD - width, buf.shape[1]), F32)
        copies = [pltpu.make_async_copy(wt_hbm.at[pl.ds(src, width), :], buf.at[pl.ds(dst, width), :], sems.at[n])
                  for n, (src, dst, width) in enumerate(pieces)]
        for c in copies:
            c.start()
        for c in copies:
            c.wait()
        emit()


def _wt_relayout(wt):
    k = wt.shape[1]
    rw_end = RW_COLS
    cv_end = rw_end + 3 * CONV_DIM
    q_end = cv_end + MEM_DIM
    src = []
    for dst0, src0, width in ((COL_RKV, 0, 3 * D), (COL_GATE, q_end, 3 * D), (COL_CONV, rw_end, 3 * CONV_DIM),
                              (COL_Q, cv_end, MEM_DIM)):
        assert dst0 == len(src) * WT_BLOCK and width % WT_BLOCK == 0
        src += [src0 + b * WT_BLOCK for b in range(width // WT_BLOCK)]
    assert len(src) * WT_BLOCK == COL_LORA and LORA_COLS == WT_BLOCK
    grid_spec = pltpu.PrefetchScalarGridSpec(
        num_scalar_prefetch=1,
        grid=(len(src) + 1,),
        in_specs=[pl.BlockSpec(memory_space=pl.ANY)],
        out_specs=pl.BlockSpec((k, WT_BLOCK), lambda j, src_rows: (0, j)),
        scratch_shapes=[pltpu.VMEM((WT_BLOCK, k), F32), pltpu.SemaphoreType.DMA((WT_BLOCK // WT_PIECE,))],
    )
    return pl.pallas_call(
        _wt_relayout_kernel,
        grid_spec=grid_spec,
        out_shape=jax.ShapeDtypeStruct((k, P_COLS), BF16),
        compiler_params=_cparams(("arbitrary",)),
        name="w_in_relayout",
    )(jnp.asarray(src, jnp.int32), wt)


PROJ_TM = 512
PROJ_TN = 1536


def _in_proj_kernel(n_prompt_tiles, xp_ref, xs_ref, w_ref, o_ref, x_bf):
    i = pl.program_id(0)

    @pl.when(pl.program_id(1) == 0)
    def _():
        x_bf[...] = jnp.where(i < n_prompt_tiles, xp_ref[...], xs_ref[...]).astype(BF16)

    o_ref[...] = jnp.dot(x_bf[...], w_ref[...], preferred_element_type=F32)


def _in_proj(xp, xs, w):
    k = xp.shape[1]
    n_prompt_tiles = xp.shape[0] // PROJ_TM
    n_tiles = n_prompt_tiles + xs.shape[0] // PROJ_TM
    ncols = w.shape[1]
    return pl.pallas_call(
        functools.partial(_in_proj_kernel, n_prompt_tiles),
        grid=(n_tiles, ncols // PROJ_TN),
        in_specs=[pl.BlockSpec((PROJ_TM, k), lambda i, j: (jnp.minimum(i, n_prompt_tiles - 1), 0)),
                  pl.BlockSpec((PROJ_TM, k), lambda i, j: (jnp.maximum(i - n_prompt_tiles, 0), 0)),
                  pl.BlockSpec((k, PROJ_TN), lambda i, j: (0, j))],
        out_specs=pl.BlockSpec((PROJ_TM, PROJ_TN), lambda i, j: (i, j)),
        out_shape=jax.ShapeDtypeStruct((n_tiles * PROJ_TM, ncols), F32),
        scratch_shapes=[pltpu.VMEM((PROJ_TM, k), BF16)],
        compiler_params=_cparams(("arbitrary", "arbitrary")),
        name="in_proj",
    )(xp, xs, w)


def _head_sum(x, bd):
    parts = []
    for g in range(N_LANE_GROUPS):
        parts.append(_dot_exact_rhs(x[:, g * GROUP_LANES:(g + 1) * GROUP_LANES], bd))
    return jnp.concatenate(parts, axis=1)


def _prep_kernel(n_prompt_tiles, rkv_ref, lora_ref, c_rkv_ref, c_lora_ref, b_rkv_ref, b_lora_ref,
                 mu_rkv_ref, mu_lora_ref, w0_ref, a0_ref, kk_ref, ka_ref, rk_ref,
                 w2_ref, a2_ref, g2_ref, bd_ref,
                 r_o, k_o, v_o, kk_o, b_o, lw_o, g_o, bonus_o):
    i = pl.program_id(0)
    rows = rkv_ref.shape[0]
    row = lax.broadcasted_iota(jnp.int32, (rows, 1), 0)
    is_sample = i >= n_prompt_tiles
    seq_start = jnp.logical_and(is_sample, (row % SEQ_S) == 0)

    def mixed(x, carry_row, bnd, mu):
        prev = pltpu.roll(x, 1, 0)
        carry_row = jnp.where(i == 0, 0.0, carry_row)
        prev = jnp.where(row == 0, carry_row, prev)
        prev = jnp.where(seq_start, bnd, prev)
        return x + (prev - x) * mu

    def section(s):
        sl = slice(s * D, (s + 1) * D)
        return mixed(rkv_ref[:, sl], c_rkv_ref[7:8, sl], b_rkv_ref[:, sl], mu_rkv_ref[:, sl])

    lo = mixed(lora_ref[...], c_lora_ref[7:8, :], b_lora_ref[...], mu_lora_ref[...])
    w_lo = lo[:, 0:LORA_PAD]
    a_lo = lo[:, LORA_PAD:2 * LORA_PAD]
    g_lo = lo[:, 2 * LORA_PAD:]
    log_w = -DECAY_SCALE * _sigmoid(w0_ref[...] + _dot(jnp.tanh(w_lo), w2_ref[...]))
    a = _sigmoid(a0_ref[...] + _dot(a_lo, a2_ref[...]))
    g_o[...] = _dot(_sigmoid(g_lo), g2_ref[...])
    lw_o[...] = log_w

    bd = bd_ref[...]
    k = section(1)
    kk = k * kk_ref[...]
    ss = _head_sum(kk * kk, bd)
    kk = kk * lax.rsqrt(jnp.maximum(ss, 1e-24))
    kk_o[...] = kk
    b_o[...] = kk * a
    k = k * (1.0 + (a - 1.0) * ka_ref[...])
    k_o[...] = k
    r = section(0)
    r_o[...] = r
    v = section(2)
    v_o[...] = v
    bonus_o[...] = _head_sum(r * k * rk_ref[...], bd) * v


def _rwkv_prep(p, bnd_rkv, bnd_lora, mu_rkv, mu_lora, w0, a0, k_k, k_a, r_k, w2p, a2p, g2, bd, n_prompt):
    n = p.shape[0]
    n_prompt_tiles = n_prompt // TR
    carry_blk = TR // 8
    lora_blk = COL_LORA // LORA_COLS

    def row_spec(cols, cb=0):
        return pl.BlockSpec((TR, cols), lambda i: (i, cb))

    def carry_spec(cols, cb=0):
        return pl.BlockSpec((8, cols), lambda i: (jnp.maximum(i * carry_blk - 1, 0), cb))

    def bnd_spec(cols):
        return pl.BlockSpec((TR, cols), lambda i: (jnp.maximum(i - n_prompt_tiles, 0), 0))

    def const_spec(shape):
        return pl.BlockSpec(shape, lambda i: (0,) * len(shape))

    out = jax.ShapeDtypeStruct((n, D), F32)
    return pl.pallas_call(
        functools.partial(_prep_kernel, n_prompt_tiles),
        grid=(n // TR,),
        in_specs=[row_spec(3 * D), row_spec(LORA_COLS, lora_blk),
                  carry_spec(3 * D), carry_spec(LORA_COLS, lora_blk),
                  bnd_spec(3 * D), bnd_spec(LORA_COLS),
                  const_spec((1, 3 * D)), const_spec((1, LORA_COLS)),
                  const_spec((1, D)), const_spec((1, D)), const_spec((1, D)), const_spec((1, D)),
                  const_spec((1, D)),
                  const_spec((LORA_PAD, D)), const_spec((LORA_PAD, D)), const_spec((LORA_G, D)),
                  const_spec((GROUP_LANES, GROUP_LANES))],
        out_specs=[row_spec(D)] * 8,
        out_shape=[out] * 8,
        compiler_params=_cparams(("arbitrary",)),
        name="rwkv_prep",
    )(p, p, p, p, bnd_rkv, bnd_lora, mu_rkv, mu_lora, w0, a0, k_k, k_a, r_k, w2p, a2p, g2, bd)


def _scan_kernel(n_prompt_tiles, r_ref, k_ref, v_ref, kk_ref, b_ref, lw_ref, s_in_ref, y_ref, s_out_ref, s_scr):
    i = pl.program_id(0)
    is_sample = i >= n_prompt_tiles
    n_chunks = r_ref.shape[0] // CHUNK

    lane = lax.broadcasted_iota(jnp.int32, (1, GROUP_LANES), 1)
    head_masks = [(lane // HEAD == h).astype(F32) for h in range(GROUP_HEADS)]
    ri = lax.broadcasted_iota(jnp.int32, (STACK, 2 * STACK), 0)
    ci = lax.broadcasted_iota(jnp.int32, (STACK, 2 * STACK), 1)
    same_head = (ri // CHUNK) == ((ci % STACK) // CHUNK)
    strict_lower = jnp.logical_and(same_head, (ci % CHUNK) < (ri % CHUNK))
    mask_incl = jnp.logical_and(same_head, (ci % CHUNK) <= (ri % CHUNK)).astype(F32)
    mask_strict_b = jnp.logical_and(strict_lower, ci < STACK).astype(F32)
    mask_strict_k = jnp.logical_and(strict_lower, ci >= STACK).astype(F32)
    eye = (ri == ci).astype(F32)
    trow = lax.broadcasted_iota(jnp.int32, (CHUNK, 1), 0)

    def running_sum(x):
        d = 1
        while d < CHUNK:
            x = x + jnp.where(trow >= d, pltpu.roll(x, d, 0), 0.0)
            d *= 2
        return x

    rb = lax.broadcasted_iota(jnp.int32, (GROUP_LANES, GROUP_LANES), 0)
    cb = lax.broadcasted_iota(jnp.int32, (GROUP_LANES, GROUP_LANES), 1)
    block_diag = ((rb // HEAD) == (cb // HEAD)).astype(F32)

    def stack(x):
        return jnp.concatenate([x * m for m in head_masks], axis=0)

    def unstack(x):
        out = x[0:CHUNK]
        for h in range(1, GROUP_HEADS):
            out = out + x[h * CHUNK:(h + 1) * CHUNK]
        return out

    def compact(s):
        out = s[0:HEAD]
        for h in range(1, GROUP_HEADS):
            out = out + s[h * HEAD:(h + 1) * HEAD]
        return out

    @pl.when(i == 0)
    def _():
        s_out_ref[...] = jnp.zeros_like(s_out_ref)

    groups = range(N_LANE_GROUPS)
    lanes = [slice(g * GROUP_LANES, (g + 1) * GROUP_LANES) for g in groups]

    def twice(x):
        return jnp.concatenate([x, x], axis=0)

    def a0(c):
        rows = slice(c * CHUNK, (c + 1) * CHUNK)
        st = {"rows": rows}
        st["lw"] = [lw_ref[rows, lanes[g]] for g in groups]
        st["cum"] = [running_sum(st["lw"][g]) for g in groups]
        return st

    def a1(st):
        rows = st["rows"]
        lhs_s, bk_s, v_s, kr_t, p_end = [], [], [], [], []
        for g in groups:
            cum, lw = st["cum"][g], st["lw"][g]
            e_incl = jnp.exp(cum)
            e_excl = jnp.exp(cum - lw)
            e_neg = jnp.exp(-cum)
            p_end.append(e_incl[CHUNK - 1:CHUNK, :])
            r_t = r_ref[rows, lanes[g]] * e_incl
            kk_t = kk_ref[rows, lanes[g]] * e_excl
            b_t = b_ref[rows, lanes[g]] * e_neg
            k_t = k_ref[rows, lanes[g]] * e_neg
            kr_t.append(jnp.concatenate([kk_t, r_t], axis=0))
            lhs_s.append(jnp.concatenate([stack(kk_t), stack(r_t)], axis=0))
            bk_s.append(jnp.concatenate([stack(b_t), stack(k_t)], axis=0))
            v_s.append(stack(v_ref[rows, lanes[g]]))
        st.update(bk_s=bk_s, v_s=v_s, kr_t=kr_t, p_end=p_end)
        st["mn"] = [_sdot(lhs_s[g], bk_s[g], _NT) for g in groups]

    def a2(st):
        mn = st.pop("mn")
        st["m1"] = [mn[g][0:STACK] * mask_strict_b for g in groups]
        m_k = [mn[g][0:STACK] * mask_strict_k for g in groups]
        st["n_bk"] = [mn[g][STACK:] * mask_incl for g in groups]
        st["m2"] = [_sdot(st["m1"][g], twice(st["m1"][g])) for g in groups]
        st["mv"] = [_sdot(m_k[g], twice(st["v_s"][g])) for g in groups]

    def a3(st):
        st["m4"] = [_sdot(st["m2"][g], twice(st["m2"][g])) for g in groups]
        st["t_inv"] = [_sdot(eye - st["m1"][g], twice(eye + st["m2"][g])) for g in groups]

    def a4(st):
        st["m8"] = [_sdot(st["m4"][g], twice(st["m4"][g])) for g in groups]
        st["t_inv"] = [_sdot(st["t_inv"][g], twice(eye + st["m4"][g])) for g in groups]

    def a5(st):
        st["t_inv"] = [_sdot(st["t_inv"][g], twice(eye + st["m8"][g])) for g in groups]

    def b1(c, st, s_prev):
        load_state = is_sample if c > 0 else jnp.logical_or(is_sample, i == 0)
        slot = jnp.where(is_sample, c, 0)
        s0 = []
        for g in groups:
            s_loaded = jnp.concatenate([s_in_ref[slot, :, lanes[g]]] * GROUP_HEADS, axis=0) * block_diag
            s0.append(jnp.where(load_state, s_loaded, s_prev[g]))
        st["s0"] = s0
        st["gr"] = [_sdot(st["kr_t"][g], s0[g], _NT) for g in groups]

    def b2(st):
        u_s = [-_sdot(st["t_inv"][g], twice(stack(st["gr"][g][0:CHUNK]) + st["mv"][g])) for g in groups]
        st["uv"] = [jnp.concatenate([u_s[g], st["v_s"][g]], axis=0) for g in groups]

    def b3(c, st):
        slot = jnp.where(is_sample, c, 0)
        for g in groups:
            y_ref[st["rows"], lanes[g]] = st["gr"][g][CHUNK:] + unstack(_sdot(st["n_bk"][g], st["uv"][g]))
        s_new = []
        for g in groups:
            s_new.append(st["s0"][g] * st["p_end"][g] + _sdot(st["uv"][g], st["bk_s"][g] * st["p_end"][g], _TN))
            s_out_ref[slot, :, lanes[g]] = compact(s_new[g])
        return s_new

    s_cur = [s_scr[g] for g in groups]
    cur = a0(0)
    for lvl in (a1, a2, a3, a4, a5):
        lvl(cur)
    for c in range(n_chunks):
        last = c == n_chunks - 1
        nxt = None if last else a0(c + 1)
        b1(c, cur, s_cur)
        if not last:
            a1(nxt)
        b2(cur)
        if not last:
            a2(nxt)
        s_cur = b3(c, cur)
        if not last:
            for lvl in (a3, a4, a5):
                lvl(nxt)
        cur = nxt
    for g in groups:
        s_scr[g] = s_cur[g]


def _rwkv_scan(r, k, v, kk, b, lw, s_in, n_prompt):
    n = r.shape[0]
    n_prompt_tiles = n_prompt // SCAN_ROWS
    row_spec = pl.BlockSpec((SCAN_ROWS, D), lambda i: (i, 0))
    state_spec = pl.BlockSpec((STATE_SLOTS, HEAD, D),
                              lambda i: (jnp.maximum(i - n_prompt_tiles + 1, 0), 0, 0))
    return pl.pallas_call(
        functools.partial(_scan_kernel, n_prompt_tiles),
        grid=(n // SCAN_ROWS,),
        in_specs=[row_spec] * 6 + [state_spec],
        out_specs=[row_spec, state_spec],
        out_shape=[jax.ShapeDtypeStruct((n, D), F32), jax.ShapeDtypeStruct(s_in.shape, F32)],
        scratch_shapes=[pltpu.VMEM((N_LANE_GROUPS, GROUP_LANES, GROUP_LANES), F32)],
        compiler_params=_cparams(("arbitrary",)),
        name="rwkv_scan",
    )(r, k, v, kk, b, lw, s_in)


def _conv_kernel(n_prompt_tiles, cb_ref, cc_ref, ch_ref, ccc_ref, cch_ref, bnd1_ref, bnd2_ref,
                 cw_ref, wout_ref, o_ref, u_ref):
    i = pl.program_id(0)
    rows = cb_ref.shape[0]
    row = lax.broadcasted_iota(jnp.int32, (rows, 1), 0)
    is_sample = i >= n_prompt_tiles
    pos = row % SEQ_S
    u = cc_ref[...] * ch_ref[...]
    u_ref[...] = u
    u_prev = jnp.where(i == 0, 0.0, ccc_ref[...] * cch_ref[...])
    prev1 = pltpu.roll(u, 1, 0)
    prev1 = jnp.where(row == 0, u_prev[7:8, :], prev1)
    prev2 = pltpu.roll(u, 2, 0)
    prev2 = jnp.where(row == 0, u_prev[6:7, :], prev2)
    prev2 = jnp.where(row == 1, u_prev[7:8, :], prev2)
    bnd1 = bnd1_ref[...]
    prev1 = jnp.where(jnp.logical_and(is_sample, pos == 0), bnd1, prev1)
    prev2 = jnp.where(jnp.logical_and(is_sample, pos == 0), bnd2_ref[...], prev2)
    prev2 = jnp.where(jnp.logical_and(is_sample, pos == 1), bnd1, prev2)
    cw = cw_ref[...]
    conv = prev2 * cw[0:1, :] + prev1 * cw[1:2, :] + u * cw[2:3, :]
    o_ref[...] = _dot(cb_ref[...] * conv, wout_ref[...])


def _short_conv(p, bnd1, bnd2, conv_w, w_out, n_prompt):
    n = p.shape[0]
    n_prompt_tiles = n_prompt // TR
    cblk = COL_CONV // CONV_DIM
    carry_blk = TR // 8

    def row_spec(cb):
        return pl.BlockSpec((TR, CONV_DIM), lambda i: (i, cb))

    def carry_spec(cb):
        return pl.BlockSpec((8, CONV_DIM), lambda i: (jnp.maximum(i * carry_blk - 1, 0), cb))

    bnd_spec = pl.BlockSpec((TR, CONV_DIM), lambda i: (jnp.maximum(i - n_prompt_tiles, 0), 0))
    return pl.pallas_call(
        functools.partial(_conv_kernel, n_prompt_tiles),
        grid=(n // TR,),
        in_specs=[row_spec(cblk), row_spec(cblk + 1), row_spec(cblk + 2),
                  carry_spec(cblk + 1), carry_spec(cblk + 2), bnd_spec, bnd_spec,
                  pl.BlockSpec((8, CONV_DIM), lambda i: (0, 0)),
                  pl.BlockSpec((CONV_DIM, D), lambda i: (0, 0))],
        out_specs=[pl.BlockSpec((TR, D), lambda i: (i, 0)), pl.BlockSpec((TR, CONV_DIM), lambda i: (i, 0))],
        out_shape=[jax.ShapeDtypeStruct((n, D), F32), jax.ShapeDtypeStruct((n, CONV_DIM), F32)],
        compiler_params=_cparams(("arbitrary",)),
        name="short_conv",
    )(p, p, p, p, p, bnd1, bnd2, conv_w, w_out)


def _mem_kernel(q_ref, k_ref, v_ref, wo_ref, o_ref):
    n_seq = k_ref.shape[0]
    rows = q_ref.shape[0] // n_seq
    per_seq = []
    for s_i in range(n_seq):
        q = q_ref[s_i * rows:(s_i + 1) * rows, :]
        k = k_ref[s_i]
        v = v_ref[s_i]
        outs = []
        for h in range(MEM_HEADS):
            sl = slice(h * MEM_HEAD_DIM, (h + 1) * MEM_HEAD_DIM)
            s = _dot(q[:, sl], k[:, sl], _NT) * (MEM_HEAD_DIM ** -0.5)
            s = s - jnp.max(s, axis=-1, keepdims=True)
            e = jnp.exp(s)
            pr = e / jnp.sum(e, axis=-1, keepdims=True)
            outs.append(_dot(pr, v[:, sl]))
        per_seq.append(jnp.concatenate(outs, axis=1))
    o_ref[...] = _dot(jnp.concatenate(per_seq, axis=0), wo_ref[...])


MEM_SEQS = 4


def _mem_sample(p, mem_k, mem_v, w_o, row_start, n_seq):
    qblk = COL_Q // MEM_DIM
    rows = MEM_SEQS * SEQ_S
    rb0 = row_start // rows
    return pl.pallas_call(
        _mem_kernel,
        grid=(n_seq // MEM_SEQS,),
        in_specs=[pl.BlockSpec((rows, MEM_DIM), lambda i: (rb0 + i, qblk)),
                  pl.BlockSpec((MEM_SEQS, N_MEM, MEM_DIM), lambda i: (i, 0, 0)),
                  pl.BlockSpec((MEM_SEQS, N_MEM, MEM_DIM), lambda i: (i, 0, 0)),
                  pl.BlockSpec((MEM_DIM, D), lambda i: (0, 0))],
        out_specs=pl.BlockSpec((rows, D), lambda i: (i, 0)),
        out_shape=jax.ShapeDtypeStruct((n_seq * SEQ_S, D), F32),
        compiler_params=_cparams(("arbitrary",)),
        name="mem_attention_sample",
    )(p, mem_k, mem_v, w_o)


MEM_TILE = 256


def _mem_prompt_kernel(n_prompt_tiles, q_ref, k_ref, v_ref, wo_ref, tail_ref, o_ref):
    i = pl.program_id(0)

    @pl.when(i < n_prompt_tiles)
    def _():
        _mem_kernel(q_ref, k_ref, v_ref, wo_ref, o_ref)

    @pl.when(i >= n_prompt_tiles)
    def _():
        o_ref[...] = tail_ref[...]


def _mem_attention(p, mem_k, mem_v, w_o, o_sample, n_prompt):
    n = p.shape[0]
    qblk = COL_Q // MEM_DIM
    n_prompt_tiles = n_prompt // MEM_TILE
    return pl.pallas_call(
        functools.partial(_mem_prompt_kernel, n_prompt_tiles),
        grid=(n // MEM_TILE,),
        in_specs=[pl.BlockSpec((MEM_TILE, MEM_DIM), lambda i: (jnp.minimum(i, n_prompt_tiles - 1), qblk)),
                  pl.BlockSpec((1, N_MEM, MEM_DIM), lambda i: (0, 0, 0)),
                  pl.BlockSpec((1, N_MEM, MEM_DIM), lambda i: (0, 0, 0)),
                  pl.BlockSpec((MEM_DIM, D), lambda i: (0, 0)),
                  pl.BlockSpec((MEM_TILE, D), lambda i: (jnp.maximum(i - n_prompt_tiles, 0), 0))],
        out_specs=pl.BlockSpec((MEM_TILE, D), lambda i: (i, 0)),
        out_shape=jax.ShapeDtypeStruct((n, D), F32),
        compiler_params=_cparams(("arbitrary",)),
        name="mem_attention",
    )(p, mem_k, mem_v, w_o, o_sample)


def _layer_norm(z, g, b):
    mu = jnp.mean(z, axis=-1, keepdims=True)
    d = z - mu
    var = jnp.mean(d * d, axis=-1, keepdims=True)
    return d * lax.rsqrt(var + LN_EPS) * g + b


def _merge_kernel(n_prompt_tiles, xp_ref, xs_ref, ga_ref, gb_ref, gm_ref, y_ref, bonus_ref, g_ref, ocv_ref, omem_ref,
                  gng_ref, gnb_ref, bd_ref, wo_ref, l1g_ref, l1b_ref, wr_ref, h_o, lt_o):
    x = jnp.where(pl.program_id(0) < n_prompt_tiles, xp_ref[...], xs_ref[...])
    bd = bd_ref[...]
    y = y_ref[...]
    mean = _head_sum(y, bd) * (1.0 / HEAD)
    d = y - mean
    var = _head_sum(d * d, bd) * (1.0 / HEAD)
    yn = d * lax.rsqrt(var + GN_EPS) * gng_ref[...] + gnb_ref[...]
    o_rw = (yn + bonus_ref[...]) * g_ref[...]
    merged = (_sigmoid(ga_ref[...]) * o_rw + _sigmoid(gb_ref[...]) * ocv_ref[...]
              + _sigmoid(gm_ref[...]) * omem_ref[...])
    z = ALPHA * x + _dot(merged, wo_ref[...])
    h = _layer_norm(z, l1g_ref[...], l1b_ref[...])
    h_o[...] = h
    lt_o[...] = _dot3(wr_ref[...], h, _NT)


def _merge_ln1(xp, xs, p, y_raw, bonus, g, o_cv, o_mem, gn_g, gn_b, bd, w_o, ln_g, ln_b, w_router_t):
    n = p.shape[0]
    n_prompt_tiles = xp.shape[0] // TR
    gblk = COL_GATE // D
    row = pl.BlockSpec((TR, D), lambda i: (i, 0))
    xp_spec = pl.BlockSpec((TR, D), lambda i: (jnp.minimum(i, n_prompt_tiles - 1), 0))
    xs_spec = pl.BlockSpec((TR, D), lambda i: (jnp.maximum(i - n_prompt_tiles, 0), 0))

    def gate_spec(j):
        return pl.BlockSpec((TR, D), lambda i: (i, gblk + j))

    def const_spec(shape):
        return pl.BlockSpec(shape, lambda i: (0,) * len(shape))

    vec = const_spec((1, D))
    return pl.pallas_call(
        functools.partial(_merge_kernel, n_prompt_tiles),
        grid=(n // TR,),
        in_specs=[xp_spec, xs_spec, gate_spec(0), gate_spec(1), gate_spec(2), row, row, row, row, row,
                  vec, vec, const_spec((GROUP_LANES, GROUP_LANES)), const_spec((D, D)), vec, vec,
                  const_spec((N_EXPERTS, D))],
        out_specs=[row, pl.BlockSpec((N_EXPERTS, TR), lambda i: (0, i))],
        out_shape=[jax.ShapeDtypeStruct((n, D), F32), jax.ShapeDtypeStruct((N_EXPERTS, n), F32)],
        compiler_params=_cparams(("arbitrary",)),
        name="merge_ln1",
    )(xp, xs, p, p, p, y_raw, bonus, g, o_cv, o_mem, gn_g, gn_b, bd, w_o, ln_g, ln_b, w_router_t)


ROUTE_TILE = 256


def _routing_kernel(lt_ref, bias_ref, tri_ref, idx_o, w_o, pos_o, cnt_o, carry):
    i = pl.program_id(0)
    tile = lt_ref.shape[1]

    @pl.when(i == 0)
    def _():
        carry[...] = jnp.zeros_like(carry)

    neg_inf = -jnp.inf
    scores = _sigmoid(lt_ref[...])
    choice = scores + bias_ref[...]
    row = lax.broadcasted_iota(jnp.int32, (N_EXPERTS, tile), 0)
    rowf = row.astype(F32)
    grpf = (row // GROUP_SIZE).astype(F32)

    def group_allreduce(x, op):
        for s in (1, 2, 4):
            up = pltpu.roll(x, N_EXPERTS - s, 0)
            dn = pltpu.roll(x, s, 0)
            x = op(x, jnp.where((row & s) == 0, up, dn))
        return x

    m1 = group_allreduce(choice, jnp.maximum)
    first = group_allreduce(jnp.where(choice == m1, rowf, float(N_EXPERTS)), jnp.minimum)
    m2 = group_allreduce(jnp.where(rowf == first, neg_inf, choice), jnp.maximum)
    gscore = m1 + m2

    gsel = jnp.zeros_like(choice)
    for _ in range(TOPK_GROUPS):
        gmax = jnp.max(gscore, axis=0, keepdims=True)
        pick = jnp.min(jnp.where(gscore == gmax, grpf, float(N_GROUPS)), axis=0, keepdims=True)
        hit = grpf == pick
        gsel = jnp.where(hit, 1.0, gsel)
        gscore = jnp.where(hit, neg_inf, gscore)

    masked = jnp.where(gsel > 0.0, choice, neg_inf)
    row8 = lax.broadcasted_iota(jnp.int32, (TOP_K, tile), 0)
    idx_acc = jnp.zeros((TOP_K, tile), F32)
    w_acc = jnp.zeros((TOP_K, tile), F32)
    sel_all = jnp.zeros_like(choice)
    for kk in range(TOP_K):
        mx = jnp.max(masked, axis=0, keepdims=True)
        pick = jnp.min(jnp.where(masked == mx, rowf, float(N_EXPERTS)), axis=0, keepdims=True)
        hit = rowf == pick
        wk = jnp.sum(jnp.where(hit, scores, 0.0), axis=0, keepdims=True)
        idx_acc = jnp.where(row8 == kk, pick, idx_acc)
        w_acc = jnp.where(row8 == kk, wk, w_acc)
        sel_all = jnp.where(hit, 1.0, sel_all)
        masked = jnp.where(hit, neg_inf, masked)

    w_sum = jnp.sum(w_acc, axis=0, keepdims=True)
    w_o[...] = w_acc / w_sum * ROUTED_SCALE
    idx_o[...] = idx_acc.astype(jnp.int32)

    prefix = lax.dot_general(sel_all.astype(BF16), tri_ref[...], _NN, preferred_element_type=F32) + carry[...]
    pos_acc = jnp.zeros((TOP_K, tile), F32)
    for kk in range(TOP_K):
        hit = rowf == idx_acc[kk:kk + 1, :]
        pk = jnp.sum(jnp.where(hit, prefix, 0.0), axis=0, keepdims=True)
        pos_acc = jnp.where(row8 == kk, pk, pos_acc)
    pos_o[...] = pos_acc.astype(jnp.int32)
    carry[...] = carry[...] + jnp.sum(sel_all, axis=1, keepdims=True)
    cnt_o[...] = carry[...]


def _routing(logits_t, bias_col, tri):
    n = logits_t.shape[1]
    tile = ROUTE_TILE
    tok = pl.BlockSpec((TOP_K, tile), lambda i: (0, i))
    return pl.pallas_call(
        _routing_kernel,
        grid=(n // tile,),
        in_specs=[pl.BlockSpec((N_EXPERTS, tile), lambda i: (0, i)),
                  pl.BlockSpec((N_EXPERTS, 1), lambda i: (0, 0)),
                  pl.BlockSpec((tile, tile), lambda i: (0, 0))],
        out_specs=[tok, tok, tok, pl.BlockSpec((N_EXPERTS, 1), lambda i: (0, 0))],
        out_shape=[jax.ShapeDtypeStruct((TOP_K, n), jnp.int32), jax.ShapeDtypeStruct((TOP_K, n), F32),
                   jax.ShapeDtypeStruct((TOP_K, n), jnp.int32), jax.ShapeDtypeStruct((N_EXPERTS, 1), F32)],
        scratch_shapes=[pltpu.VMEM((N_EXPERTS, 1), F32)],
        compiler_params=_cparams(("arbitrary",)),
        name="routing",
    )(logits_t, bias_col, tri)


ZERO_ROWS = 128
PACKED = D // 2


def _pack_rows(h):
    hi = pltpu.bitcast(h[:, :PACKED].astype(BF16).astype(F32), jnp.uint32)
    lo = pltpu.bitcast(h[:, PACKED:].astype(BF16).astype(F32), jnp.uint32)
    return hi | (lo >> 16)


def _unpack_halves(w):
    return pltpu.bitcast(w & jnp.uint32(0xFFFF0000), F32), pltpu.bitcast(w << 16, F32)


def _unpack_rows(w):
    first, second = _unpack_halves(w)
    return jnp.concatenate([first.astype(BF16), second.astype(BF16)], axis=1)


def _dispatch_kernel(zs_ref, zc_ref, dest_ref, h_ref, xb_out, dest_smem, zbuf, packed, sem, idx_sem, zsem):
    i = pl.program_id(0)
    rows = h_ref.shape[0]

    @pl.when(i == 0)
    def _():
        zbuf[...] = jnp.zeros_like(zbuf)

        def zero_copy(piece):
            dst0 = pl.multiple_of(piece * ZERO_ROWS, ZERO_ROWS)
            return pltpu.make_async_copy(zbuf, xb_out.at[pl.ds(dst0, ZERO_ROWS), :], zsem)

        def per_range(e, c):
            def issue_piece(j, c2):
                zero_copy(zs_ref[e] + j).start()
                return c2
            lax.fori_loop(0, zc_ref[e], issue_piece, 0)
            return c

        def per_range_wait(e, c):
            def wait_piece(j, c2):
                zero_copy(zs_ref[e] + j).wait()
                return c2
            lax.fori_loop(0, zc_ref[e], wait_piece, 0)
            return c

        lax.fori_loop(0, N_EXPERTS + 1, per_range, 0)
        lax.fori_loop(0, N_EXPERTS + 1, per_range_wait, 0)

    cp = pltpu.make_async_copy(dest_ref, dest_smem, idx_sem)
    cp.start()
    packed[...] = _pack_rows(h_ref[...])
    cp.wait()

    def row_copy(t, k):
        return pltpu.make_async_copy(packed.at[pl.ds(t, 1), :],
                                     xb_out.at[pl.ds(dest_smem[k, t], 1), :], sem)

    def issue(t, c):
        for k in range(TOP_K):
            row_copy(t, k).start()
        return c

    def drain(t, c):
        for k in range(TOP_K):
            row_copy(t, k).wait()
        return c

    lax.fori_loop(0, rows, issue, 0)
    lax.fori_loop(0, rows, drain, 0)


def _dispatch(zero_start, zero_count, dest_t, h, n_rows):
    n = h.shape[0]
    grid_spec = pltpu.PrefetchScalarGridSpec(
        num_scalar_prefetch=2,
        grid=(n // TR,),
        in_specs=[pl.BlockSpec((TOP_K, TR), lambda i, zs, zc: (0, i)),
                  pl.BlockSpec((TR, D), lambda i, zs, zc: (i, 0))],
        out_specs=pl.BlockSpec(memory_space=pl.ANY),
        scratch_shapes=[pltpu.SMEM((TOP_K, TR), jnp.int32), pltpu.VMEM((ZERO_ROWS, PACKED), jnp.uint32),
                        pltpu.VMEM((TR, PACKED), jnp.uint32),
                        pltpu.SemaphoreType.DMA, pltpu.SemaphoreType.DMA, pltpu.SemaphoreType.DMA],
    )
    return pl.pallas_call(
        _dispatch_kernel,
        grid_spec=grid_spec,
        out_shape=jax.ShapeDtypeStruct((n_rows, PACKED), jnp.uint32),
        compiler_params=_cparams(("arbitrary",)),
        name="moe_dispatch",
    )(zero_start, zero_count, dest_t, h)


def _silu(x):
    return x * _sigmoid(x)


def _expert_kernel(be_ref, nu_ref, first_ref, slot_ref, next_ref, x_ref, wu_hbm, wd_hbm, o_ref,
                   wu_f32, wd_f32, wu_bf, wd_bf, sem_u, sem_d):
    b = pl.program_id(0)

    def weight_copies(e, slot):
        return (pltpu.make_async_copy(wu_hbm.at[e], wu_f32.at[slot], sem_u.at[slot]),
                pltpu.make_async_copy(wd_hbm.at[e], wd_f32.at[slot], sem_d.at[slot]))

    @pl.when(b == 0)
    def _():
        for cp in weight_copies(be_ref[0], 0):
            cp.start()

    @pl.when(first_ref[b] == 1)
    def _():
        slot = slot_ref[b]
        for cp in weight_copies(be_ref[b], slot):
            cp.wait()
        wu_bf[...] = wu_f32[slot].astype(BF16)
        wd_bf[...] = wd_f32[slot].astype(BF16)

        @pl.when(next_ref[b] >= 0)
        def _():
            for cp in weight_copies(next_ref[b], 1 - slot):
                cp.start()

    @pl.when(b < nu_ref[0])
    def _():
        up = _dot(_unpack_rows(x_ref[...]), wu_bf[...])
        act = _silu(up[:, :EXPERT_FF]) * up[:, EXPERT_FF:]
        o_ref[...] = _pack_rows(_dot(act, wd_bf[...]))

    @pl.when(b >= nu_ref[0])
    def _():
        o_ref[...] = jnp.zeros_like(o_ref)


def _experts(block_e, n_used, xb, w_up, w_down):
    rows = xb.shape[0]
    nb = rows // EXPERT_BM

    bidx = jnp.arange(nb, dtype=jnp.int32)
    prev_e = jnp.concatenate([jnp.full((1,), -1, jnp.int32), block_e[:-1]])
    first = jnp.logical_and(bidx < n_used[0], block_e != prev_e)
    slot = (jnp.cumsum(first.astype(jnp.int32)) - 1) % 2
    first_pos = jnp.where(first, bidx, nb)
    next_first = jnp.concatenate([jnp.flip(lax.cummin(jnp.flip(first_pos)))[1:], jnp.full((1,), nb, jnp.int32)])
    next_e = jnp.where(next_first < nb, block_e[jnp.minimum(next_first, nb - 1)], -1)

    def xmap(b, be, nu, fi, sl, ne):
        return (jnp.minimum(b, nu[0] - 1), 0)

    grid_spec = pltpu.PrefetchScalarGridSpec(
        num_scalar_prefetch=5,
        grid=(nb,),
        in_specs=[pl.BlockSpec((EXPERT_BM, PACKED), xmap),
                  pl.BlockSpec(memory_space=pl.ANY),
                  pl.BlockSpec(memory_space=pl.ANY)],
        out_specs=pl.BlockSpec((EXPERT_BM, PACKED), lambda b, be, nu, fi, sl, ne: (b, 0)),
        scratch_shapes=[pltpu.VMEM((2, D, 2 * EXPERT_FF), F32), pltpu.VMEM((2, EXPERT_FF, D), F32),
                        pltpu.VMEM((D, 2 * EXPERT_FF), BF16), pltpu.VMEM((EXPERT_FF, D), BF16),
                        pltpu.SemaphoreType.DMA((2,)), pltpu.SemaphoreType.DMA((2,))],
    )
    return pl.pallas_call(
        _expert_kernel,
        grid_spec=grid_spec,
        out_shape=jax.ShapeDtypeStruct((rows, PACKED), jnp.uint32),
        compiler_params=_cparams(("arbitrary",)),
        name="moe_experts",
    )(block_e, n_used, first.astype(jnp.int32), slot.astype(jnp.int32), next_e.astype(jnp.int32), xb, w_up, w_down)


SHARED_TILE = 512


def _shared_kernel(h_ref, wu_ref, wd_ref, o_ref):
    up = _dot(h_ref[...], wu_ref[...])
    act = _silu(up[:, :SHARED_FF]) * up[:, SHARED_FF:]
    o_ref[...] = _dot(act, wd_ref[...])


def _shared_ffn(h, w_up, w_down):
    n = h.shape[0]
    row = pl.BlockSpec((SHARED_TILE, D), lambda i: (i, 0))
    return pl.pallas_call(
        _shared_kernel,
        grid=(n // SHARED_TILE,),
        in_specs=[row, pl.BlockSpec((D, 2 * SHARED_FF), lambda i: (0, 0)),
                  pl.BlockSpec((SHARED_FF, D), lambda i: (0, 0))],
        out_specs=row,
        out_shape=jax.ShapeDtypeStruct((n, D), F32),
        compiler_params=_cparams(("parallel",)),
        name="shared_ffn",
    )(h, w_up, w_down)


def _combine_kernel(n_prompt_tiles, dest_ref, w_ref, h_ref, sh_ref, yb_ref, l2g_ref, l2b_ref, yp_o, ys_o,
                    buf, dest_smem, sem, idx_sem):
    i = pl.program_id(0)
    rows = h_ref.shape[0]
    cp = pltpu.make_async_copy(dest_ref, dest_smem, idx_sem)
    cp.start()
    cp.wait()

    def row_copy(t, k):
        return pltpu.make_async_copy(yb_ref.at[pl.ds(dest_smem[k, t], 1), :],
                                     buf.at[k, pl.ds(t, 1), :], sem)

    def issue(t, c):
        for k in range(TOP_K):
            row_copy(t, k).start()
        return c

    def drain(t, c):
        for k in range(TOP_K):
            row_copy(t, k).wait()
        return c

    lax.fori_loop(0, rows, issue, 0)
    lax.fori_loop(0, rows, drain, 0)
    w = w_ref[...]
    sh = sh_ref[...]
    f_first, f_second = sh[:, :PACKED], sh[:, PACKED:]
    for k in range(TOP_K):
        y_first, y_second = _unpack_halves(buf[k])
        f_first = f_first + w[:, k:k + 1] * y_first
        f_second = f_second + w[:, k:k + 1] * y_second
    z = ALPHA * h_ref[...] + jnp.concatenate([f_first, f_second], axis=1)
    y = _layer_norm(z, l2g_ref[...], l2b_ref[...])

    @pl.when(i < n_prompt_tiles)
    def _():
        yp_o[...] = y

    @pl.when(i >= n_prompt_tiles)
    def _():
        ys_o[...] = y


def _combine_ln2(dest_t, w_tok, h, shared, yb, ln_g, ln_b, n_prompt):
    n = h.shape[0]
    n_prompt_tiles = n_prompt // TR
    row = pl.BlockSpec((TR, D), lambda i: (i, 0))
    vec = pl.BlockSpec((1, D), lambda i: (0, 0))
    return pl.pallas_call(
        functools.partial(_combine_kernel, n_prompt_tiles),
        grid=(n // TR,),
        in_specs=[pl.BlockSpec((TOP_K, TR), lambda i: (0, i)),
                  pl.BlockSpec((TR, TOP_K), lambda i: (i, 0)),
                  row, row, pl.BlockSpec(memory_space=pl.ANY), vec, vec],
        out_specs=[pl.BlockSpec((TR, D), lambda i: (jnp.minimum(i, n_prompt_tiles - 1), 0)),
                   pl.BlockSpec((TR, D), lambda i: (jnp.maximum(i - n_prompt_tiles, 0), 0))],
        out_shape=[jax.ShapeDtypeStruct((n_prompt, D), F32), jax.ShapeDtypeStruct((n - n_prompt, D), F32)],
        scratch_shapes=[pltpu.VMEM((TOP_K, TR, PACKED), jnp.uint32), pltpu.SMEM((TOP_K, TR), jnp.int32),
                        pltpu.SemaphoreType.DMA, pltpu.SemaphoreType.DMA],
        compiler_params=_cparams(("arbitrary",)),
        name="moe_combine_ln2",
    )(dest_t, w_tok, h, shared, yb, ln_g, ln_b)


def _rw_cols_split(v):
    pad = jnp.zeros(v.shape[:-1] + (LORA_PAD - LORA_W,), v.dtype)
    lora = jnp.concatenate([v[..., 3 * D:3 * D + LORA_W], pad,
                            v[..., 3 * D + LORA_W:3 * D + LORA_W + LORA_A], pad,
                            v[..., 3 * D + LORA_W + LORA_A:]], axis=-1)
    return v[..., 0:3 * D], lora


def _pad_rows(w, rows):
    return jnp.concatenate([w, jnp.zeros((rows - w.shape[0],) + w.shape[1:], w.dtype)], axis=0)


def kernel(x_prompt, x_sample, mem_prompt, state_rwkv, state_shift, state_conv, cache_mem_k, cache_mem_v,
           w_in, mu_shift, rw_w0, rw_w2, rw_a0, rw_a2, rw_g2, rw_k_k, rw_k_a, rw_r_k, rw_gn_g, rw_gn_b,
           conv_w, w_conv_out, w_mem_k, w_mem_v, w_mem_o, w_o, ln1_g, ln1_b, w_router, router_bias,
           w_exp_up, w_exp_down, w_sh_up, w_sh_down, ln2_g, ln2_b):
    n_prompt = x_prompt.shape[0] * x_prompt.shape[1]
    n_seq_s, seq_s = x_sample.shape[0], x_sample.shape[1]
    n_sample = n_seq_s * seq_s
    n = n_prompt + n_sample
    assert x_prompt.shape[0] == 1 and seq_s == SEQ_S and n_prompt % TR == 0 and n_sample % TR == 0
    assert n % SHARED_TILE == 0 and n % ROUTE_TILE == 0 and w_in.shape[0] == 1
    assert n_prompt % PROJ_TM == 0 and n_sample % PROJ_TM == 0

    xp = x_prompt.reshape(n_prompt, D)
    xs = x_sample.reshape(n_sample, D)

    def vec(v):
        return v.reshape(1, -1).astype(F32)

    w_in_r = _wt_relayout(w_in[0].T)
    p = _in_proj(xp, xs, w_in_r)

    w_kv = jnp.concatenate([w_mem_k[0], w_mem_v[0]], axis=1).astype(BF16)
    kv = _matmul(mem_prompt[0], w_kv, N_MEM, 512, "mem_kv")
    mem_k_p, mem_v_p = kv[:, :MEM_DIM], kv[:, MEM_DIM:]

    mu_rkv, mu_lora = _rw_cols_split(vec(mu_shift[0]))
    sh_rkv, sh_lora = _rw_cols_split(state_shift[0, :, 0, :])
    bnd_rkv = jnp.repeat(sh_rkv, seq_s, axis=0)
    bnd_lora = jnp.repeat(sh_lora, seq_s, axis=0)
    hi = lax.broadcasted_iota(jnp.int32, (GROUP_LANES, GROUP_LANES), 0) // HEAD
    hj = lax.broadcasted_iota(jnp.int32, (GROUP_LANES, GROUP_LANES), 1) // HEAD
    bd = (hi == hj).astype(BF16)
    r, k, v, kk, b, lw, g, bonus = _rwkv_prep(
        p, bnd_rkv, bnd_lora, mu_rkv, mu_lora, vec(rw_w0[0]), vec(rw_a0[0]), vec(rw_k_k[0]), vec(rw_k_a[0]),
        vec(rw_r_k[0]), _pad_rows(rw_w2[0], LORA_PAD).astype(BF16), _pad_rows(rw_a2[0], LORA_PAD).astype(BF16),
        rw_g2[0].astype(BF16), bd, n_prompt)

    s_sample = jnp.transpose(state_rwkv[0], (0, 2, 1, 3)).reshape(n_seq_s, HEAD, D)
    s_in = jnp.concatenate([jnp.zeros((STATE_SLOTS, HEAD, D), F32), s_sample.astype(F32)], axis=0)
    y_raw, s_out = _rwkv_scan(r, k, v, kk, b, lw, s_in, n_prompt)

    bnd1 = jnp.repeat(state_conv[0, :, 1, :], seq_s, axis=0)
    bnd2 = jnp.repeat(state_conv[0, :, 0, :], seq_s, axis=0)
    o_cv, u = _short_conv(p, bnd1, bnd2, _pad_rows(conv_w[0], 8), w_conv_out[0].astype(BF16), n_prompt)

    w_mem_o_b = w_mem_o[0].astype(BF16)
    o_mem_s = _mem_sample(p, cache_mem_k[0].reshape(n_seq_s, N_MEM, MEM_DIM),
                          cache_mem_v[0].reshape(n_seq_s, N_MEM, MEM_DIM), w_mem_o_b, n_prompt, n_seq_s)
    o_mem = _mem_attention(p, mem_k_p[None], mem_v_p[None], w_mem_o_b, o_mem_s, n_prompt)

    h, logits_t = _merge_ln1(xp, xs, p, y_raw, bonus, g, o_cv, o_mem, vec(rw_gn_g[0]), vec(rw_gn_b[0]), bd,
                             w_o[0].astype(BF16), vec(ln1_g[0]), vec(ln1_b[0]), w_router[0].T)

    ti = lax.broadcasted_iota(jnp.int32, (ROUTE_TILE, ROUTE_TILE), 0)
    tj = lax.broadcasted_iota(jnp.int32, (ROUTE_TILE, ROUTE_TILE), 1)
    tri = (ti < tj).astype(BF16)
    idx_t, w_t, pos_t, counts = _routing(logits_t, router_bias[0].reshape(N_EXPERTS, 1).astype(F32), tri)

    counts = counts[:, 0].astype(jnp.int32)
    padded = (counts + EXPERT_BM - 1) // EXPERT_BM * EXPERT_BM
    seg_end = jnp.cumsum(padded)
    seg_start = seg_end - padded
    expert_ids = jnp.arange(N_EXPERTS, dtype=jnp.int32)
    dest_t = pos_t + jnp.sum(
        jnp.where(idx_t[None] == expert_ids[:, None, None], seg_start[:, None, None], 0), axis=0)
    nb = (n * TOP_K) // EXPERT_BM + N_EXPERTS
    block_rows = jnp.arange(nb, dtype=jnp.int32) * EXPERT_BM
    block_e = jnp.minimum(jnp.sum((seg_end[None, :] <= block_rows[:, None]).astype(jnp.int32), axis=1),
                          N_EXPERTS - 1)
    n_used = (seg_end[-1:] // EXPERT_BM).astype(jnp.int32)

    pieces_per_block = EXPERT_BM // ZERO_ROWS
    valid_last = counts - (padded - EXPERT_BM)
    first_piece = valid_last // ZERO_ROWS
    zero_start = jnp.where(padded > 0, (seg_end - EXPERT_BM) // ZERO_ROWS + first_piece, 0)
    zero_count = jnp.where(padded > 0, pieces_per_block - first_piece, 0)
    total_pieces = nb * pieces_per_block
    zero_start = jnp.concatenate([zero_start, seg_end[-1:] // ZERO_ROWS]).astype(jnp.int32)
    zero_count = jnp.concatenate([zero_count, total_pieces - seg_end[-1:] // ZERO_ROWS]).astype(jnp.int32)

    xb = _dispatch(zero_start, zero_count, dest_t, h, nb * EXPERT_BM)
    yb = _experts(block_e, n_used, xb, w_exp_up[0], w_exp_down[0])
    shared = _shared_ffn(h, w_sh_up[0].astype(BF16), w_sh_down[0].astype(BF16))
    y_p, y_s = _combine_ln2(dest_t, w_t.T, h, shared, yb, vec(ln2_g[0]), vec(ln2_b[0]), n_prompt)

    dt = x_prompt.dtype
    y_p = y_p.reshape(x_prompt.shape)
    y_s = y_s.reshape(x_sample.shape)

    def state_out(s):
        q = s.reshape(s.shape[0], HEAD, N_HEADS, HEAD)
        return jnp.transpose(q, (0, 2, 1, 3))[None].astype(dt)

    rw_p = state_out(s_out[0:1])
    rw_s = state_out(s_out[STATE_SLOTS:])

    last_rows = jnp.concatenate([jnp.array([n_prompt - 1], jnp.int32),
                                 n_prompt + seq_s - 1 + seq_s * jnp.arange(n_seq_s, dtype=jnp.int32)])
    p_last = p[last_rows]
    shift = jnp.concatenate([p_last[:, 0:3 * D],
                             p_last[:, COL_LORA:COL_LORA + LORA_W],
                             p_last[:, COL_LORA + LORA_PAD:COL_LORA + LORA_PAD + LORA_A],
                             p_last[:, COL_LORA + 2 * LORA_PAD:]], axis=1)
    sh_p = shift[0:1].reshape(1, 1, 1, RW_COLS)
    sh_s = shift[1:].reshape(1, n_seq_s, 1, RW_COLS)

    cv_p = u[n_prompt - 2:n_prompt].reshape(1, 1, 2, CONV_DIM)
    cv_s = u[n_prompt:].reshape(n_seq_s, seq_s, CONV_DIM)[:, seq_s - 2:, :][None]

    mk_p = mem_k_p.reshape(1, 1, N_MEM, MEM_HEADS, MEM_HEAD_DIM)
    mv_p = mem_v_p.reshape(1, 1, N_MEM, MEM_HEADS, MEM_HEAD_DIM)
    return (y_p, y_s, rw_p, sh_p, cv_p, mk_p, mv_p, rw_s, sh_s, cv_s)
```

```python
import functools

import jax
import jax.numpy as jnp
from jax import lax
from jax.experimental import pallas as pl
from jax.experimental.pallas import tpu as pltpu

F32 = jnp.float32
BF16 = jnp.bfloat16
ACT = jnp.bfloat16

D = 2048
HEAD = 64
N_HEADS = D // HEAD
LORA_W = 96
LORA_A = 96
LORA_G = 256
DECAY_SCALE = 0.6065306597126334
GN_EPS = HEAD * 1e-5
CONV_DIM = D // 2
N_MEM = 256
MEM_HEADS = 4
MEM_HEAD_DIM = 256
MEM_DIM = MEM_HEADS * MEM_HEAD_DIM
N_EXPERTS = 64
N_GROUPS = 8
GROUP_SIZE = N_EXPERTS // N_GROUPS
TOPK_GROUPS = 4
TOP_K = 8
EXPERT_FF = 512
SHARED_FF = 512
ROUTED_SCALE = 2.5
LN_EPS = 1e-5
DEPTH = 1
ALPHA = (2 * DEPTH) ** 0.25
RW_COLS = 3 * D + LORA_W + LORA_A + LORA_G

LORA_PAD = 128
LORA_COLS = 2 * LORA_PAD + LORA_G
COL_RKV = 0
COL_GATE = 3 * D
COL_CONV = 6 * D
COL_Q = COL_CONV + 3 * CONV_DIM
COL_LORA = COL_Q + MEM_DIM
P_COLS = COL_LORA + LORA_COLS

CHUNK = 16
GROUP_HEADS = 4
GROUP_LANES = GROUP_HEADS * HEAD
N_LANE_GROUPS = D // GROUP_LANES
STACK = GROUP_HEADS * CHUNK
SEQ_S = 16
STATE_SLOTS = 8

TR = 128
SCAN_ROWS = STATE_SLOTS * CHUNK
EXPERT_BM = 256
VMEM_LIMIT = 56 * 1024 * 1024


def _cparams(sem):
    return pltpu.CompilerParams(dimension_semantics=sem, vmem_limit_bytes=VMEM_LIMIT)


def _sigmoid(x):
    return 1.0 / (1.0 + jnp.exp(-x))


def _dot(a, b, dims=(((1,), (0,)), ((), ()))):
    return lax.dot_general(a.astype(BF16), b.astype(BF16), dims, preferred_element_type=F32)


_NN = (((1,), (0,)), ((), ()))
_NT = (((1,), (1,)), ((), ()))
_TN = (((0,), (0,)), ((), ()))


def _split2(x):
    hi = x.astype(BF16)
    lo = (x - hi.astype(F32)).astype(BF16)
    return hi, lo


def _split3(x):
    hi = x.astype(BF16)
    r1 = x - hi.astype(F32)
    mid = r1.astype(BF16)
    lo = (r1 - mid.astype(F32)).astype(BF16)
    return hi, mid, lo


def _dot3(a, b, dims=_NN):
    ah, al = _split2(a)
    bh, bl = _split2(b)
    f = functools.partial(lax.dot_general, dimension_numbers=dims, preferred_element_type=F32)
    return f(ah, bh) + (f(ah, bl) + f(al, bh))


def _dot_exact_rhs(a, b_bf16, dims=_NN):
    hi, mid, lo = _split3(a)
    f = functools.partial(lax.dot_general, dimension_numbers=dims, preferred_element_type=F32)
    return f(hi, b_bf16) + (f(mid, b_bf16) + f(lo, b_bf16))


def _dot_exact_lhs(a_bf16, b):
    hi, mid, lo = _split3(b)
    f = functools.partial(lax.dot_general, dimension_numbers=_NN, preferred_element_type=F32)
    return f(a_bf16, hi) + (f(a_bf16, mid) + f(a_bf16, lo))


_sdot = _dot


def _mm_kernel(x_ref, w_ref, o_ref):
    o_ref[...] = _dot(x_ref[...], w_ref[...]).astype(o_ref.dtype)


def _matmul(x, w, tm, tn, name):
    m, k = x.shape
    n = w.shape[1]
    return pl.pallas_call(
        _mm_kernel,
        grid=(m // tm, n // tn),
        in_specs=[pl.BlockSpec((tm, k), lambda i, j: (i, 0)),
                  pl.BlockSpec((k, tn), lambda i, j: (0, j))],
        out_specs=pl.BlockSpec((tm, tn), lambda i, j: (i, j)),
        out_shape=jax.ShapeDtypeStruct((m, n), F32),
        compiler_params=_cparams(("parallel", "arbitrary")),
        name=name,
    )(x, w)


WT_BLOCK = 512
WT_PIECE = 128


def _wt_relayout_kernel(src_ref, wt_hbm, o_ref, buf, sems):
    step = pl.program_id(0)
    n_plain = pl.num_programs(0) - 1
    n_pieces = WT_BLOCK // WT_PIECE

    def emit(slot):
        for s in range(n_pieces):
            rows = slice(s * WT_PIECE, (s + 1) * WT_PIECE)
            o_ref[:, rows] = buf[slot, rows, :].T.astype(BF16)

    def plain_copies(p, slot):
        row0 = pl.multiple_of(src_ref[p], 8)
        return [pltpu.make_async_copy(wt_hbm.at[pl.ds(row0 + s * WT_PIECE, WT_PIECE), :],
                                      buf.at[slot, pl.ds(s * WT_PIECE, WT_PIECE), :], sems.at[slot, s])
                for s in range(n_pieces)]

    @pl.when(step == 0)
    def _():
        for cp in plain_copies(0, 0):
            cp.start()
        lo_w = 3 * D
        lo_a = lo_w + LORA_W
        lo_g = lo_a + LORA_A
        pieces = ((lo_w, 0, LORA_W), (lo_a, LORA_PAD, LORA_A), (lo_g, 2 * LORA_PAD, LORA_G))
        for _, dst, width in pieces[:2]:
            buf[1, dst + width:dst + LORA_PAD, :] = jnp.zeros((LORA_PAD - width, buf.shape[2]), F32)
        copies = [pltpu.make_async_copy(wt_hbm.at[pl.ds(src, width), :], buf.at[1, pl.ds(dst, width), :],
                                        sems.at[1, n])
                  for n, (src, dst, width) in enumerate(pieces)]
        for cp in copies:
            cp.start()
        for cp in copies:
            cp.wait()
        emit(1)

    @pl.when(step > 0)
    def _():
        p = step - 1
        slot = p % 2

        @pl.when(p + 1 < n_plain)
        def _():
            for cp in plain_copies(p + 1, 1 - slot):
                cp.start()

        for cp in plain_copies(p, slot):
            cp.wait()
        emit(slot)


def _wt_relayout(wt):
    k = wt.shape[1]
    rw_end = RW_COLS
    cv_end = rw_end + 3 * CONV_DIM
    q_end = cv_end + MEM_DIM
    src = []
    for dst0, src0, width in ((COL_RKV, 0, 3 * D), (COL_GATE, q_end, 3 * D), (COL_CONV, rw_end, 3 * CONV_DIM),
                              (COL_Q, cv_end, MEM_DIM)):
        assert dst0 == len(src) * WT_BLOCK and width % WT_BLOCK == 0
        src += [src0 + b * WT_BLOCK for b in range(width // WT_BLOCK)]
    assert len(src) * WT_BLOCK == COL_LORA and LORA_COLS == WT_BLOCK
    grid_spec = pltpu.PrefetchScalarGridSpec(
        num_scalar_prefetch=1,
        grid=(len(src) + 1,),
        in_specs=[pl.BlockSpec(memory_space=pl.ANY)],
        out_specs=pl.BlockSpec((k, WT_BLOCK), lambda j, src_rows: (0, (j + len(src)) % (len(src) + 1))),
        scratch_shapes=[pltpu.VMEM((2, WT_BLOCK, k), F32), pltpu.SemaphoreType.DMA((2, WT_BLOCK // WT_PIECE))],
    )
    return pl.pallas_call(
        _wt_relayout_kernel,
        grid_spec=grid_spec,
        out_shape=jax.ShapeDtypeStruct((k, P_COLS), BF16),
        compiler_params=_cparams(("arbitrary",)),
        name="w_in_relayout",
    )(jnp.asarray(src, jnp.int32), wt)


PROJ_TM = 512
PROJ_TN = 1536


def _in_proj_kernel(n_prompt_tiles, xp_ref, xs_ref, w_ref, o_ref, x_bf):
    i = pl.program_id(0)

    @pl.when(pl.program_id(1) == 0)
    def _():
        x_bf[...] = jnp.where(i < n_prompt_tiles, xp_ref[...], xs_ref[...]).astype(BF16)

    o_ref[...] = jnp.dot(x_bf[...], w_ref[...], preferred_element_type=F32)


def _in_proj(xp, xs, w):
    k = xp.shape[1]
    n_prompt_tiles = xp.shape[0] // PROJ_TM
    n_tiles = n_prompt_tiles + xs.shape[0] // PROJ_TM
    ncols = w.shape[1]
    return pl.pallas_call(
        functools.partial(_in_proj_kernel, n_prompt_tiles),
        grid=(n_tiles, ncols // PROJ_TN),
        in_specs=[pl.BlockSpec((PROJ_TM, k), lambda i, j: (jnp.minimum(i, n_prompt_tiles - 1), 0)),
                  pl.BlockSpec((PROJ_TM, k), lambda i, j: (jnp.maximum(i - n_prompt_tiles, 0), 0)),
                  pl.BlockSpec((k, PROJ_TN), lambda i, j: (0, j))],
        out_specs=pl.BlockSpec((PROJ_TM, PROJ_TN), lambda i, j: (i, j)),
        out_shape=jax.ShapeDtypeStruct((n_tiles * PROJ_TM, ncols), F32),
        scratch_shapes=[pltpu.VMEM((PROJ_TM, k), BF16)],
        compiler_params=_cparams(("arbitrary", "arbitrary")),
        name="in_proj",
    )(xp, xs, w)


def _head_sum(x, bd):
    parts = []
    for g in range(N_LANE_GROUPS):
        parts.append(_dot_exact_rhs(x[:, g * GROUP_LANES:(g + 1) * GROUP_LANES], bd))
    return jnp.concatenate(parts, axis=1)


def _prep_kernel(n_prompt_tiles, rkv_ref, lora_ref, c_rkv_ref, c_lora_ref, b_rkv_ref, b_lora_ref,
                 mu_rkv_ref, mu_lora_ref, w0_ref, a0_ref, kk_ref, ka_ref, rk_ref,
                 w2_ref, a2_ref, g2_ref, bd_ref,
                 r_o, k_o, v_o, kk_o, b_o, lw_o, g_o, bonus_o):
    i = pl.program_id(0)
    rows = rkv_ref.shape[0]
    row = lax.broadcasted_iota(jnp.int32, (rows, 1), 0)
    is_sample = i >= n_prompt_tiles
    seq_start = jnp.logical_and(is_sample, (row % SEQ_S) == 0)

    def mixed(x, carry_row, bnd, mu):
        prev = pltpu.roll(x, 1, 0)
        carry_row = jnp.where(i == 0, 0.0, carry_row)
        prev = jnp.where(row == 0, carry_row, prev)
        prev = jnp.where(seq_start, bnd, prev)
        return x + (prev - x) * mu

    def section(s):
        sl = slice(s * D, (s + 1) * D)
        return mixed(rkv_ref[:, sl], c_rkv_ref[7:8, sl], b_rkv_ref[:, sl], mu_rkv_ref[:, sl])

    lo = mixed(lora_ref[...], c_lora_ref[7:8, :], b_lora_ref[...], mu_lora_ref[...])
    w_lo = lo[:, 0:LORA_PAD]
    a_lo = lo[:, LORA_PAD:2 * LORA_PAD]
    g_lo = lo[:, 2 * LORA_PAD:]
    log_w = -DECAY_SCALE * _sigmoid(w0_ref[...] + _dot(jnp.tanh(w_lo), w2_ref[...]))
    a = _sigmoid(a0_ref[...] + _dot(a_lo, a2_ref[...]))
    g_o[...] = _dot(_sigmoid(g_lo), g2_ref[...]).astype(g_o.dtype)
    lw_o[...] = log_w

    bd = bd_ref[...]
    k = section(1)
    kk = k * kk_ref[...]
    ss = _head_sum(kk * kk, bd)
    kk = kk * lax.rsqrt(jnp.maximum(ss, 1e-24))
    kk_o[...] = kk
    b_o[...] = kk * a
    k = k * (1.0 + (a - 1.0) * ka_ref[...])
    k_o[...] = k
    r = section(0)
    r_o[...] = r
    v = section(2)
    v_o[...] = v
    bonus_o[...] = (_head_sum(r * k * rk_ref[...], bd) * v).astype(bonus_o.dtype)


def _rwkv_prep(p, bnd_rkv, bnd_lora, mu_rkv, mu_lora, w0, a0, k_k, k_a, r_k, w2p, a2p, g2, bd, n_prompt):
    n = p.shape[0]
    n_prompt_tiles = n_prompt // TR
    carry_blk = TR // 8
    lora_blk = COL_LORA // LORA_COLS

    def row_spec(cols, cb=0):
        return pl.BlockSpec((TR, cols), lambda i: (i, cb))

    def carry_spec(cols, cb=0):
        return pl.BlockSpec((8, cols), lambda i: (jnp.maximum(i * carry_blk - 1, 0), cb))

    def bnd_spec(cols):
        return pl.BlockSpec((TR, cols), lambda i: (jnp.maximum(i - n_prompt_tiles, 0), 0))

    def const_spec(shape):
        return pl.BlockSpec(shape, lambda i: (0,) * len(shape))

    out = jax.ShapeDtypeStruct((n, D), F32)
    out_act = jax.ShapeDtypeStruct((n, D), ACT)
    return pl.pallas_call(
        functools.partial(_prep_kernel, n_prompt_tiles),
        grid=(n // TR,),
        in_specs=[row_spec(3 * D), row_spec(LORA_COLS, lora_blk),
                  carry_spec(3 * D), carry_spec(LORA_COLS, lora_blk),
                  bnd_spec(3 * D), bnd_spec(LORA_COLS),
                  const_spec((1, 3 * D)), const_spec((1, LORA_COLS)),
                  const_spec((1, D)), const_spec((1, D)), const_spec((1, D)), const_spec((1, D)),
                  const_spec((1, D)),
                  const_spec((LORA_PAD, D)), const_spec((LORA_PAD, D)), const_spec((LORA_G, D)),
                  const_spec((GROUP_LANES, GROUP_LANES))],
        out_specs=[row_spec(D)] * 8,
        out_shape=[out] * 6 + [out_act] * 2,
        compiler_params=_cparams(("arbitrary",)),
        name="rwkv_prep",
    )(p, p, p, p, bnd_rkv, bnd_lora, mu_rkv, mu_lora, w0, a0, k_k, k_a, r_k, w2p, a2p, g2, bd)


def _scan_kernel(n_prompt_tiles, r_ref, k_ref, v_ref, kk_ref, b_ref, lw_ref, s_in_ref, y_ref, s_out_ref, s_scr):
    i = pl.program_id(0)
    is_sample = i >= n_prompt_tiles
    n_chunks = r_ref.shape[0] // CHUNK

    lane = lax.broadcasted_iota(jnp.int32, (1, GROUP_LANES), 1)
    head_masks = [(lane // HEAD == h).astype(F32) for h in range(GROUP_HEADS)]
    ri = lax.broadcasted_iota(jnp.int32, (STACK, 2 * STACK), 0)
    ci = lax.broadcasted_iota(jnp.int32, (STACK, 2 * STACK), 1)
    same_head = (ri // CHUNK) == ((ci % STACK) // CHUNK)
    strict_lower = jnp.logical_and(same_head, (ci % CHUNK) < (ri % CHUNK))
    mask_incl = jnp.logical_and(same_head, (ci % CHUNK) <= (ri % CHUNK)).astype(F32)
    mask_strict_b = jnp.logical_and(strict_lower, ci < STACK).astype(F32)
    mask_strict_k = jnp.logical_and(strict_lower, ci >= STACK).astype(F32)
    eye = (ri == ci).astype(F32)
    trow = lax.broadcasted_iota(jnp.int32, (CHUNK, 1), 0)

    def running_sum(x):
        d = 1
        while d < CHUNK:
            x = x + jnp.where(trow >= d, pltpu.roll(x, d, 0), 0.0)
            d *= 2
        return x

    rb = lax.broadcasted_iota(jnp.int32, (GROUP_LANES, GROUP_LANES), 0)
    cb = lax.broadcasted_iota(jnp.int32, (GROUP_LANES, GROUP_LANES), 1)
    block_diag = ((rb // HEAD) == (cb // HEAD)).astype(F32)

    def stack(x):
        return jnp.concatenate([x * m for m in head_masks], axis=0)

    def unstack(x):
        out = x[0:CHUNK]
        for h in range(1, GROUP_HEADS):
            out = out + x[h * CHUNK:(h + 1) * CHUNK]
        return out

    def compact(s):
        out = s[0:HEAD]
        for h in range(1, GROUP_HEADS):
            out = out + s[h * HEAD:(h + 1) * HEAD]
        return out

    @pl.when(i == 0)
    def _():
        s_out_ref[...] = jnp.zeros_like(s_out_ref)

    groups = range(N_LANE_GROUPS)
    lanes = [slice(g * GROUP_LANES, (g + 1) * GROUP_LANES) for g in groups]

    def twice(x):
        return jnp.concatenate([x, x], axis=0)

    def a0(c):
        rows = slice(c * CHUNK, (c + 1) * CHUNK)
        st = {"rows": rows}
        st["lw"] = [lw_ref[rows, lanes[g]] for g in groups]
        st["cum"] = [running_sum(st["lw"][g]) for g in groups]
        return st

    def a1(st):
        rows = st["rows"]
        lhs_s, bk_s, v_s, kr_t, p_end = [], [], [], [], []
        for g in groups:
            cum, lw = st["cum"][g], st["lw"][g]
            e_incl = jnp.exp(cum)
            e_excl = jnp.exp(cum - lw)
            e_neg = jnp.exp(-cum)
            p_end.append(e_incl[CHUNK - 1:CHUNK, :])
            r_t = r_ref[rows, lanes[g]] * e_incl
            kk_t = kk_ref[rows, lanes[g]] * e_excl
            b_t = b_ref[rows, lanes[g]] * e_neg
            k_t = k_ref[rows, lanes[g]] * e_neg
            kr_t.append(jnp.concatenate([kk_t, r_t], axis=0))
            lhs_s.append(jnp.concatenate([stack(kk_t), stack(r_t)], axis=0))
            bk_s.append(jnp.concatenate([stack(b_t), stack(k_t)], axis=0))
            v_s.append(stack(v_ref[rows, lanes[g]]))
        st.update(bk_s=bk_s, v_s=v_s, kr_t=kr_t, p_end=p_end)
        st["mn"] = [_sdot(lhs_s[g], bk_s[g], _NT) for g in groups]

    def a2(st):
        mn = st.pop("mn")
        st["m1"] = [mn[g][0:STACK] * mask_strict_b for g in groups]
        m_k = [mn[g][0:STACK] * mask_strict_k for g in groups]
        st["n_bk"] = [mn[g][STACK:] * mask_incl for g in groups]
        st["m2"] = [_sdot(st["m1"][g], twice(st["m1"][g])) for g in groups]
        st["mv"] = [_sdot(m_k[g], twice(st["v_s"][g])) for g in groups]

    def a3(st):
        st["m4"] = [_sdot(st["m2"][g], twice(st["m2"][g])) for g in groups]
        st["t_inv"] = [_sdot(eye - st["m1"][g], twice(eye + st["m2"][g])) for g in groups]

    def a4(st):
        st["m8"] = [_sdot(st["m4"][g], twice(st["m4"][g])) for g in groups]
        st["t_inv"] = [_sdot(st["t_inv"][g], twice(eye + st["m4"][g])) for g in groups]

    def a5(st):
        st["t_inv"] = [_sdot(st["t_inv"][g], twice(eye + st["m8"][g])) for g in groups]

    def b1(c, st, s_prev):
        load_state = is_sample if c > 0 else jnp.logical_or(is_sample, i == 0)
        slot = jnp.where(is_sample, c, 0)
        s0 = []
        for g in groups:
            s_loaded = jnp.concatenate([s_in_ref[slot, :, lanes[g]]] * GROUP_HEADS, axis=0) * block_diag
            s0.append(jnp.where(load_state, s_loaded, s_prev[g]))
        st["s0"] = s0
        st["gr"] = [_sdot(st["kr_t"][g], s0[g], _NT) for g in groups]

    def b2(st):
        u_s = [-_sdot(st["t_inv"][g], twice(stack(st["gr"][g][0:CHUNK]) + st["mv"][g])) for g in groups]
        st["uv"] = [jnp.concatenate([u_s[g], st["v_s"][g]], axis=0) for g in groups]

    def b3(c, st):
        slot = jnp.where(is_sample, c, 0)
        for g in groups:
            y = st["gr"][g][CHUNK:] + unstack(_sdot(st["n_bk"][g], st["uv"][g]))
            y_ref[st["rows"], lanes[g]] = y.astype(y_ref.dtype)
        s_new = []
        for g in groups:
            s_new.append(st["s0"][g] * st["p_end"][g] + _sdot(st["uv"][g], st["bk_s"][g] * st["p_end"][g], _TN))
            s_out_ref[slot, :, lanes[g]] = compact(s_new[g])
        return s_new

    s_cur = [s_scr[g] for g in groups]
    cur = a0(0)
    for lvl in (a1, a2, a3, a4, a5):
        lvl(cur)
    for c in range(n_chunks):
        last = c == n_chunks - 1
        nxt = None if last else a0(c + 1)
        b1(c, cur, s_cur)
        if not last:
            a1(nxt)
        b2(cur)
        if not last:
            a2(nxt)
        s_cur = b3(c, cur)
        if not last:
            for lvl in (a3, a4, a5):
                lvl(nxt)
        cur = nxt
    for g in groups:
        s_scr[g] = s_cur[g]


def _rwkv_scan(r, k, v, kk, b, lw, s_in, n_prompt):
    n = r.shape[0]
    n_prompt_tiles = n_prompt // SCAN_ROWS
    row_spec = pl.BlockSpec((SCAN_ROWS, D), lambda i: (i, 0))
    state_spec = pl.BlockSpec((STATE_SLOTS, HEAD, D),
                              lambda i: (jnp.maximum(i - n_prompt_tiles + 1, 0), 0, 0))
    return pl.pallas_call(
        functools.partial(_scan_kernel, n_prompt_tiles),
        grid=(n // SCAN_ROWS,),
        in_specs=[row_spec] * 6 + [state_spec],
        out_specs=[row_spec, state_spec],
        out_shape=[jax.ShapeDtypeStruct((n, D), ACT), jax.ShapeDtypeStruct(s_in.shape, F32)],
        scratch_shapes=[pltpu.VMEM((N_LANE_GROUPS, GROUP_LANES, GROUP_LANES), F32)],
        compiler_params=_cparams(("arbitrary",)),
        name="rwkv_scan",
    )(r, k, v, kk, b, lw, s_in)


def _conv_kernel(n_prompt_tiles, cb_ref, cc_ref, ch_ref, ccc_ref, cch_ref, bnd1_ref, bnd2_ref,
                 cw_ref, wout_ref, o_ref, u_ref):
    i = pl.program_id(0)
    rows = cb_ref.shape[0]
    row = lax.broadcasted_iota(jnp.int32, (rows, 1), 0)
    is_sample = i >= n_prompt_tiles
    pos = row % SEQ_S
    u = cc_ref[...] * ch_ref[...]
    u_ref[...] = u
    u_prev = jnp.where(i == 0, 0.0, ccc_ref[...] * cch_ref[...])
    prev1 = pltpu.roll(u, 1, 0)
    prev1 = jnp.where(row == 0, u_prev[7:8, :], prev1)
    prev2 = pltpu.roll(u, 2, 0)
    prev2 = jnp.where(row == 0, u_prev[6:7, :], prev2)
    prev2 = jnp.where(row == 1, u_prev[7:8, :], prev2)
    bnd1 = bnd1_ref[...]
    prev1 = jnp.where(jnp.logical_and(is_sample, pos == 0), bnd1, prev1)
    prev2 = jnp.where(jnp.logical_and(is_sample, pos == 0), bnd2_ref[...], prev2)
    prev2 = jnp.where(jnp.logical_and(is_sample, pos == 1), bnd1, prev2)
    cw = cw_ref[...]
    conv = prev2 * cw[0:1, :] + prev1 * cw[1:2, :] + u * cw[2:3, :]
    o_ref[...] = _dot(cb_ref[...] * conv, wout_ref[...]).astype(o_ref.dtype)


def _short_conv(p, bnd1, bnd2, conv_w, w_out, n_prompt):
    n = p.shape[0]
    n_prompt_tiles = n_prompt // TR
    cblk = COL_CONV // CONV_DIM
    carry_blk = TR // 8

    def row_spec(cb):
        return pl.BlockSpec((TR, CONV_DIM), lambda i: (i, cb))

    def carry_spec(cb):
        return pl.BlockSpec((8, CONV_DIM), lambda i: (jnp.maximum(i * carry_blk - 1, 0), cb))

    bnd_spec = pl.BlockSpec((TR, CONV_DIM), lambda i: (jnp.maximum(i - n_prompt_tiles, 0), 0))
    return pl.pallas_call(
        functools.partial(_conv_kernel, n_prompt_tiles),
        grid=(n // TR,),
        in_specs=[row_spec(cblk), row_spec(cblk + 1), row_spec(cblk + 2),
                  carry_spec(cblk + 1), carry_spec(cblk + 2), bnd_spec, bnd_spec,
                  pl.BlockSpec((8, CONV_DIM), lambda i: (0, 0)),
                  pl.BlockSpec((CONV_DIM, D), lambda i: (0, 0))],
        out_specs=[pl.BlockSpec((TR, D), lambda i: (i, 0)), pl.BlockSpec((TR, CONV_DIM), lambda i: (i, 0))],
        out_shape=[jax.ShapeDtypeStruct((n, D), ACT), jax.ShapeDtypeStruct((n, CONV_DIM), F32)],
        compiler_params=_cparams(("arbitrary",)),
        name="short_conv",
    )(p, p, p, p, p, bnd1, bnd2, conv_w, w_out)


def _mem_kernel(q_ref, k_ref, v_ref, wo_ref, o_ref):
    n_seq = k_ref.shape[0]
    rows = q_ref.shape[0] // n_seq
    per_seq = []
    for s_i in range(n_seq):
        q = q_ref[s_i * rows:(s_i + 1) * rows, :]
        k = k_ref[s_i]
        v = v_ref[s_i]
        outs = []
        for h in range(MEM_HEADS):
            sl = slice(h * MEM_HEAD_DIM, (h + 1) * MEM_HEAD_DIM)
            s = _dot(q[:, sl], k[:, sl], _NT) * (MEM_HEAD_DIM ** -0.5)
            s = s - jnp.max(s, axis=-1, keepdims=True)
            e = jnp.exp(s)
            pr = e / jnp.sum(e, axis=-1, keepdims=True)
            outs.append(_dot(pr, v[:, sl]))
        per_seq.append(jnp.concatenate(outs, axis=1))
    o_ref[...] = _dot(jnp.concatenate(per_seq, axis=0), wo_ref[...]).astype(o_ref.dtype)


MEM_SEQS = 4


def _mem_sample(p, mem_k, mem_v, w_o, row_start, n_seq):
    qblk = COL_Q // MEM_DIM
    rows = MEM_SEQS * SEQ_S
    rb0 = row_start // rows
    return pl.pallas_call(
        _mem_kernel,
        grid=(n_seq // MEM_SEQS,),
        in_specs=[pl.BlockSpec((rows, MEM_DIM), lambda i: (rb0 + i, qblk)),
                  pl.BlockSpec((MEM_SEQS, N_MEM, MEM_DIM), lambda i: (i, 0, 0)),
                  pl.BlockSpec((MEM_SEQS, N_MEM, MEM_DIM), lambda i: (i, 0, 0)),
                  pl.BlockSpec((MEM_DIM, D), lambda i: (0, 0))],
        out_specs=pl.BlockSpec((rows, D), lambda i: (i, 0)),
        out_shape=jax.ShapeDtypeStruct((n_seq * SEQ_S, D), ACT),
        compiler_params=_cparams(("arbitrary",)),
        name="mem_attention_sample",
    )(p, mem_k, mem_v, w_o)


MEM_TILE = 256


def _mem_prompt_kernel(n_prompt_tiles, q_ref, k_ref, v_ref, wo_ref, tail_ref, o_ref):
    i = pl.program_id(0)

    @pl.when(i < n_prompt_tiles)
    def _():
        _mem_kernel(q_ref, k_ref, v_ref, wo_ref, o_ref)

    @pl.when(i >= n_prompt_tiles)
    def _():
        o_ref[...] = tail_ref[...]


def _mem_attention(p, mem_k, mem_v, w_o, o_sample, n_prompt):
    n = p.shape[0]
    qblk = COL_Q // MEM_DIM
    n_prompt_tiles = n_prompt // MEM_TILE
    return pl.pallas_call(
        functools.partial(_mem_prompt_kernel, n_prompt_tiles),
        grid=(n // MEM_TILE,),
        in_specs=[pl.BlockSpec((MEM_TILE, MEM_DIM), lambda i: (jnp.minimum(i, n_prompt_tiles - 1), qblk)),
                  pl.BlockSpec((1, N_MEM, MEM_DIM), lambda i: (0, 0, 0)),
                  pl.BlockSpec((1, N_MEM, MEM_DIM), lambda i: (0, 0, 0)),
                  pl.BlockSpec((MEM_DIM, D), lambda i: (0, 0)),
                  pl.BlockSpec((MEM_TILE, D), lambda i: (jnp.maximum(i - n_prompt_tiles, 0), 0))],
        out_specs=pl.BlockSpec((MEM_TILE, D), lambda i: (i, 0)),
        out_shape=jax.ShapeDtypeStruct((n, D), ACT),
        compiler_params=_cparams(("arbitrary",)),
        name="mem_attention",
    )(p, mem_k, mem_v, w_o, o_sample)


def _layer_norm(z, g, b):
    mu = jnp.mean(z, axis=-1, keepdims=True)
    d = z - mu
    var = jnp.mean(d * d, axis=-1, keepdims=True)
    return d * lax.rsqrt(var + LN_EPS) * g + b


def _merge_kernel(n_prompt_tiles, xp_ref, xs_ref, ga_ref, gb_ref, gm_ref, y_ref, bonus_ref, g_ref, ocv_ref, omem_ref,
                  gng_ref, gnb_ref, bd_ref, wo_ref, l1g_ref, l1b_ref, wr_ref, h_o, lt_o):
    x = jnp.where(pl.program_id(0) < n_prompt_tiles, xp_ref[...], xs_ref[...])
    bd = bd_ref[...]
    y = y_ref[...].astype(F32)
    mean = _head_sum(y, bd) * (1.0 / HEAD)
    d = y - mean
    var = _head_sum(d * d, bd) * (1.0 / HEAD)
    yn = d * lax.rsqrt(var + GN_EPS) * gng_ref[...] + gnb_ref[...]
    o_rw = (yn + bonus_ref[...].astype(F32)) * g_ref[...].astype(F32)
    merged = (_sigmoid(ga_ref[...]) * o_rw + _sigmoid(gb_ref[...]) * ocv_ref[...].astype(F32)
              + _sigmoid(gm_ref[...]) * omem_ref[...].astype(F32))
    z = ALPHA * x + _dot(merged, wo_ref[...])
    h = _layer_norm(z, l1g_ref[...], l1b_ref[...])
    h_o[...] = h
    lt_o[...] = _dot3(wr_ref[...], h, _NT)


def _merge_ln1(xp, xs, p, y_raw, bonus, g, o_cv, o_mem, gn_g, gn_b, bd, w_o, ln_g, ln_b, w_router_t):
    n = p.shape[0]
    n_prompt_tiles = xp.shape[0] // TR
    gblk = COL_GATE // D
    row = pl.BlockSpec((TR, D), lambda i: (i, 0))
    xp_spec = pl.BlockSpec((TR, D), lambda i: (jnp.minimum(i, n_prompt_tiles - 1), 0))
    xs_spec = pl.BlockSpec((TR, D), lambda i: (jnp.maximum(i - n_prompt_tiles, 0), 0))

    def gate_spec(j):
        return pl.BlockSpec((TR, D), lambda i: (i, gblk + j))

    def const_spec(shape):
        return pl.BlockSpec(shape, lambda i: (0,) * len(shape))

    vec = const_spec((1, D))
    return pl.pallas_call(
        functools.partial(_merge_kernel, n_prompt_tiles),
        grid=(n // TR,),
        in_specs=[xp_spec, xs_spec, gate_spec(0), gate_spec(1), gate_spec(2), row, row, row, row, row,
                  vec, vec, const_spec((GROUP_LANES, GROUP_LANES)), const_spec((D, D)), vec, vec,
                  const_spec((N_EXPERTS, D))],
        out_specs=[row, pl.BlockSpec((N_EXPERTS, TR), lambda i: (0, i))],
        out_shape=[jax.ShapeDtypeStruct((n, D), F32), jax.ShapeDtypeStruct((N_EXPERTS, n), F32)],
        compiler_params=_cparams(("arbitrary",)),
        name="merge_ln1",
    )(xp, xs, p, p, p, y_raw, bonus, g, o_cv, o_mem, gn_g, gn_b, bd, w_o, ln_g, ln_b, w_router_t)


ROUTE_TILE = 256


def _routing_kernel(lt_ref, bias_ref, tri_ref, idx_o, w_o, pos_o, cnt_o, carry):
    i = pl.program_id(0)
    tile = lt_ref.shape[1]

    @pl.when(i == 0)
    def _():
        carry[...] = jnp.zeros_like(carry)

    neg_inf = -jnp.inf
    scores = _sigmoid(lt_ref[...])
    choice = scores + bias_ref[...]
    row = lax.broadcasted_iota(jnp.int32, (N_EXPERTS, tile), 0)
    rowf = row.astype(F32)
    grpf = (row // GROUP_SIZE).astype(F32)

    def group_allreduce(x, op):
        for s in (1, 2, 4):
            up = pltpu.roll(x, N_EXPERTS - s, 0)
            dn = pltpu.roll(x, s, 0)
            x = op(x, jnp.where((row & s) == 0, up, dn))
        return x

    m1 = group_allreduce(choice, jnp.maximum)
    first = group_allreduce(jnp.where(choice == m1, rowf, float(N_EXPERTS)), jnp.minimum)
    m2 = group_allreduce(jnp.where(rowf == first, neg_inf, choice), jnp.maximum)
    gscore = m1 + m2

    gsel = jnp.zeros_like(choice)
    for _ in range(TOPK_GROUPS):
        gmax = jnp.max(gscore, axis=0, keepdims=True)
        pick = jnp.min(jnp.where(gscore == gmax, grpf, float(N_GROUPS)), axis=0, keepdims=True)
        hit = grpf == pick
        gsel = jnp.where(hit, 1.0, gsel)
        gscore = jnp.where(hit, neg_inf, gscore)

    masked = jnp.where(gsel > 0.0, choice, neg_inf)
    row8 = lax.broadcasted_iota(jnp.int32, (TOP_K, tile), 0)
    idx_acc = jnp.zeros((TOP_K, tile), F32)
    w_acc = jnp.zeros((TOP_K, tile), F32)
    sel_all = jnp.zeros_like(choice)
    for kk in range(TOP_K):
        mx = jnp.max(masked, axis=0, keepdims=True)
        pick = jnp.min(jnp.where(masked == mx, rowf, float(N_EXPERTS)), axis=0, keepdims=True)
        hit = rowf == pick
        wk = jnp.sum(jnp.where(hit, scores, 0.0), axis=0, keepdims=True)
        idx_acc = jnp.where(row8 == kk, pick, idx_acc)
        w_acc = jnp.where(row8 == kk, wk, w_acc)
        sel_all = jnp.where(hit, 1.0, sel_all)
        masked = jnp.where(hit, neg_inf, masked)

    w_sum = jnp.sum(w_acc, axis=0, keepdims=True)
    w_o[...] = w_acc / w_sum * ROUTED_SCALE
    idx_o[...] = idx_acc.astype(jnp.int32)

    prefix = lax.dot_general(sel_all.astype(BF16), tri_ref[...], _NN, preferred_element_type=F32) + carry[...]
    pos_acc = jnp.zeros((TOP_K, tile), F32)
    for kk in range(TOP_K):
        hit = rowf == idx_acc[kk:kk + 1, :]
        pk = jnp.sum(jnp.where(hit, prefix, 0.0), axis=0, keepdims=True)
        pos_acc = jnp.where(row8 == kk, pk, pos_acc)
    pos_o[...] = pos_acc.astype(jnp.int32)
    carry[...] = carry[...] + jnp.sum(sel_all, axis=1, keepdims=True)
    cnt_o[...] = carry[...]


def _routing(logits_t, bias_col, tri):
    n = logits_t.shape[1]
    tile = ROUTE_TILE
    tok = pl.BlockSpec((TOP_K, tile), lambda i: (0, i))
    return pl.pallas_call(
        _routing_kernel,
        grid=(n // tile,),
        in_specs=[pl.BlockSpec((N_EXPERTS, tile), lambda i: (0, i)),
                  pl.BlockSpec((N_EXPERTS, 1), lambda i: (0, 0)),
                  pl.BlockSpec((tile, tile), lambda i: (0, 0))],
        out_specs=[tok, tok, tok, pl.BlockSpec((N_EXPERTS, 1), lambda i: (0, 0))],
        out_shape=[jax.ShapeDtypeStruct((TOP_K, n), jnp.int32), jax.ShapeDtypeStruct((TOP_K, n), F32),
                   jax.ShapeDtypeStruct((TOP_K, n), jnp.int32), jax.ShapeDtypeStruct((N_EXPERTS, 1), F32)],
        scratch_shapes=[pltpu.VMEM((N_EXPERTS, 1), F32)],
        compiler_params=_cparams(("arbitrary",)),
        name="routing",
    )(logits_t, bias_col, tri)


ZERO_ROWS = 128
PACKED = D // 2


def _pack_rows(h):
    hi = pltpu.bitcast(h[:, :PACKED].astype(BF16).astype(F32), jnp.uint32)
    lo = pltpu.bitcast(h[:, PACKED:].astype(BF16).astype(F32), jnp.uint32)
    return hi | (lo >> 16)


def _unpack_halves(w):
    return pltpu.bitcast(w & jnp.uint32(0xFFFF0000), F32), pltpu.bitcast(w << 16, F32)


def _unpack_rows(w):
    first, second = _unpack_halves(w)
    return jnp.concatenate([first.astype(BF16), second.astype(BF16)], axis=1)


def _dispatch_kernel(zs_ref, zc_ref, dest_ref, h_ref, xb_out, dest_smem, zbuf, packed, sem, idx_sem, zsem):
    i = pl.program_id(0)
    rows = h_ref.shape[0]

    @pl.when(i == 0)
    def _():
        zbuf[...] = jnp.zeros_like(zbuf)

        def zero_copy(piece):
            dst0 = pl.multiple_of(piece * ZERO_ROWS, ZERO_ROWS)
            return pltpu.make_async_copy(zbuf, xb_out.at[pl.ds(dst0, ZERO_ROWS), :], zsem)

        def per_range(e, c):
            def issue_piece(j, c2):
                zero_copy(zs_ref[e] + j).start()
                return c2
            lax.fori_loop(0, zc_ref[e], issue_piece, 0)
            return c

        def per_range_wait(e, c):
            def wait_piece(j, c2):
                zero_copy(zs_ref[e] + j).wait()
                return c2
            lax.fori_loop(0, zc_ref[e], wait_piece, 0)
            return c

        lax.fori_loop(0, N_EXPERTS + 1, per_range, 0)
        lax.fori_loop(0, N_EXPERTS + 1, per_range_wait, 0)

    cp = pltpu.make_async_copy(dest_ref, dest_smem, idx_sem)
    cp.start()
    packed[...] = _pack_rows(h_ref[...])
    cp.wait()

    def row_copy(t, k):
        return pltpu.make_async_copy(packed.at[pl.ds(t, 1), :],
                                     xb_out.at[pl.ds(dest_smem[k, t], 1), :], sem)

    def issue(t, c):
        for k in range(TOP_K):
            row_copy(t, k).start()
        return c

    def drain(t, c):
        for k in range(TOP_K):
            row_copy(t, k).wait()
        return c

    lax.fori_loop(0, rows, issue, 0)
    lax.fori_loop(0, rows, drain, 0)


def _dispatch(zero_start, zero_count, dest_t, h, n_rows):
    n = h.shape[0]
    grid_spec = pltpu.PrefetchScalarGridSpec(
        num_scalar_prefetch=2,
        grid=(n // TR,),
        in_specs=[pl.BlockSpec((TOP_K, TR), lambda i, zs, zc: (0, i)),
                  pl.BlockSpec((TR, D), lambda i, zs, zc: (i, 0))],
        out_specs=pl.BlockSpec(memory_space=pl.ANY),
        scratch_shapes=[pltpu.SMEM((TOP_K, TR), jnp.int32), pltpu.VMEM((ZERO_ROWS, PACKED), jnp.uint32),
                        pltpu.VMEM((TR, PACKED), jnp.uint32),
                        pltpu.SemaphoreType.DMA, pltpu.SemaphoreType.DMA, pltpu.SemaphoreType.DMA],
    )
    return pl.pallas_call(
        _dispatch_kernel,
        grid_spec=grid_spec,
        out_shape=jax.ShapeDtypeStruct((n_rows, PACKED), jnp.uint32),
        compiler_params=_cparams(("arbitrary",)),
        name="moe_dispatch",
    )(zero_start, zero_count, dest_t, h)


def _silu(x):
    return x * _sigmoid(x)


def _expert_kernel(be_ref, nu_ref, first_ref, slot_ref, next_ref, x_ref, wu_hbm, wd_hbm, o_ref,
                   wu_f32, wd_f32, wu_bf, wd_bf, sem_u, sem_d):
    b = pl.program_id(0)

    def weight_copies(e, slot):
        return (pltpu.make_async_copy(wu_hbm.at[e], wu_f32.at[slot], sem_u.at[slot]),
                pltpu.make_async_copy(wd_hbm.at[e], wd_f32.at[slot], sem_d.at[slot]))

    @pl.when(b == 0)
    def _():
        for cp in weight_copies(be_ref[0], 0):
            cp.start()

    @pl.when(first_ref[b] == 1)
    def _():
        slot = slot_ref[b]
        for cp in weight_copies(be_ref[b], slot):
            cp.wait()
        wu_bf[...] = wu_f32[slot].astype(BF16)
        wd_bf[...] = wd_f32[slot].astype(BF16)

        @pl.when(next_ref[b] >= 0)
        def _():
            for cp in weight_copies(next_ref[b], 1 - slot):
                cp.start()

    @pl.when(b < nu_ref[0])
    def _():
        up = _dot(_unpack_rows(x_ref[...]), wu_bf[...])
        act = _silu(up[:, :EXPERT_FF]) * up[:, EXPERT_FF:]
        o_ref[...] = _pack_rows(_dot(act, wd_bf[...]))

    @pl.when(b >= nu_ref[0])
    def _():
        o_ref[...] = jnp.zeros_like(o_ref)


def _experts(block_e, n_used, xb, w_up, w_down):
    rows = xb.shape[0]
    nb = rows // EXPERT_BM

    bidx = jnp.arange(nb, dtype=jnp.int32)
    prev_e = jnp.concatenate([jnp.full((1,), -1, jnp.int32), block_e[:-1]])
    first = jnp.logical_and(bidx < n_used[0], block_e != prev_e)
    slot = (jnp.cumsum(first.astype(jnp.int32)) - 1) % 2
    first_pos = jnp.where(first, bidx, nb)
    next_first = jnp.concatenate([jnp.flip(lax.cummin(jnp.flip(first_pos)))[1:], jnp.full((1,), nb, jnp.int32)])
    next_e = jnp.where(next_first < nb, block_e[jnp.minimum(next_first, nb - 1)], -1)

    def xmap(b, be, nu, fi, sl, ne):
        return (jnp.minimum(b, nu[0] - 1), 0)

    grid_spec = pltpu.PrefetchScalarGridSpec(
        num_scalar_prefetch=5,
        grid=(nb,),
        in_specs=[pl.BlockSpec((EXPERT_BM, PACKED), xmap),
                  pl.BlockSpec(memory_space=pl.ANY),
                  pl.BlockSpec(memory_space=pl.ANY)],
        out_specs=pl.BlockSpec((EXPERT_BM, PACKED), lambda b, be, nu, fi, sl, ne: (b, 0)),
        scratch_shapes=[pltpu.VMEM((2, D, 2 * EXPERT_FF), F32), pltpu.VMEM((2, EXPERT_FF, D), F32),
                        pltpu.VMEM((D, 2 * EXPERT_FF), BF16), pltpu.VMEM((EXPERT_FF, D), BF16),
                        pltpu.SemaphoreType.DMA((2,)), pltpu.SemaphoreType.DMA((2,))],
    )
    return pl.pallas_call(
        _expert_kernel,
        grid_spec=grid_spec,
        out_shape=jax.ShapeDtypeStruct((rows, PACKED), jnp.uint32),
        compiler_params=_cparams(("arbitrary",)),
        name="moe_experts",
    )(block_e, n_used, first.astype(jnp.int32), slot.astype(jnp.int32), next_e.astype(jnp.int32), xb, w_up, w_down)


SHARED_TILE = 512


def _shared_kernel(h_ref, wu_ref, wd_ref, o_ref):
    up = _dot(h_ref[...], wu_ref[...])
    act = _silu(up[:, :SHARED_FF]) * up[:, SHARED_FF:]
    o_ref[...] = _dot(act, wd_ref[...])


def _shared_ffn(h, w_up, w_down):
    n = h.shape[0]
    row = pl.BlockSpec((SHARED_TILE, D), lambda i: (i, 0))
    return pl.pallas_call(
        _shared_kernel,
        grid=(n // SHARED_TILE,),
        in_specs=[row, pl.BlockSpec((D, 2 * SHARED_FF), lambda i: (0, 0)),
                  pl.BlockSpec((SHARED_FF, D), lambda i: (0, 0))],
        out_specs=row,
        out_shape=jax.ShapeDtypeStruct((n, D), F32),
        compiler_params=_cparams(("parallel",)),
        name="shared_ffn",
    )(h, w_up, w_down)


def _combine_kernel(n_prompt_tiles, dest_ref, w_ref, h_ref, sh_ref, yb_ref, l2g_ref, l2b_ref, yp_o, ys_o,
                    buf, dest_smem, sem, idx_sem):
    i = pl.program_id(0)
    rows = h_ref.shape[0]
    cp = pltpu.make_async_copy(dest_ref, dest_smem, idx_sem)
    cp.start()
    cp.wait()

    def row_copy(t, k):
        return pltpu.make_async_copy(yb_ref.at[pl.ds(dest_smem[k, t], 1), :],
                                     buf.at[k, pl.ds(t, 1), :], sem)

    def issue(t, c):
        for k in range(TOP_K):
            row_copy(t, k).start()
        return c

    def drain(t, c):
        for k in range(TOP_K):
            row_copy(t, k).wait()
        return c

    lax.fori_loop(0, rows, issue, 0)
    lax.fori_loop(0, rows, drain, 0)
    w = w_ref[...]
    sh = sh_ref[...]
    f_first, f_second = sh[:, :PACKED], sh[:, PACKED:]
    for k in range(TOP_K):
        y_first, y_second = _unpack_halves(buf[k])
        f_first = f_first + w[:, k:k + 1] * y_first
        f_second = f_second + w[:, k:k + 1] * y_second
    z = ALPHA * h_ref[...] + jnp.concatenate([f_first, f_second], axis=1)
    y = _layer_norm(z, l2g_ref[...], l2b_ref[...])

    @pl.when(i < n_prompt_tiles)
    def _():
        yp_o[...] = y

    @pl.when(i >= n_prompt_tiles)
    def _():
        ys_o[...] = y


def _combine_ln2(dest_t, w_tok, h, shared, yb, ln_g, ln_b, n_prompt):
    n = h.shape[0]
    n_prompt_tiles = n_prompt // TR
    row = pl.BlockSpec((TR, D), lambda i: (i, 0))
    vec = pl.BlockSpec((1, D), lambda i: (0, 0))
    return pl.pallas_call(
        functools.partial(_combine_kernel, n_prompt_tiles),
        grid=(n // TR,),
        in_specs=[pl.BlockSpec((TOP_K, TR), lambda i: (0, i)),
                  pl.BlockSpec((TR, TOP_K), lambda i: (i, 0)),
                  row, row, pl.BlockSpec(memory_space=pl.ANY), vec, vec],
        out_specs=[pl.BlockSpec((TR, D), lambda i: (jnp.minimum(i, n_prompt_tiles - 1), 0)),
                   pl.BlockSpec((TR, D), lambda i: (jnp.maximum(i - n_prompt_tiles, 0), 0))],
        out_shape=[jax.ShapeDtypeStruct((n_prompt, D), F32), jax.ShapeDtypeStruct((n - n_prompt, D), F32)],
        scratch_shapes=[pltpu.VMEM((TOP_K, TR, PACKED), jnp.uint32), pltpu.SMEM((TOP_K, TR), jnp.int32),
                        pltpu.SemaphoreType.DMA, pltpu.SemaphoreType.DMA],
        compiler_params=_cparams(("arbitrary",)),
        name="moe_combine_ln2",
    )(dest_t, w_tok, h, shared, yb, ln_g, ln_b)


def _rw_cols_split(v):
    pad = jnp.zeros(v.shape[:-1] + (LORA_PAD - LORA_W,), v.dtype)
    lora = jnp.concatenate([v[..., 3 * D:3 * D + LORA_W], pad,
                            v[..., 3 * D + LORA_W:3 * D + LORA_W + LORA_A], pad,
                            v[..., 3 * D + LORA_W + LORA_A:]], axis=-1)
    return v[..., 0:3 * D], lora


def _pad_rows(w, rows):
    return jnp.concatenate([w, jnp.zeros((rows - w.shape[0],) + w.shape[1:], w.dtype)], axis=0)


def kernel(x_prompt, x_sample, mem_prompt, state_rwkv, state_shift, state_conv, cache_mem_k, cache_mem_v,
           w_in, mu_shift, rw_w0, rw_w2, rw_a0, rw_a2, rw_g2, rw_k_k, rw_k_a, rw_r_k, rw_gn_g, rw_gn_b,
           conv_w, w_conv_out, w_mem_k, w_mem_v, w_mem_o, w_o, ln1_g, ln1_b, w_router, router_bias,
           w_exp_up, w_exp_down, w_sh_up, w_sh_down, ln2_g, ln2_b):
    n_prompt = x_prompt.shape[0] * x_prompt.shape[1]
    n_seq_s, seq_s = x_sample.shape[0], x_sample.shape[1]
    n_sample = n_seq_s * seq_s
    n = n_prompt + n_sample
    assert x_prompt.shape[0] == 1 and seq_s == SEQ_S and n_prompt % TR == 0 and n_sample % TR == 0
    assert n % SHARED_TILE == 0 and n % ROUTE_TILE == 0 and w_in.shape[0] == 1
    assert n_prompt % PROJ_TM == 0 and n_sample % PROJ_TM == 0

    xp = x_prompt.reshape(n_prompt, D)
    xs = x_sample.reshape(n_sample, D)

    def vec(v):
        return v.reshape(1, -1).astype(F32)

    w_in_r = _wt_relayout(w_in[0].T)
    p = _in_proj(xp, xs, w_in_r)

    w_kv = jnp.concatenate([w_mem_k[0], w_mem_v[0]], axis=1).astype(BF16)
    kv = _matmul(mem_prompt[0], w_kv, N_MEM, 512, "mem_kv")
    mem_k_p, mem_v_p = kv[:, :MEM_DIM], kv[:, MEM_DIM:]

    mu_rkv, mu_lora = _rw_cols_split(vec(mu_shift[0]))
    sh_rkv, sh_lora = _rw_cols_split(state_shift[0, :, 0, :])
    bnd_rkv = jnp.repeat(sh_rkv, seq_s, axis=0)
    bnd_lora = jnp.repeat(sh_lora, seq_s, axis=0)
    hi = lax.broadcasted_iota(jnp.int32, (GROUP_LANES, GROUP_LANES), 0) // HEAD
    hj = lax.broadcasted_iota(jnp.int32, (GROUP_LANES, GROUP_LANES), 1) // HEAD
    bd = (hi == hj).astype(BF16)
    r, k, v, kk, b, lw, g, bonus = _rwkv_prep(
        p, bnd_rkv, bnd_lora, mu_rkv, mu_lora, vec(rw_w0[0]), vec(rw_a0[0]), vec(rw_k_k[0]), vec(rw_k_a[0]),
        vec(rw_r_k[0]), _pad_rows(rw_w2[0], LORA_PAD).astype(BF16), _pad_rows(rw_a2[0], LORA_PAD).astype(BF16),
        rw_g2[0].astype(BF16), bd, n_prompt)

    s_sample = jnp.transpose(state_rwkv[0], (0, 2, 1, 3)).reshape(n_seq_s, HEAD, D)
    s_in = jnp.concatenate([jnp.zeros((STATE_SLOTS, HEAD, D), F32), s_sample.astype(F32)], axis=0)
    y_raw, s_out = _rwkv_scan(r, k, v, kk, b, lw, s_in, n_prompt)

    bnd1 = jnp.repeat(state_conv[0, :, 1, :], seq_s, axis=0)
    bnd2 = jnp.repeat(state_conv[0, :, 0, :], seq_s, axis=0)
    o_cv, u = _short_conv(p, bnd1, bnd2, _pad_rows(conv_w[0], 8), w_conv_out[0].astype(BF16), n_prompt)

    w_mem_o_b = w_mem_o[0].astype(BF16)
    o_mem_s = _mem_sample(p, cache_mem_k[0].reshape(n_seq_s, N_MEM, MEM_DIM),
                          cache_mem_v[0].reshape(n_seq_s, N_MEM, MEM_DIM), w_mem_o_b, n_prompt, n_seq_s)
    o_mem = _mem_attention(p, mem_k_p[None], mem_v_p[None], w_mem_o_b, o_mem_s, n_prompt)

    h, logits_t = _merge_ln1(xp, xs, p, y_raw, bonus, g, o_cv, o_mem, vec(rw_gn_g[0]), vec(rw_gn_b[0]), bd,
                             w_o[0].astype(BF16), vec(ln1_g[0]), vec(ln1_b[0]), w_router[0].T)

    ti = lax.broadcasted_iota(jnp.int32, (ROUTE_TILE, ROUTE_TILE), 0)
    tj = lax.broadcasted_iota(jnp.int32, (ROUTE_TILE, ROUTE_TILE), 1)
    tri = (ti < tj).astype(BF16)
    idx_t, w_t, pos_t, counts = _routing(logits_t, router_bias[0].reshape(N_EXPERTS, 1).astype(F32), tri)

    counts = counts[:, 0].astype(jnp.int32)
    padded = (counts + EXPERT_BM - 1) // EXPERT_BM * EXPERT_BM
    seg_end = jnp.cumsum(padded)
    seg_start = seg_end - padded
    expert_ids = jnp.arange(N_EXPERTS, dtype=jnp.int32)
    dest_t = pos_t + jnp.sum(
        jnp.where(idx_t[None] == expert_ids[:, None, None], seg_start[:, None, None], 0), axis=0)
    nb = (n * TOP_K) // EXPERT_BM + N_EXPERTS
    block_rows = jnp.arange(nb, dtype=jnp.int32) * EXPERT_BM
    block_e = jnp.minimum(jnp.sum((seg_end[None, :] <= block_rows[:, None]).astype(jnp.int32), axis=1),
                          N_EXPERTS - 1)
    n_used = (seg_end[-1:] // EXPERT_BM).astype(jnp.int32)

    pieces_per_block = EXPERT_BM // ZERO_ROWS
    valid_last = counts - (padded - EXPERT_BM)
    first_piece = valid_last // ZERO_ROWS
    zero_start = jnp.where(padded > 0, (seg_end - EXPERT_BM) // ZERO_ROWS + first_piece, 0)
    zero_count = jnp.where(padded > 0, pieces_per_block - first_piece, 0)
    total_pieces = nb * pieces_per_block
    zero_start = jnp.concatenate([zero_start, seg_end[-1:] // ZERO_ROWS]).astype(jnp.int32)
    zero_count = jnp.concatenate([zero_count, total_pieces - seg_end[-1:] // ZERO_ROWS]).astype(jnp.int32)

    xb = _dispatch(zero_start, zero_count, dest_t, h, nb * EXPERT_BM)
    yb = _experts(block_e, n_used, xb, w_exp_up[0], w_exp_down[0])
    shared = _shared_ffn(h, w_sh_up[0].astype(BF16), w_sh_down[0].astype(BF16))
    y_p, y_s = _combine_ln2(dest_t, w_t.T, h, shared, yb, vec(ln2_g[0]), vec(ln2_b[0]), n_prompt)

    dt = x_prompt.dtype
    y_p = y_p.reshape(x_prompt.shape)
    y_s = y_s.reshape(x_sample.shape)

    def state_out(s):
        q = s.reshape(s.shape[0], HEAD, N_HEADS, HEAD)
        return jnp.transpose(q, (0, 2, 1, 3))[None].astype(dt)

    rw_p = state_out(s_out[0:1])
    rw_s = state_out(s_out[STATE_SLOTS:])

    last_rows = jnp.concatenate([jnp.array([n_prompt - 1], jnp.int32),
                                 n_prompt + seq_s - 1 + seq_s * jnp.arange(n_seq_s, dtype=jnp.int32)])
    p_last = p[last_rows]
    shift = jnp.concatenate([p_last[:, 0:3 * D],
                             p_last[:, COL_LORA:COL_LORA + LORA_W],
                             p_last[:, COL_LORA + LORA_PAD:COL_LORA + LORA_PAD + LORA_A],
                             p_last[:, COL_LORA + 2 * LORA_PAD:]], axis=1)
    sh_p = shift[0:1].reshape(1, 1, 1, RW_COLS)
    sh_s = shift[1:].reshape(1, n_seq_s, 1, RW_COLS)

    cv_p = u[n_prompt - 2:n_prompt].reshape(1, 1, 2, CONV_DIM)
    cv_s = u[n_prompt:].reshape(n_seq_s, seq_s, CONV_DIM)[:, seq_s - 2:, :][None]

    mk_p = mem_k_p.reshape(1, 1, N_MEM, MEM_HEADS, MEM_HEAD_DIM)
    mv_p = mem_v_p.reshape(1, 1, N_MEM, MEM_HEADS, MEM_HEAD_DIM)
    return (y_p, y_s, rw_p, sh_p, cv_p, mk_p, mv_p, rw_s, sh_s, cv_s)
```

```python
import functools

import jax
import jax.numpy as jnp
from jax import lax
from jax.experimental import pallas as pl
from jax.experimental.pallas import tpu as pltpu

F32 = jnp.float32
BF16 = jnp.bfloat16
ACT = jnp.bfloat16

D = 2048
HEAD = 64
N_HEADS = D // HEAD
LORA_W = 96
LORA_A = 96
LORA_G = 256
DECAY_SCALE = 0.6065306597126334
GN_EPS = HEAD * 1e-5
CONV_DIM = D // 2
N_MEM = 256
MEM_HEADS = 4
MEM_HEAD_DIM = 256
MEM_DIM = MEM_HEADS * MEM_HEAD_DIM
N_EXPERTS = 64
N_GROUPS = 8
GROUP_SIZE = N_EXPERTS // N_GROUPS
TOPK_GROUPS = 4
TOP_K = 8
EXPERT_FF = 512
SHARED_FF = 512
ROUTED_SCALE = 2.5
LN_EPS = 1e-5
DEPTH = 1
ALPHA = (2 * DEPTH) ** 0.25
RW_COLS = 3 * D + LORA_W + LORA_A + LORA_G

LORA_PAD = 128
LORA_COLS = 2 * LORA_PAD + LORA_G
COL_RKV = 0
COL_GATE = 3 * D
COL_CONV = 6 * D
COL_Q = COL_CONV + 3 * CONV_DIM
COL_LORA = COL_Q + MEM_DIM
P_COLS = COL_LORA + LORA_COLS

CHUNK = 16
GROUP_HEADS = 4
GROUP_LANES = GROUP_HEADS * HEAD
N_LANE_GROUPS = D // GROUP_LANES
STACK = GROUP_HEADS * CHUNK
SEQ_S = 16
STATE_SLOTS = 8

TR = 128
SCAN_ROWS = STATE_SLOTS * CHUNK
EXPERT_BM = 256
VMEM_LIMIT = 56 * 1024 * 1024


def _cparams(sem):
    return pltpu.CompilerParams(dimension_semantics=sem, vmem_limit_bytes=VMEM_LIMIT)


def _sigmoid(x):
    return 1.0 / (1.0 + jnp.exp(-x))


def _dot(a, b, dims=(((1,), (0,)), ((), ()))):
    return lax.dot_general(a.astype(BF16), b.astype(BF16), dims, preferred_element_type=F32)


_NN = (((1,), (0,)), ((), ()))
_NT = (((1,), (1,)), ((), ()))
_TN = (((0,), (0,)), ((), ()))


def _split2(x):
    hi = x.astype(BF16)
    lo = (x - hi.astype(F32)).astype(BF16)
    return hi, lo


def _split3(x):
    hi = x.astype(BF16)
    r1 = x - hi.astype(F32)
    mid = r1.astype(BF16)
    lo = (r1 - mid.astype(F32)).astype(BF16)
    return hi, mid, lo


def _dot3(a, b, dims=_NN):
    ah, al = _split2(a)
    bh, bl = _split2(b)
    f = functools.partial(lax.dot_general, dimension_numbers=dims, preferred_element_type=F32)
    return f(ah, bh) + (f(ah, bl) + f(al, bh))


def _dot_exact_rhs(a, b_bf16, dims=_NN):
    hi, mid, lo = _split3(a)
    f = functools.partial(lax.dot_general, dimension_numbers=dims, preferred_element_type=F32)
    return f(hi, b_bf16) + (f(mid, b_bf16) + f(lo, b_bf16))


def _dot_exact_lhs(a_bf16, b):
    hi, mid, lo = _split3(b)
    f = functools.partial(lax.dot_general, dimension_numbers=_NN, preferred_element_type=F32)
    return f(a_bf16, hi) + (f(a_bf16, mid) + f(a_bf16, lo))


_sdot = _dot


def _mm_kernel(x_ref, w_ref, o_ref):
    o_ref[...] = _dot(x_ref[...], w_ref[...]).astype(o_ref.dtype)


def _matmul(x, w, tm, tn, name):
    m, k = x.shape
    n = w.shape[1]
    return pl.pallas_call(
        _mm_kernel,
        grid=(m // tm, n // tn),
        in_specs=[pl.BlockSpec((tm, k), lambda i, j: (i, 0)),
                  pl.BlockSpec((k, tn), lambda i, j: (0, j))],
        out_specs=pl.BlockSpec((tm, tn), lambda i, j: (i, j)),
        out_shape=jax.ShapeDtypeStruct((m, n), F32),
        compiler_params=_cparams(("parallel", "arbitrary")),
        name=name,
    )(x, w)


WT_BLOCK = 512
WT_PIECE = 128


def _wt_relayout_kernel(src_ref, wt_hbm, o_ref, buf, sems):
    step = pl.program_id(0)
    n_plain = pl.num_programs(0) - 1
    n_pieces = WT_BLOCK // WT_PIECE

    def emit(slot):
        for s in range(n_pieces):
            rows = slice(s * WT_PIECE, (s + 1) * WT_PIECE)
            o_ref[:, rows] = buf[slot, rows, :].T.astype(BF16)

    def plain_copies(p, slot):
        row0 = pl.multiple_of(src_ref[p], 8)
        return [pltpu.make_async_copy(wt_hbm.at[pl.ds(row0 + s * WT_PIECE, WT_PIECE), :],
                                      buf.at[slot, pl.ds(s * WT_PIECE, WT_PIECE), :], sems.at[slot, s])
                for s in range(n_pieces)]

    @pl.when(step == 0)
    def _():
        for cp in plain_copies(0, 0):
            cp.start()
        lo_w = 3 * D
        lo_a = lo_w + LORA_W
        lo_g = lo_a + LORA_A
        pieces = ((lo_w, 0, LORA_W), (lo_a, LORA_PAD, LORA_A), (lo_g, 2 * LORA_PAD, LORA_G))
        for _, dst, width in pieces[:2]:
            buf[1, dst + width:dst + LORA_PAD, :] = jnp.zeros((LORA_PAD - width, buf.shape[2]), F32)
        copies = [pltpu.make_async_copy(wt_hbm.at[pl.ds(src, width), :], buf.at[1, pl.ds(dst, width), :],
                                        sems.at[1, n])
                  for n, (src, dst, width) in enumerate(pieces)]
        for cp in copies:
            cp.start()
        for cp in copies:
            cp.wait()
        emit(1)

    @pl.when(step > 0)
    def _():
        p = step - 1
        slot = p % 2

        @pl.when(p + 1 < n_plain)
        def _():
            for cp in plain_copies(p + 1, 1 - slot):
                cp.start()

        for cp in plain_copies(p, slot):
            cp.wait()
        emit(slot)


def _wt_relayout(wt):
    k = wt.shape[1]
    rw_end = RW_COLS
    cv_end = rw_end + 3 * CONV_DIM
    q_end = cv_end + MEM_DIM
    src = []
    for dst0, src0, width in ((COL_RKV, 0, 3 * D), (COL_GATE, q_end, 3 * D), (COL_CONV, rw_end, 3 * CONV_DIM),
                              (COL_Q, cv_end, MEM_DIM)):
        assert dst0 == len(src) * WT_BLOCK and width % WT_BLOCK == 0
        src += [src0 + b * WT_BLOCK for b in range(width // WT_BLOCK)]
    assert len(src) * WT_BLOCK == COL_LORA and LORA_COLS == WT_BLOCK
    grid_spec = pltpu.PrefetchScalarGridSpec(
        num_scalar_prefetch=1,
        grid=(len(src) + 1,),
        in_specs=[pl.BlockSpec(memory_space=pl.ANY)],
        out_specs=pl.BlockSpec((k, WT_BLOCK), lambda j, src_rows: (0, (j + len(src)) % (len(src) + 1))),
        scratch_shapes=[pltpu.VMEM((2, WT_BLOCK, k), F32), pltpu.SemaphoreType.DMA((2, WT_BLOCK // WT_PIECE))],
    )
    return pl.pallas_call(
        _wt_relayout_kernel,
        grid_spec=grid_spec,
        out_shape=jax.ShapeDtypeStruct((k, P_COLS), BF16),
        compiler_params=_cparams(("arbitrary",)),
        name="w_in_relayout",
    )(jnp.asarray(src, jnp.int32), wt)


PROJ_TM = 512
PROJ_TN = 1536


def _in_proj_kernel(n_prompt_tiles, xp_ref, xs_ref, w_ref, o_ref, x_bf):
    i = pl.program_id(0)

    @pl.when(pl.program_id(1) == 0)
    def _():
        x_bf[...] = jnp.where(i < n_prompt_tiles, xp_ref[...], xs_ref[...]).astype(BF16)

    o_ref[...] = jnp.dot(x_bf[...], w_ref[...], preferred_element_type=F32)


def _in_proj(xp, xs, w):
    k = xp.shape[1]
    n_prompt_tiles = xp.shape[0] // PROJ_TM
    n_tiles = n_prompt_tiles + xs.shape[0] // PROJ_TM
    ncols = w.shape[1]
    return pl.pallas_call(
        functools.partial(_in_proj_kernel, n_prompt_tiles),
        grid=(n_tiles, ncols // PROJ_TN),
        in_specs=[pl.BlockSpec((PROJ_TM, k), lambda i, j: (jnp.minimum(i, n_prompt_tiles - 1), 0)),
                  pl.BlockSpec((PROJ_TM, k), lambda i, j: (jnp.maximum(i - n_prompt_tiles, 0), 0)),
                  pl.BlockSpec((k, PROJ_TN), lambda i, j: (0, j))],
        out_specs=pl.BlockSpec((PROJ_TM, PROJ_TN), lambda i, j: (i, j)),
        out_shape=jax.ShapeDtypeStruct((n_tiles * PROJ_TM, ncols), F32),
        scratch_shapes=[pltpu.VMEM((PROJ_TM, k), BF16)],
        compiler_params=_cparams(("arbitrary", "arbitrary")),
        name="in_proj",
    )(xp, xs, w)


def _head_sum(x, bd):
    parts = []
    for g in range(N_LANE_GROUPS):
        parts.append(_dot_exact_rhs(x[:, g * GROUP_LANES:(g + 1) * GROUP_LANES], bd))
    return jnp.concatenate(parts, axis=1)


def _prep_kernel(n_prompt_tiles, rkv_ref, lora_ref, c_rkv_ref, c_lora_ref, b_rkv_ref, b_lora_ref,
                 mu_rkv_ref, mu_lora_ref, w0_ref, a0_ref, kk_ref, ka_ref, rk_ref,
                 w2_ref, a2_ref, g2_ref, bd_ref,
                 r_o, k_o, v_o, kk_o, b_o, lw_o, g_o, bonus_o):
    i = pl.program_id(0)
    rows = rkv_ref.shape[0]
    row = lax.broadcasted_iota(jnp.int32, (rows, 1), 0)
    is_sample = i >= n_prompt_tiles
    seq_start = jnp.logical_and(is_sample, (row % SEQ_S) == 0)

    def mixed(x, carry_row, bnd, mu):
        prev = pltpu.roll(x, 1, 0)
        carry_row = jnp.where(i == 0, 0.0, carry_row)
        prev = jnp.where(row == 0, carry_row, prev)
        prev = jnp.where(seq_start, bnd, prev)
        return x + (prev - x) * mu

    def section(s):
        sl = slice(s * D, (s + 1) * D)
        return mixed(rkv_ref[:, sl], c_rkv_ref[7:8, sl], b_rkv_ref[:, sl], mu_rkv_ref[:, sl])

    lo = mixed(lora_ref[...], c_lora_ref[7:8, :], b_lora_ref[...], mu_lora_ref[...])
    w_lo = lo[:, 0:LORA_PAD]
    a_lo = lo[:, LORA_PAD:2 * LORA_PAD]
    g_lo = lo[:, 2 * LORA_PAD:]
    log_w = -DECAY_SCALE * _sigmoid(w0_ref[...] + _dot(jnp.tanh(w_lo), w2_ref[...]))
    a = _sigmoid(a0_ref[...] + _dot(a_lo, a2_ref[...]))
    g_o[...] = _dot(_sigmoid(g_lo), g2_ref[...]).astype(g_o.dtype)
    lw_o[...] = log_w

    bd = bd_ref[...]
    k = section(1)
    kk = k * kk_ref[...]
    ss = _head_sum(kk * kk, bd)
    kk = kk * lax.rsqrt(jnp.maximum(ss, 1e-24))
    kk_o[...] = kk
    b_o[...] = kk * a
    k = k * (1.0 + (a - 1.0) * ka_ref[...])
    k_o[...] = k
    r = section(0)
    r_o[...] = r
    v = section(2)
    v_o[...] = v.astype(v_o.dtype)
    bonus_o[...] = (_head_sum(r * k * rk_ref[...], bd) * v).astype(bonus_o.dtype)


def _rwkv_prep(p, bnd_rkv, bnd_lora, mu_rkv, mu_lora, w0, a0, k_k, k_a, r_k, w2p, a2p, g2, bd, n_prompt):
    n = p.shape[0]
    n_prompt_tiles = n_prompt // TR
    carry_blk = TR // 8
    lora_blk = COL_LORA // LORA_COLS

    def row_spec(cols, cb=0):
        return pl.BlockSpec((TR, cols), lambda i: (i, cb))

    def carry_spec(cols, cb=0):
        return pl.BlockSpec((8, cols), lambda i: (jnp.maximum(i * carry_blk - 1, 0), cb))

    def bnd_spec(cols):
        return pl.BlockSpec((TR, cols), lambda i: (jnp.maximum(i - n_prompt_tiles, 0), 0))

    def const_spec(shape):
        return pl.BlockSpec(shape, lambda i: (0,) * len(shape))

    out = jax.ShapeDtypeStruct((n, D), F32)
    out_act = jax.ShapeDtypeStruct((n, D), ACT)
    return pl.pallas_call(
        functools.partial(_prep_kernel, n_prompt_tiles),
        grid=(n // TR,),
        in_specs=[row_spec(3 * D), row_spec(LORA_COLS, lora_blk),
                  carry_spec(3 * D), carry_spec(LORA_COLS, lora_blk),
                  bnd_spec(3 * D), bnd_spec(LORA_COLS),
                  const_spec((1, 3 * D)), const_spec((1, LORA_COLS)),
                  const_spec((1, D)), const_spec((1, D)), const_spec((1, D)), const_spec((1, D)),
                  const_spec((1, D)),
                  const_spec((LORA_PAD, D)), const_spec((LORA_PAD, D)), const_spec((LORA_G, D)),
                  const_spec((GROUP_LANES, GROUP_LANES))],
        out_specs=[row_spec(D)] * 8,
        out_shape=[out, out, out_act, out, out, out, out_act, out_act],
        compiler_params=_cparams(("arbitrary",)),
        name="rwkv_prep",
    )(p, p, p, p, bnd_rkv, bnd_lora, mu_rkv, mu_lora, w0, a0, k_k, k_a, r_k, w2p, a2p, g2, bd)


def _scan_kernel(n_prompt_tiles, r_ref, k_ref, v_ref, kk_ref, b_ref, lw_ref, s_in_ref, y_ref, s_out_ref, s_scr):
    i = pl.program_id(0)
    is_sample = i >= n_prompt_tiles
    n_chunks = r_ref.shape[0] // CHUNK

    lane = lax.broadcasted_iota(jnp.int32, (1, GROUP_LANES), 1)
    head_masks = [(lane // HEAD == h).astype(F32) for h in range(GROUP_HEADS)]
    ri = lax.broadcasted_iota(jnp.int32, (STACK, 2 * STACK), 0)
    ci = lax.broadcasted_iota(jnp.int32, (STACK, 2 * STACK), 1)
    same_head = (ri // CHUNK) == ((ci % STACK) // CHUNK)
    strict_lower = jnp.logical_and(same_head, (ci % CHUNK) < (ri % CHUNK))
    mask_incl = jnp.logical_and(same_head, (ci % CHUNK) <= (ri % CHUNK)).astype(F32)
    mask_strict_b = jnp.logical_and(strict_lower, ci < STACK).astype(F32)
    mask_strict_k = jnp.logical_and(strict_lower, ci >= STACK).astype(F32)
    eye = (ri == ci).astype(F32)
    trow = lax.broadcasted_iota(jnp.int32, (CHUNK, 1), 0)

    def running_sum(x):
        d = 1
        while d < CHUNK:
            x = x + jnp.where(trow >= d, pltpu.roll(x, d, 0), 0.0)
            d *= 2
        return x

    rb = lax.broadcasted_iota(jnp.int32, (GROUP_LANES, GROUP_LANES), 0)
    cb = lax.broadcasted_iota(jnp.int32, (GROUP_LANES, GROUP_LANES), 1)
    block_diag = ((rb // HEAD) == (cb // HEAD)).astype(F32)

    def stack(x):
        return jnp.concatenate([x * m for m in head_masks], axis=0)

    def unstack(x):
        out = x[0:CHUNK]
        for h in range(1, GROUP_HEADS):
            out = out + x[h * CHUNK:(h + 1) * CHUNK]
        return out

    def compact(s):
        out = s[0:HEAD]
        for h in range(1, GROUP_HEADS):
            out = out + s[h * HEAD:(h + 1) * HEAD]
        return out

    @pl.when(i == 0)
    def _():
        s_out_ref[...] = jnp.zeros_like(s_out_ref)

    groups = range(N_LANE_GROUPS)
    lanes = [slice(g * GROUP_LANES, (g + 1) * GROUP_LANES) for g in groups]

    def twice(x):
        return jnp.concatenate([x, x], axis=0)

    def a0(c):
        rows = slice(c * CHUNK, (c + 1) * CHUNK)
        st = {"rows": rows}
        st["lw"] = [lw_ref[rows, lanes[g]] for g in groups]
        st["cum"] = [running_sum(st["lw"][g]) for g in groups]
        return st

    def a1(st):
        rows = st["rows"]
        lhs_s, bk_s, v_s, kr_t, p_end = [], [], [], [], []
        for g in groups:
            cum, lw = st["cum"][g], st["lw"][g]
            e_incl = jnp.exp(cum)
            e_excl = jnp.exp(cum - lw)
            e_neg = jnp.exp(-cum)
            p_end.append(e_incl[CHUNK - 1:CHUNK, :])
            r_t = r_ref[rows, lanes[g]] * e_incl
            kk_t = kk_ref[rows, lanes[g]] * e_excl
            b_t = b_ref[rows, lanes[g]] * e_neg
            k_t = k_ref[rows, lanes[g]] * e_neg
            kr_t.append(jnp.concatenate([kk_t, r_t], axis=0))
            lhs_s.append(jnp.concatenate([stack(kk_t), stack(r_t)], axis=0))
            bk_s.append(jnp.concatenate([stack(b_t), stack(k_t)], axis=0))
            v_s.append(stack(v_ref[rows, lanes[g]]))
        st.update(bk_s=bk_s, v_s=v_s, kr_t=kr_t, p_end=p_end)
        st["mn"] = [_sdot(lhs_s[g], bk_s[g], _NT) for g in groups]

    def a2(st):
        mn = st.pop("mn")
        st["m1"] = [mn[g][0:STACK] * mask_strict_b for g in groups]
        m_k = [mn[g][0:STACK] * mask_strict_k for g in groups]
        st["n_bk"] = [mn[g][STACK:] * mask_incl for g in groups]
        st["m2"] = [_sdot(st["m1"][g], twice(st["m1"][g])) for g in groups]
        st["mv"] = [_sdot(m_k[g], twice(st["v_s"][g])) for g in groups]

    def a3(st):
        st["m4"] = [_sdot(st["m2"][g], twice(st["m2"][g])) for g in groups]
        st["t_inv"] = [_sdot(eye - st["m1"][g], twice(eye + st["m2"][g])) for g in groups]

    def a4(st):
        st["m8"] = [_sdot(st["m4"][g], twice(st["m4"][g])) for g in groups]
        st["t_inv"] = [_sdot(st["t_inv"][g], twice(eye + st["m4"][g])) for g in groups]

    def a5(st):
        st["t_inv"] = [_sdot(st["t_inv"][g], twice(eye + st["m8"][g])) for g in groups]

    def b1(c, st, s_prev):
        load_state = is_sample if c > 0 else jnp.logical_or(is_sample, i == 0)
        slot = jnp.where(is_sample, c, 0)
        s0 = []
        for g in groups:
            s_loaded = jnp.concatenate([s_in_ref[slot, :, lanes[g]]] * GROUP_HEADS, axis=0) * block_diag
            s0.append(jnp.where(load_state, s_loaded, s_prev[g]))
        st["s0"] = s0
        st["gr"] = [_sdot(st["kr_t"][g], s0[g], _NT) for g in groups]

    def b2(st):
        u_s = [-_sdot(st["t_inv"][g], twice(stack(st["gr"][g][0:CHUNK]) + st["mv"][g])) for g in groups]
        st["uv"] = [jnp.concatenate([u_s[g], st["v_s"][g]], axis=0) for g in groups]

    def b3(c, st):
        slot = jnp.where(is_sample, c, 0)
        for g in groups:
            y = st["gr"][g][CHUNK:] + unstack(_sdot(st["n_bk"][g], st["uv"][g]))
            y_ref[st["rows"], lanes[g]] = y.astype(y_ref.dtype)
        s_new = []
        for g in groups:
            s_new.append(st["s0"][g] * st["p_end"][g] + _sdot(st["uv"][g], st["bk_s"][g] * st["p_end"][g], _TN))
            s_out_ref[slot, :, lanes[g]] = compact(s_new[g])
        return s_new

    s_cur = [s_scr[g] for g in groups]
    cur = a0(0)
    for lvl in (a1, a2, a3, a4, a5):
        lvl(cur)
    for c in range(n_chunks):
        last = c == n_chunks - 1
        nxt = None if last else a0(c + 1)
        b1(c, cur, s_cur)
        if not last:
            a1(nxt)
        b2(cur)
        if not last:
            a2(nxt)
        s_cur = b3(c, cur)
        if not last:
            for lvl in (a3, a4, a5):
                lvl(nxt)
        cur = nxt
    for g in groups:
        s_scr[g] = s_cur[g]


def _rwkv_scan(r, k, v, kk, b, lw, s_in, n_prompt):
    n = r.shape[0]
    n_prompt_tiles = n_prompt // SCAN_ROWS
    row_spec = pl.BlockSpec((SCAN_ROWS, D), lambda i: (i, 0))
    state_spec = pl.BlockSpec((STATE_SLOTS, HEAD, D),
                              lambda i: (jnp.maximum(i - n_prompt_tiles + 1, 0), 0, 0))
    return pl.pallas_call(
        functools.partial(_scan_kernel, n_prompt_tiles),
        grid=(n // SCAN_ROWS,),
        in_specs=[row_spec] * 6 + [state_spec],
        out_specs=[row_spec, state_spec],
        out_shape=[jax.ShapeDtypeStruct((n, D), ACT), jax.ShapeDtypeStruct(s_in.shape, F32)],
        scratch_shapes=[pltpu.VMEM((N_LANE_GROUPS, GROUP_LANES, GROUP_LANES), F32)],
        compiler_params=_cparams(("arbitrary",)),
        name="rwkv_scan",
    )(r, k, v, kk, b, lw, s_in)


def _conv_kernel(n_prompt_tiles, cb_ref, cc_ref, ch_ref, ccc_ref, cch_ref, bnd1_ref, bnd2_ref,
                 cw_ref, wout_ref, o_ref, u_ref):
    i = pl.program_id(0)
    rows = cb_ref.shape[0]
    row = lax.broadcasted_iota(jnp.int32, (rows, 1), 0)
    is_sample = i >= n_prompt_tiles
    pos = row % SEQ_S
    u = cc_ref[...] * ch_ref[...]
    u_ref[...] = u
    u_prev = jnp.where(i == 0, 0.0, ccc_ref[...] * cch_ref[...])
    prev1 = pltpu.roll(u, 1, 0)
    prev1 = jnp.where(row == 0, u_prev[7:8, :], prev1)
    prev2 = pltpu.roll(u, 2, 0)
    prev2 = jnp.where(row == 0, u_prev[6:7, :], prev2)
    prev2 = jnp.where(row == 1, u_prev[7:8, :], prev2)
    bnd1 = bnd1_ref[...]
    prev1 = jnp.where(jnp.logical_and(is_sample, pos == 0), bnd1, prev1)
    prev2 = jnp.where(jnp.logical_and(is_sample, pos == 0), bnd2_ref[...], prev2)
    prev2 = jnp.where(jnp.logical_and(is_sample, pos == 1), bnd1, prev2)
    cw = cw_ref[...]
    conv = prev2 * cw[0:1, :] + prev1 * cw[1:2, :] + u * cw[2:3, :]
    o_ref[...] = _dot(cb_ref[...] * conv, wout_ref[...]).astype(o_ref.dtype)


def _short_conv(p, bnd1, bnd2, conv_w, w_out, n_prompt):
    n = p.shape[0]
    n_prompt_tiles = n_prompt // TR
    cblk = COL_CONV // CONV_DIM
    carry_blk = TR // 8

    def row_spec(cb):
        return pl.BlockSpec((TR, CONV_DIM), lambda i: (i, cb))

    def carry_spec(cb):
        return pl.BlockSpec((8, CONV_DIM), lambda i: (jnp.maximum(i * carry_blk - 1, 0), cb))

    bnd_spec = pl.BlockSpec((TR, CONV_DIM), lambda i: (jnp.maximum(i - n_prompt_tiles, 0), 0))
    return pl.pallas_call(
        functools.partial(_conv_kernel, n_prompt_tiles),
        grid=(n // TR,),
        in_specs=[row_spec(cblk), row_spec(cblk + 1), row_spec(cblk + 2),
                  carry_spec(cblk + 1), carry_spec(cblk + 2), bnd_spec, bnd_spec,
                  pl.BlockSpec((8, CONV_DIM), lambda i: (0, 0)),
                  pl.BlockSpec((CONV_DIM, D), lambda i: (0, 0))],
        out_specs=[pl.BlockSpec((TR, D), lambda i: (i, 0)), pl.BlockSpec((TR, CONV_DIM), lambda i: (i, 0))],
        out_shape=[jax.ShapeDtypeStruct((n, D), ACT), jax.ShapeDtypeStruct((n, CONV_DIM), F32)],
        compiler_params=_cparams(("arbitrary",)),
        name="short_conv",
    )(p, p, p, p, p, bnd1, bnd2, conv_w, w_out)


def _mem_kernel(q_ref, k_ref, v_ref, wo_ref, o_ref):
    n_seq = k_ref.shape[0]
    rows = q_ref.shape[0] // n_seq
    per_seq = []
    for s_i in range(n_seq):
        q = q_ref[s_i * rows:(s_i + 1) * rows, :]
        k = k_ref[s_i]
        v = v_ref[s_i]
        outs = []
        for h in range(MEM_HEADS):
            sl = slice(h * MEM_HEAD_DIM, (h + 1) * MEM_HEAD_DIM)
            s = _dot(q[:, sl], k[:, sl], _NT) * (MEM_HEAD_DIM ** -0.5)
            s = s - jnp.max(s, axis=-1, keepdims=True)
            e = jnp.exp(s)
            pr = e / jnp.sum(e, axis=-1, keepdims=True)
            outs.append(_dot(pr, v[:, sl]))
        per_seq.append(jnp.concatenate(outs, axis=1))
    o_ref[...] = _dot(jnp.concatenate(per_seq, axis=0), wo_ref[...]).astype(o_ref.dtype)


MEM_SEQS = 4


def _mem_sample(p, mem_k, mem_v, w_o, row_start, n_seq):
    qblk = COL_Q // MEM_DIM
    rows = MEM_SEQS * SEQ_S
    rb0 = row_start // rows
    return pl.pallas_call(
        _mem_kernel,
        grid=(n_seq // MEM_SEQS,),
        in_specs=[pl.BlockSpec((rows, MEM_DIM), lambda i: (rb0 + i, qblk)),
                  pl.BlockSpec((MEM_SEQS, N_MEM, MEM_DIM), lambda i: (i, 0, 0)),
                  pl.BlockSpec((MEM_SEQS, N_MEM, MEM_DIM), lambda i: (i, 0, 0)),
                  pl.BlockSpec((MEM_DIM, D), lambda i: (0, 0))],
        out_specs=pl.BlockSpec((rows, D), lambda i: (i, 0)),
        out_shape=jax.ShapeDtypeStruct((n_seq * SEQ_S, D), ACT),
        compiler_params=_cparams(("arbitrary",)),
        name="mem_attention_sample",
    )(p, mem_k, mem_v, w_o)


MEM_TILE = 256


def _mem_prompt_kernel(n_prompt_tiles, q_ref, k_ref, v_ref, wo_ref, tail_ref, o_ref):
    i = pl.program_id(0)

    @pl.when(i < n_prompt_tiles)
    def _():
        _mem_kernel(q_ref, k_ref, v_ref, wo_ref, o_ref)

    @pl.when(i >= n_prompt_tiles)
    def _():
        o_ref[...] = tail_ref[...]


def _mem_attention(p, mem_k, mem_v, w_o, o_sample, n_prompt):
    n = p.shape[0]
    qblk = COL_Q // MEM_DIM
    n_prompt_tiles = n_prompt // MEM_TILE
    return pl.pallas_call(
        functools.partial(_mem_prompt_kernel, n_prompt_tiles),
        grid=(n // MEM_TILE,),
        in_specs=[pl.BlockSpec((MEM_TILE, MEM_DIM), lambda i: (jnp.minimum(i, n_prompt_tiles - 1), qblk)),
                  pl.BlockSpec((1, N_MEM, MEM_DIM), lambda i: (0, 0, 0)),
                  pl.BlockSpec((1, N_MEM, MEM_DIM), lambda i: (0, 0, 0)),
                  pl.BlockSpec((MEM_DIM, D), lambda i: (0, 0)),
                  pl.BlockSpec((MEM_TILE, D), lambda i: (jnp.maximum(i - n_prompt_tiles, 0), 0))],
        out_specs=pl.BlockSpec((MEM_TILE, D), lambda i: (i, 0)),
        out_shape=jax.ShapeDtypeStruct((n, D), ACT),
        compiler_params=_cparams(("arbitrary",)),
        name="mem_attention",
    )(p, mem_k, mem_v, w_o, o_sample)


def _layer_norm(z, g, b):
    mu = jnp.mean(z, axis=-1, keepdims=True)
    d = z - mu
    var = jnp.mean(d * d, axis=-1, keepdims=True)
    return d * lax.rsqrt(var + LN_EPS) * g + b


def _merge_kernel(n_prompt_tiles, xp_ref, xs_ref, ga_ref, gb_ref, gm_ref, y_ref, bonus_ref, g_ref, ocv_ref, omem_ref,
                  gng_ref, gnb_ref, bd_ref, wo_ref, l1g_ref, l1b_ref, wr_ref, h_o, lt_o):
    x = jnp.where(pl.program_id(0) < n_prompt_tiles, xp_ref[...], xs_ref[...])
    bd = bd_ref[...]
    y = y_ref[...].astype(F32)
    mean = _head_sum(y, bd) * (1.0 / HEAD)
    d = y - mean
    var = _head_sum(d * d, bd) * (1.0 / HEAD)
    yn = d * lax.rsqrt(var + GN_EPS) * gng_ref[...] + gnb_ref[...]
    o_rw = (yn + bonus_ref[...].astype(F32)) * g_ref[...].astype(F32)
    merged = (_sigmoid(ga_ref[...]) * o_rw + _sigmoid(gb_ref[...]) * ocv_ref[...].astype(F32)
              + _sigmoid(gm_ref[...]) * omem_ref[...].astype(F32))
    z = ALPHA * x + _dot(merged, wo_ref[...])
    h = _layer_norm(z, l1g_ref[...], l1b_ref[...])
    h_o[...] = h
    lt_o[...] = _dot3(wr_ref[...], h, _NT)


MERGE_TILE = 256


def _merge_ln1(xp, xs, p, y_raw, bonus, g, o_cv, o_mem, gn_g, gn_b, bd, w_o, ln_g, ln_b, w_router_t):
    n = p.shape[0]
    tile = MERGE_TILE
    n_prompt_tiles = xp.shape[0] // tile
    gblk = COL_GATE // D
    row = pl.BlockSpec((tile, D), lambda i: (i, 0))
    xp_spec = pl.BlockSpec((tile, D), lambda i: (jnp.minimum(i, n_prompt_tiles - 1), 0))
    xs_spec = pl.BlockSpec((tile, D), lambda i: (jnp.maximum(i - n_prompt_tiles, 0), 0))

    def gate_spec(j):
        return pl.BlockSpec((tile, D), lambda i: (i, gblk + j))

    def const_spec(shape):
        return pl.BlockSpec(shape, lambda i: (0,) * len(shape))

    vec = const_spec((1, D))
    return pl.pallas_call(
        functools.partial(_merge_kernel, n_prompt_tiles),
        grid=(n // tile,),
        in_specs=[xp_spec, xs_spec, gate_spec(0), gate_spec(1), gate_spec(2), row, row, row, row, row,
                  vec, vec, const_spec((GROUP_LANES, GROUP_LANES)), const_spec((D, D)), vec, vec,
                  const_spec((N_EXPERTS, D))],
        out_specs=[row, pl.BlockSpec((N_EXPERTS, tile), lambda i: (0, i))],
        out_shape=[jax.ShapeDtypeStruct((n, D), F32), jax.ShapeDtypeStruct((N_EXPERTS, n), F32)],
        compiler_params=_cparams(("arbitrary",)),
        name="merge_ln1",
    )(xp, xs, p, p, p, y_raw, bonus, g, o_cv, o_mem, gn_g, gn_b, bd, w_o, ln_g, ln_b, w_router_t)


ROUTE_TILE = 256


def _routing_kernel(lt_ref, bias_ref, tri_ref, idx_o, w_o, pos_o, cnt_o, carry):
    i = pl.program_id(0)
    tile = lt_ref.shape[1]

    @pl.when(i == 0)
    def _():
        carry[...] = jnp.zeros_like(carry)

    neg_inf = -jnp.inf
    scores = _sigmoid(lt_ref[...])
    choice = scores + bias_ref[...]
    row = lax.broadcasted_iota(jnp.int32, (N_EXPERTS, tile), 0)
    rowf = row.astype(F32)
    grpf = (row // GROUP_SIZE).astype(F32)

    def group_allreduce(x, op):
        for s in (1, 2, 4):
            up = pltpu.roll(x, N_EXPERTS - s, 0)
            dn = pltpu.roll(x, s, 0)
            x = op(x, jnp.where((row & s) == 0, up, dn))
        return x

    m1 = group_allreduce(choice, jnp.maximum)
    first = group_allreduce(jnp.where(choice == m1, rowf, float(N_EXPERTS)), jnp.minimum)
    m2 = group_allreduce(jnp.where(rowf == first, neg_inf, choice), jnp.maximum)
    gscore = m1 + m2

    gsel = jnp.zeros_like(choice)
    for _ in range(TOPK_GROUPS):
        gmax = jnp.max(gscore, axis=0, keepdims=True)
        pick = jnp.min(jnp.where(gscore == gmax, grpf, float(N_GROUPS)), axis=0, keepdims=True)
        hit = grpf == pick
        gsel = jnp.where(hit, 1.0, gsel)
        gscore = jnp.where(hit, neg_inf, gscore)

    masked = jnp.where(gsel > 0.0, choice, neg_inf)
    row8 = lax.broadcasted_iota(jnp.int32, (TOP_K, tile), 0)
    idx_acc = jnp.zeros((TOP_K, tile), F32)
    w_acc = jnp.zeros((TOP_K, tile), F32)
    sel_all = jnp.zeros_like(choice)
    for kk in range(TOP_K):
        mx = jnp.max(masked, axis=0, keepdims=True)
        pick = jnp.min(jnp.where(masked == mx, rowf, float(N_EXPERTS)), axis=0, keepdims=True)
        hit = rowf == pick
        wk = jnp.sum(jnp.where(hit, scores, 0.0), axis=0, keepdims=True)
        idx_acc = jnp.where(row8 == kk, pick, idx_acc)
        w_acc = jnp.where(row8 == kk, wk, w_acc)
        sel_all = jnp.where(hit, 1.0, sel_all)
        masked = jnp.where(hit, neg_inf, masked)

    w_sum = jnp.sum(w_acc, axis=0, keepdims=True)
    w_o[...] = w_acc / w_sum * ROUTED_SCALE
    idx_o[...] = idx_acc.astype(jnp.int32)

    prefix = lax.dot_general(sel_all.astype(BF16), tri_ref[...], _NN, preferred_element_type=F32) + carry[...]
    pos_acc = jnp.zeros((TOP_K, tile), F32)
    for kk in range(TOP_K):
        hit = rowf == idx_acc[kk:kk + 1, :]
        pk = jnp.sum(jnp.where(hit, prefix, 0.0), axis=0, keepdims=True)
        pos_acc = jnp.where(row8 == kk, pk, pos_acc)
    pos_o[...] = pos_acc.astype(jnp.int32)
    carry[...] = carry[...] + jnp.sum(sel_all, axis=1, keepdims=True)
    cnt_o[...] = carry[...]


def _routing(logits_t, bias_col, tri):
    n = logits_t.shape[1]
    tile = ROUTE_TILE
    tok = pl.BlockSpec((TOP_K, tile), lambda i: (0, i))
    return pl.pallas_call(
        _routing_kernel,
        grid=(n // tile,),
        in_specs=[pl.BlockSpec((N_EXPERTS, tile), lambda i: (0, i)),
                  pl.BlockSpec((N_EXPERTS, 1), lambda i: (0, 0)),
                  pl.BlockSpec((tile, tile), lambda i: (0, 0))],
        out_specs=[tok, tok, tok, pl.BlockSpec((N_EXPERTS, 1), lambda i: (0, 0))],
        out_shape=[jax.ShapeDtypeStruct((TOP_K, n), jnp.int32), jax.ShapeDtypeStruct((TOP_K, n), F32),
                   jax.ShapeDtypeStruct((TOP_K, n), jnp.int32), jax.ShapeDtypeStruct((N_EXPERTS, 1), F32)],
        scratch_shapes=[pltpu.VMEM((N_EXPERTS, 1), F32)],
        compiler_params=_cparams(("arbitrary",)),
        name="routing",
    )(logits_t, bias_col, tri)


ZERO_ROWS = 128
PACKED = D // 2


def _pack_rows(h):
    hi = pltpu.bitcast(h[:, :PACKED].astype(BF16).astype(F32), jnp.uint32)
    lo = pltpu.bitcast(h[:, PACKED:].astype(BF16).astype(F32), jnp.uint32)
    return hi | (lo >> 16)


def _unpack_halves(w):
    return pltpu.bitcast(w & jnp.uint32(0xFFFF0000), F32), pltpu.bitcast(w << 16, F32)


def _unpack_rows(w):
    first, second = _unpack_halves(w)
    return jnp.concatenate([first.astype(BF16), second.astype(BF16)], axis=1)


def _dispatch_kernel(zs_ref, zc_ref, dest_ref, h_ref, xb_out, dest_smem, zbuf, packed, sem, idx_sem, zsem):
    i = pl.program_id(0)
    rows = h_ref.shape[0]

    @pl.when(i == 0)
    def _():
        zbuf[...] = jnp.zeros_like(zbuf)

        def zero_copy(piece):
            dst0 = pl.multiple_of(piece * ZERO_ROWS, ZERO_ROWS)
            return pltpu.make_async_copy(zbuf, xb_out.at[pl.ds(dst0, ZERO_ROWS), :], zsem)

        def per_range(e, c):
            def issue_piece(j, c2):
                zero_copy(zs_ref[e] + j).start()
                return c2
            lax.fori_loop(0, zc_ref[e], issue_piece, 0)
            return c

        def per_range_wait(e, c):
            def wait_piece(j, c2):
                zero_copy(zs_ref[e] + j).wait()
                return c2
            lax.fori_loop(0, zc_ref[e], wait_piece, 0)
            return c

        lax.fori_loop(0, N_EXPERTS + 1, per_range, 0)
        lax.fori_loop(0, N_EXPERTS + 1, per_range_wait, 0)

    cp = pltpu.make_async_copy(dest_ref, dest_smem, idx_sem)
    cp.start()
    packed[...] = _pack_rows(h_ref[...])
    cp.wait()

    def row_copy(t, k):
        return pltpu.make_async_copy(packed.at[pl.ds(t, 1), :],
                                     xb_out.at[pl.ds(dest_smem[k, t], 1), :], sem)

    def issue(t, c):
        for k in range(TOP_K):
            row_copy(t, k).start()
        return c

    def drain(t, c):
        for k in range(TOP_K):
            row_copy(t, k).wait()
        return c

    lax.fori_loop(0, rows, issue, 0, unroll=4)
    lax.fori_loop(0, rows, drain, 0)


def _dispatch(zero_start, zero_count, dest_t, h, n_rows):
    n = h.shape[0]
    grid_spec = pltpu.PrefetchScalarGridSpec(
        num_scalar_prefetch=2,
        grid=(n // TR,),
        in_specs=[pl.BlockSpec((TOP_K, TR), lambda i, zs, zc: (0, i)),
                  pl.BlockSpec((TR, D), lambda i, zs, zc: (i, 0))],
        out_specs=pl.BlockSpec(memory_space=pl.ANY),
        scratch_shapes=[pltpu.SMEM((TOP_K, TR), jnp.int32), pltpu.VMEM((ZERO_ROWS, PACKED), jnp.uint32),
                        pltpu.VMEM((TR, PACKED), jnp.uint32),
                        pltpu.SemaphoreType.DMA, pltpu.SemaphoreType.DMA, pltpu.SemaphoreType.DMA],
    )
    return pl.pallas_call(
        _dispatch_kernel,
        grid_spec=grid_spec,
        out_shape=jax.ShapeDtypeStruct((n_rows, PACKED), jnp.uint32),
        compiler_params=_cparams(("arbitrary",)),
        name="moe_dispatch",
    )(zero_start, zero_count, dest_t, h)


def _silu(x):
    return x * _sigmoid(x)


def _expert_kernel(be_ref, nu_ref, first_ref, slot_ref, next_ref, x_ref, wu_hbm, wd_hbm, o_ref,
                   wu_f32, wd_f32, wu_bf, wd_bf, sem_u, sem_d):
    b = pl.program_id(0)

    def weight_copies(e, slot):
        return (pltpu.make_async_copy(wu_hbm.at[e], wu_f32.at[slot], sem_u.at[slot]),
                pltpu.make_async_copy(wd_hbm.at[e], wd_f32.at[slot], sem_d.at[slot]))

    @pl.when(b == 0)
    def _():
        for cp in weight_copies(be_ref[0], 0):
            cp.start()

    @pl.when(first_ref[b] == 1)
    def _():
        slot = slot_ref[b]
        for cp in weight_copies(be_ref[b], slot):
            cp.wait()
        wu_bf[...] = wu_f32[slot].astype(BF16)
        wd_bf[...] = wd_f32[slot].astype(BF16)

        @pl.when(next_ref[b] >= 0)
        def _():
            for cp in weight_copies(next_ref[b], 1 - slot):
                cp.start()

    @pl.when(b < nu_ref[0])
    def _():
        up = _dot(_unpack_rows(x_ref[...]), wu_bf[...])
        act = _silu(up[:, :EXPERT_FF]) * up[:, EXPERT_FF:]
        o_ref[...] = _pack_rows(_dot(act, wd_bf[...]))

    @pl.when(b >= nu_ref[0])
    def _():
        o_ref[...] = jnp.zeros_like(o_ref)


def _experts(block_e, n_used, xb, w_up, w_down):
    rows = xb.shape[0]
    nb = rows // EXPERT_BM

    bidx = jnp.arange(nb, dtype=jnp.int32)
    prev_e = jnp.concatenate([jnp.full((1,), -1, jnp.int32), block_e[:-1]])
    first = jnp.logical_and(bidx < n_used[0], block_e != prev_e)
    slot = (jnp.cumsum(first.astype(jnp.int32)) - 1) % 2
    first_pos = jnp.where(first, bidx, nb)
    next_first = jnp.concatenate([jnp.flip(lax.cummin(jnp.flip(first_pos)))[1:], jnp.full((1,), nb, jnp.int32)])
    next_e = jnp.where(next_first < nb, block_e[jnp.minimum(next_first, nb - 1)], -1)

    def xmap(b, be, nu, fi, sl, ne):
        return (jnp.minimum(b, nu[0] - 1), 0)

    grid_spec = pltpu.PrefetchScalarGridSpec(
        num_scalar_prefetch=5,
        grid=(nb,),
        in_specs=[pl.BlockSpec((EXPERT_BM, PACKED), xmap),
                  pl.BlockSpec(memory_space=pl.ANY),
                  pl.BlockSpec(memory_space=pl.ANY)],
        out_specs=pl.BlockSpec((EXPERT_BM, PACKED), lambda b, be, nu, fi, sl, ne: (b, 0)),
        scratch_shapes=[pltpu.VMEM((2, D, 2 * EXPERT_FF), F32), pltpu.VMEM((2, EXPERT_FF, D), F32),
                        pltpu.VMEM((D, 2 * EXPERT_FF), BF16), pltpu.VMEM((EXPERT_FF, D), BF16),
                        pltpu.SemaphoreType.DMA((2,)), pltpu.SemaphoreType.DMA((2,))],
    )
    return pl.pallas_call(
        _expert_kernel,
        grid_spec=grid_spec,
        out_shape=jax.ShapeDtypeStruct((rows, PACKED), jnp.uint32),
        compiler_params=_cparams(("arbitrary",)),
        name="moe_experts",
    )(block_e, n_used, first.astype(jnp.int32), slot.astype(jnp.int32), next_e.astype(jnp.int32), xb, w_up, w_down)


SHARED_TILE = 512


def _shared_kernel(h_ref, wu_ref, wd_ref, o_ref):
    up = _dot(h_ref[...], wu_ref[...])
    act = _silu(up[:, :SHARED_FF]) * up[:, SHARED_FF:]
    o_ref[...] = _dot(act, wd_ref[...])


def _shared_ffn(h, w_up, w_down):
    n = h.shape[0]
    row = pl.BlockSpec((SHARED_TILE, D), lambda i: (i, 0))
    return pl.pallas_call(
        _shared_kernel,
        grid=(n // SHARED_TILE,),
        in_specs=[row, pl.BlockSpec((D, 2 * SHARED_FF), lambda i: (0, 0)),
                  pl.BlockSpec((SHARED_FF, D), lambda i: (0, 0))],
        out_specs=row,
        out_shape=jax.ShapeDtypeStruct((n, D), F32),
        compiler_params=_cparams(("parallel",)),
        name="shared_ffn",
    )(h, w_up, w_down)


def _combine_kernel(n_prompt_tiles, dest_ref, w_ref, h_ref, sh_ref, yb_ref, l2g_ref, l2b_ref, yp_o, ys_o,
                    buf, dest_smem, sem, idx_sem):
    i = pl.program_id(0)
    rows = h_ref.shape[0]
    cp = pltpu.make_async_copy(dest_ref, dest_smem, idx_sem)
    cp.start()
    cp.wait()

    def row_copy(t, k):
        return pltpu.make_async_copy(yb_ref.at[pl.ds(dest_smem[k, t], 1), :],
                                     buf.at[k, pl.ds(t, 1), :], sem)

    def issue(t, c):
        for k in range(TOP_K):
            row_copy(t, k).start()
        return c

    def drain(t, c):
        for k in range(TOP_K):
            row_copy(t, k).wait()
        return c

    lax.fori_loop(0, rows, issue, 0, unroll=4)
    lax.fori_loop(0, rows, drain, 0)
    w = w_ref[...]
    sh = sh_ref[...]
    f_first, f_second = sh[:, :PACKED], sh[:, PACKED:]
    for k in range(TOP_K):
        y_first, y_second = _unpack_halves(buf[k])
        f_first = f_first + w[:, k:k + 1] * y_first
        f_second = f_second + w[:, k:k + 1] * y_second
    z = ALPHA * h_ref[...] + jnp.concatenate([f_first, f_second], axis=1)
    y = _layer_norm(z, l2g_ref[...], l2b_ref[...])

    @pl.when(i < n_prompt_tiles)
    def _():
        yp_o[...] = y

    @pl.when(i >= n_prompt_tiles)
    def _():
        ys_o[...] = y


def _combine_ln2(dest_t, w_tok, h, shared, yb, ln_g, ln_b, n_prompt):
    n = h.shape[0]
    n_prompt_tiles = n_prompt // TR
    row = pl.BlockSpec((TR, D), lambda i: (i, 0))
    vec = pl.BlockSpec((1, D), lambda i: (0, 0))
    return pl.pallas_call(
        functools.partial(_combine_kernel, n_prompt_tiles),
        grid=(n // TR,),
        in_specs=[pl.BlockSpec((TOP_K, TR), lambda i: (0, i)),
                  pl.BlockSpec((TR, TOP_K), lambda i: (i, 0)),
                  row, row, pl.BlockSpec(memory_space=pl.ANY), vec, vec],
        out_specs=[pl.BlockSpec((TR, D), lambda i: (jnp.minimum(i, n_prompt_tiles - 1), 0)),
                   pl.BlockSpec((TR, D), lambda i: (jnp.maximum(i - n_prompt_tiles, 0), 0))],
        out_shape=[jax.ShapeDtypeStruct((n_prompt, D), F32), jax.ShapeDtypeStruct((n - n_prompt, D), F32)],
        scratch_shapes=[pltpu.VMEM((TOP_K, TR, PACKED), jnp.uint32), pltpu.SMEM((TOP_K, TR), jnp.int32),
                        pltpu.SemaphoreType.DMA, pltpu.SemaphoreType.DMA],
        compiler_params=_cparams(("arbitrary",)),
        name="moe_combine_ln2",
    )(dest_t, w_tok, h, shared, yb, ln_g, ln_b)


def _rw_cols_split(v):
    pad = jnp.zeros(v.shape[:-1] + (LORA_PAD - LORA_W,), v.dtype)
    lora = jnp.concatenate([v[..., 3 * D:3 * D + LORA_W], pad,
                            v[..., 3 * D + LORA_W:3 * D + LORA_W + LORA_A], pad,
                            v[..., 3 * D + LORA_W + LORA_A:]], axis=-1)
    return v[..., 0:3 * D], lora


def _pad_rows(w, rows):
    return jnp.concatenate([w, jnp.zeros((rows - w.shape[0],) + w.shape[1:], w.dtype)], axis=0)


def kernel(x_prompt, x_sample, mem_prompt, state_rwkv, state_shift, state_conv, cache_mem_k, cache_mem_v,
           w_in, mu_shift, rw_w0, rw_w2, rw_a0, rw_a2, rw_g2, rw_k_k, rw_k_a, rw_r_k, rw_gn_g, rw_gn_b,
           conv_w, w_conv_out, w_mem_k, w_mem_v, w_mem_o, w_o, ln1_g, ln1_b, w_router, router_bias,
           w_exp_up, w_exp_down, w_sh_up, w_sh_down, ln2_g, ln2_b):
    n_prompt = x_prompt.shape[0] * x_prompt.shape[1]
    n_seq_s, seq_s = x_sample.shape[0], x_sample.shape[1]
    n_sample = n_seq_s * seq_s
    n = n_prompt + n_sample
    assert x_prompt.shape[0] == 1 and seq_s == SEQ_S and n_prompt % TR == 0 and n_sample % TR == 0
    assert n % SHARED_TILE == 0 and n % ROUTE_TILE == 0 and w_in.shape[0] == 1
    assert n_prompt % MERGE_TILE == 0 and n_sample % MERGE_TILE == 0
    assert n_prompt % PROJ_TM == 0 and n_sample % PROJ_TM == 0

    xp = x_prompt.reshape(n_prompt, D)
    xs = x_sample.reshape(n_sample, D)

    def vec(v):
        return v.reshape(1, -1).astype(F32)

    w_in_r = _wt_relayout(w_in[0].T)
    p = _in_proj(xp, xs, w_in_r)

    w_kv = jnp.concatenate([w_mem_k[0], w_mem_v[0]], axis=1).astype(BF16)
    kv = _matmul(mem_prompt[0], w_kv, N_MEM, 512, "mem_kv")
    mem_k_p, mem_v_p = kv[:, :MEM_DIM], kv[:, MEM_DIM:]

    mu_rkv, mu_lora = _rw_cols_split(vec(mu_shift[0]))
    sh_rkv, sh_lora = _rw_cols_split(state_shift[0, :, 0, :])
    bnd_rkv = jnp.repeat(sh_rkv, seq_s, axis=0)
    bnd_lora = jnp.repeat(sh_lora, seq_s, axis=0)
    hi = lax.broadcasted_iota(jnp.int32, (GROUP_LANES, GROUP_LANES), 0) // HEAD
    hj = lax.broadcasted_iota(jnp.int32, (GROUP_LANES, GROUP_LANES), 1) // HEAD
    bd = (hi == hj).astype(BF16)
    r, k, v, kk, b, lw, g, bonus = _rwkv_prep(
        p, bnd_rkv, bnd_lora, mu_rkv, mu_lora, vec(rw_w0[0]), vec(rw_a0[0]), vec(rw_k_k[0]), vec(rw_k_a[0]),
        vec(rw_r_k[0]), _pad_rows(rw_w2[0], LORA_PAD).astype(BF16), _pad_rows(rw_a2[0], LORA_PAD).astype(BF16),
        rw_g2[0].astype(BF16), bd, n_prompt)

    s_sample = jnp.transpose(state_rwkv[0], (0, 2, 1, 3)).reshape(n_seq_s, HEAD, D)
    s_in = jnp.concatenate([jnp.zeros((STATE_SLOTS, HEAD, D), F32), s_sample.astype(F32)], axis=0)
    y_raw, s_out = _rwkv_scan(r, k, v, kk, b, lw, s_in, n_prompt)

    bnd1 = jnp.repeat(state_conv[0, :, 1, :], seq_s, axis=0)
    bnd2 = jnp.repeat(state_conv[0, :, 0, :], seq_s, axis=0)
    o_cv, u = _short_conv(p, bnd1, bnd2, _pad_rows(conv_w[0], 8), w_conv_out[0].astype(BF16), n_prompt)

    w_mem_o_b = w_mem_o[0].astype(BF16)
    o_mem_s = _mem_sample(p, cache_mem_k[0].reshape(n_seq_s, N_MEM, MEM_DIM),
                          cache_mem_v[0].reshape(n_seq_s, N_MEM, MEM_DIM), w_mem_o_b, n_prompt, n_seq_s)
    o_mem = _mem_attention(p, mem_k_p[None], mem_v_p[None], w_mem_o_b, o_mem_s, n_prompt)

    h, logits_t = _merge_ln1(xp, xs, p, y_raw, bonus, g, o_cv, o_mem, vec(rw_gn_g[0]), vec(rw_gn_b[0]), bd,
                             w_o[0].astype(BF16), vec(ln1_g[0]), vec(ln1_b[0]), w_router[0].T)

    ti = lax.broadcasted_iota(jnp.int32, (ROUTE_TILE, ROUTE_TILE), 0)
    tj = lax.broadcasted_iota(jnp.int32, (ROUTE_TILE, ROUTE_TILE), 1)
    tri = (ti < tj).astype(BF16)
    idx_t, w_t, pos_t, counts = _routing(logits_t, router_bias[0].reshape(N_EXPERTS, 1).astype(F32), tri)

    counts = counts[:, 0].astype(jnp.int32)
    padded = (counts + EXPERT_BM - 1) // EXPERT_BM * EXPERT_BM
    seg_end = jnp.cumsum(padded)
    seg_start = seg_end - padded
    expert_ids = jnp.arange(N_EXPERTS, dtype=jnp.int32)
    dest_t = pos_t + jnp.sum(
        jnp.where(idx_t[None] == expert_ids[:, None, None], seg_start[:, None, None], 0), axis=0)
    nb = (n * TOP_K) // EXPERT_BM + N_EXPERTS
    block_rows = jnp.arange(nb, dtype=jnp.int32) * EXPERT_BM
    block_e = jnp.minimum(jnp.sum((seg_end[None, :] <= block_rows[:, None]).astype(jnp.int32), axis=1),
                          N_EXPERTS - 1)
    n_used = (seg_end[-1:] // EXPERT_BM).astype(jnp.int32)

    pieces_per_block = EXPERT_BM // ZERO_ROWS
    valid_last = counts - (padded - EXPERT_BM)
    first_piece = valid_last // ZERO_ROWS
    zero_start = jnp.where(padded > 0, (seg_end - EXPERT_BM) // ZERO_ROWS + first_piece, 0)
    zero_count = jnp.where(padded > 0, pieces_per_block - first_piece, 0)
    total_pieces = nb * pieces_per_block
    zero_start = jnp.concatenate([zero_start, seg_end[-1:] // ZERO_ROWS]).astype(jnp.int32)
    zero_count = jnp.concatenate([zero_count, total_pieces - seg_end[-1:] // ZERO_ROWS]).astype(jnp.int32)

    xb = _dispatch(zero_start, zero_count, dest_t, h, nb * EXPERT_BM)
    yb = _experts(block_e, n_used, xb, w_exp_up[0], w_exp_down[0])
    shared = _shared_ffn(h, w_sh_up[0].astype(BF16), w_sh_down[0].astype(BF16))
    y_p, y_s = _combine_ln2(dest_t, w_t.T, h, shared, yb, vec(ln2_g[0]), vec(ln2_b[0]), n_prompt)

    dt = x_prompt.dtype
    y_p = y_p.reshape(x_prompt.shape)
    y_s = y_s.reshape(x_sample.shape)

    def state_out(s):
        q = s.reshape(s.shape[0], HEAD, N_HEADS, HEAD)
        return jnp.transpose(q, (0, 2, 1, 3))[None].astype(dt)

    rw_p = state_out(s_out[0:1])
    rw_s = state_out(s_out[STATE_SLOTS:])

    last_rows = jnp.concatenate([jnp.array([n_prompt - 1], jnp.int32),
                                 n_prompt + seq_s - 1 + seq_s * jnp.arange(n_seq_s, dtype=jnp.int32)])
    p_last = p[last_rows]
    shift = jnp.concatenate([p_last[:, 0:3 * D],
                             p_last[:, COL_LORA:COL_LORA + LORA_W],
                             p_last[:, COL_LORA + LORA_PAD:COL_LORA + LORA_PAD + LORA_A],
                             p_last[:, COL_LORA + 2 * LORA_PAD:]], axis=1)
    sh_p = shift[0:1].reshape(1, 1, 1, RW_COLS)
    sh_s = shift[1:].reshape(1, n_seq_s, 1, RW_COLS)

    cv_p = u[n_prompt - 2:n_prompt].reshape(1, 1, 2, CONV_DIM)
    cv_s = u[n_prompt:].reshape(n_seq_s, seq_s, CONV_DIM)[:, seq_s - 2:, :][None]

    mk_p = mem_k_p.reshape(1, 1, N_MEM, MEM_HEADS, MEM_HEAD_DIM)
    mv_p = mem_v_p.reshape(1, 1, N_MEM, MEM_HEADS, MEM_HEAD_DIM)
    return (y_p, y_s, rw_p, sh_p, cv_p, mk_p, mv_p, rw_s, sh_s, cv_s)
```

```python
import functools

import jax
import jax.numpy as jnp
from jax import lax
from jax.experimental import pallas as pl
from jax.experimental.pallas import tpu as pltpu

F32 = jnp.float32
BF16 = jnp.bfloat16
ACT = jnp.bfloat16

D = 2048
HEAD = 64
N_HEADS = D // HEAD
LORA_W = 96
LORA_A = 96
LORA_G = 256
DECAY_SCALE = 0.6065306597126334
GN_EPS = HEAD * 1e-5
CONV_DIM = D // 2
N_MEM = 256
MEM_HEADS = 4
MEM_HEAD_DIM = 256
MEM_DIM = MEM_HEADS * MEM_HEAD_DIM
N_EXPERTS = 64
N_GROUPS = 8
GROUP_SIZE = N_EXPERTS // N_GROUPS
TOPK_GROUPS = 4
TOP_K = 8
EXPERT_FF = 512
SHARED_FF = 512
ROUTED_SCALE = 2.5
LN_EPS = 1e-5
DEPTH = 1
ALPHA = (2 * DEPTH) ** 0.25
RW_COLS = 3 * D + LORA_W + LORA_A + LORA_G

LORA_PAD = 128
LORA_COLS = 2 * LORA_PAD + LORA_G
COL_RKV = 0
COL_GATE = 3 * D
COL_CONV = 6 * D
COL_Q = COL_CONV + 3 * CONV_DIM
COL_LORA = COL_Q + MEM_DIM
P_COLS = COL_LORA + LORA_COLS

CHUNK = 16
GROUP_HEADS = 4
GROUP_LANES = GROUP_HEADS * HEAD
N_LANE_GROUPS = D // GROUP_LANES
STACK = GROUP_HEADS * CHUNK
SEQ_S = 16
STATE_SLOTS = 8

TR = 128
SCAN_ROWS = STATE_SLOTS * CHUNK
EXPERT_BM = 256
VMEM_LIMIT = 56 * 1024 * 1024


def _cparams(sem):
    return pltpu.CompilerParams(dimension_semantics=sem, vmem_limit_bytes=VMEM_LIMIT)


def _sigmoid(x):
    return 1.0 / (1.0 + jnp.exp(-x))


def _dot(a, b, dims=(((1,), (0,)), ((), ()))):
    return lax.dot_general(a.astype(BF16), b.astype(BF16), dims, preferred_element_type=F32)


_NN = (((1,), (0,)), ((), ()))
_NT = (((1,), (1,)), ((), ()))
_TN = (((0,), (0,)), ((), ()))


def _split2(x):
    hi = x.astype(BF16)
    lo = (x - hi.astype(F32)).astype(BF16)
    return hi, lo


def _split3(x):
    hi = x.astype(BF16)
    r1 = x - hi.astype(F32)
    mid = r1.astype(BF16)
    lo = (r1 - mid.astype(F32)).astype(BF16)
    return hi, mid, lo


def _dot3(a, b, dims=_NN):
    ah, al = _split2(a)
    bh, bl = _split2(b)
    f = functools.partial(lax.dot_general, dimension_numbers=dims, preferred_element_type=F32)
    return f(ah, bh) + (f(ah, bl) + f(al, bh))


def _dot_exact_rhs(a, b_bf16, dims=_NN):
    hi, mid, lo = _split3(a)
    f = functools.partial(lax.dot_general, dimension_numbers=dims, preferred_element_type=F32)
    return f(hi, b_bf16) + (f(mid, b_bf16) + f(lo, b_bf16))


def _dot_exact_lhs(a_bf16, b):
    hi, mid, lo = _split3(b)
    f = functools.partial(lax.dot_general, dimension_numbers=_NN, preferred_element_type=F32)
    return f(a_bf16, hi) + (f(a_bf16, mid) + f(a_bf16, lo))


_sdot = _dot


def _mm_kernel(x_ref, w_ref, o_ref):
    o_ref[...] = _dot(x_ref[...], w_ref[...]).astype(o_ref.dtype)


def _matmul(x, w, tm, tn, name):
    m, k = x.shape
    n = w.shape[1]
    return pl.pallas_call(
        _mm_kernel,
        grid=(m // tm, n // tn),
        in_specs=[pl.BlockSpec((tm, k), lambda i, j: (i, 0)),
                  pl.BlockSpec((k, tn), lambda i, j: (0, j))],
        out_specs=pl.BlockSpec((tm, tn), lambda i, j: (i, j)),
        out_shape=jax.ShapeDtypeStruct((m, n), F32),
        compiler_params=_cparams(("parallel", "arbitrary")),
        name=name,
    )(x, w)


WT_BLOCK = 512
WT_PIECE = 128


def _wt_relayout_kernel(src_ref, wt_hbm, o_ref, buf, sems):
    step = pl.program_id(0)
    n_plain = pl.num_programs(0) - 1
    n_pieces = WT_BLOCK // WT_PIECE

    def emit(slot):
        for s in range(n_pieces):
            rows = slice(s * WT_PIECE, (s + 1) * WT_PIECE)
            o_ref[:, rows] = buf[slot, rows, :].T.astype(BF16)

    def plain_copies(p, slot):
        row0 = pl.multiple_of(src_ref[p], 8)
        return [pltpu.make_async_copy(wt_hbm.at[pl.ds(row0 + s * WT_PIECE, WT_PIECE), :],
                                      buf.at[slot, pl.ds(s * WT_PIECE, WT_PIECE), :], sems.at[slot, s])
                for s in range(n_pieces)]

    @pl.when(step == 0)
    def _():
        for cp in plain_copies(0, 0):
            cp.start()
        lo_w = 3 * D
        lo_a = lo_w + LORA_W
        lo_g = lo_a + LORA_A
        pieces = ((lo_w, 0, LORA_W), (lo_a, LORA_PAD, LORA_A), (lo_g, 2 * LORA_PAD, LORA_G))
        for _, dst, width in pieces[:2]:
            buf[1, dst + width:dst + LORA_PAD, :] = jnp.zeros((LORA_PAD - width, buf.shape[2]), F32)
        copies = [pltpu.make_async_copy(wt_hbm.at[pl.ds(src, width), :], buf.at[1, pl.ds(dst, width), :],
                                        sems.at[1, n])
                  for n, (src, dst, width) in enumerate(pieces)]
        for cp in copies:
            cp.start()
        for cp in copies:
            cp.wait()
        emit(1)

    @pl.when(step > 0)
    def _():
        p = step - 1
        slot = p % 2

        @pl.when(p + 1 < n_plain)
        def _():
            for cp in plain_copies(p + 1, 1 - slot):
                cp.start()

        for cp in plain_copies(p, slot):
            cp.wait()
        emit(slot)


def _wt_relayout(wt):
    k = wt.shape[1]
    rw_end = RW_COLS
    cv_end = rw_end + 3 * CONV_DIM
    q_end = cv_end + MEM_DIM
    src = []
    for dst0, src0, width in ((COL_RKV, 0, 3 * D), (COL_GATE, q_end, 3 * D), (COL_CONV, rw_end, 3 * CONV_DIM),
                              (COL_Q, cv_end, MEM_DIM)):
        assert dst0 == len(src) * WT_BLOCK and width % WT_BLOCK == 0
        src += [src0 + b * WT_BLOCK for b in range(width // WT_BLOCK)]
    assert len(src) * WT_BLOCK == COL_LORA and LORA_COLS == WT_BLOCK
    grid_spec = pltpu.PrefetchScalarGridSpec(
        num_scalar_prefetch=1,
        grid=(len(src) + 1,),
        in_specs=[pl.BlockSpec(memory_space=pl.ANY)],
        out_specs=pl.BlockSpec((k, WT_BLOCK), lambda j, src_rows: (0, (j + len(src)) % (len(src) + 1))),
        scratch_shapes=[pltpu.VMEM((2, WT_BLOCK, k), F32), pltpu.SemaphoreType.DMA((2, WT_BLOCK // WT_PIECE))],
    )
    return pl.pallas_call(
        _wt_relayout_kernel,
        grid_spec=grid_spec,
        out_shape=jax.ShapeDtypeStruct((k, P_COLS), BF16),
        compiler_params=_cparams(("arbitrary",)),
        name="w_in_relayout",
    )(jnp.asarray(src, jnp.int32), wt)


PROJ_TM = 512
PROJ_TN = 1536


def _in_proj_kernel(n_prompt_tiles, xp_ref, xs_ref, w_ref, o_ref):
    i = pl.program_id(1)
    x = jnp.where(i < n_prompt_tiles, xp_ref[...], xs_ref[...]).astype(BF16)
    o_ref[...] = jnp.dot(x, w_ref[...], preferred_element_type=F32)


def _in_proj(xp, xs, w):
    k = xp.shape[1]
    n_prompt_tiles = xp.shape[0] // PROJ_TM
    n_tiles = n_prompt_tiles + xs.shape[0] // PROJ_TM
    ncols = w.shape[1]
    return pl.pallas_call(
        functools.partial(_in_proj_kernel, n_prompt_tiles),
        grid=(ncols // PROJ_TN, n_tiles),
        in_specs=[pl.BlockSpec((PROJ_TM, k), lambda j, i: (jnp.minimum(i, n_prompt_tiles - 1), 0)),
                  pl.BlockSpec((PROJ_TM, k), lambda j, i: (jnp.maximum(i - n_prompt_tiles, 0), 0)),
                  pl.BlockSpec((k, PROJ_TN), lambda j, i: (0, j))],
        out_specs=pl.BlockSpec((PROJ_TM, PROJ_TN), lambda j, i: (i, j)),
        out_shape=jax.ShapeDtypeStruct((n_tiles * PROJ_TM, ncols), F32),
        compiler_params=_cparams(("arbitrary", "arbitrary")),
        name="in_proj",
    )(xp, xs, w)


def _head_sum(x, bd):
    parts = []
    for g in range(N_LANE_GROUPS):
        parts.append(_dot_exact_rhs(x[:, g * GROUP_LANES:(g + 1) * GROUP_LANES], bd))
    return jnp.concatenate(parts, axis=1)


def _prep_kernel(n_prompt_tiles, rkv_ref, lora_ref, c_rkv_ref, c_lora_ref, b_rkv_ref, b_lora_ref,
                 mu_rkv_ref, mu_lora_ref, w0_ref, a0_ref, kk_ref, ka_ref, rk_ref,
                 w2_ref, a2_ref, g2_ref, bd_ref,
                 r_o, k_o, v_o, kk_o, b_o, lw_o, g_o, bonus_o):
    i = pl.program_id(0)
    rows = rkv_ref.shape[0]
    row = lax.broadcasted_iota(jnp.int32, (rows, 1), 0)
    is_sample = i >= n_prompt_tiles
    seq_start = jnp.logical_and(is_sample, (row % SEQ_S) == 0)

    def mixed(x, carry_row, bnd, mu):
        prev = pltpu.roll(x, 1, 0)
        carry_row = jnp.where(i == 0, 0.0, carry_row)
        prev = jnp.where(row == 0, carry_row, prev)
        prev = jnp.where(seq_start, bnd, prev)
        return x + (prev - x) * mu

    def section(s):
        sl = slice(s * D, (s + 1) * D)
        return mixed(rkv_ref[:, sl], c_rkv_ref[7:8, sl], b_rkv_ref[:, sl], mu_rkv_ref[:, sl])

    lo = mixed(lora_ref[...], c_lora_ref[7:8, :], b_lora_ref[...], mu_lora_ref[...])
    w_lo = lo[:, 0:LORA_PAD]
    a_lo = lo[:, LORA_PAD:2 * LORA_PAD]
    g_lo = lo[:, 2 * LORA_PAD:]
    log_w = -DECAY_SCALE * _sigmoid(w0_ref[...] + _dot(jnp.tanh(w_lo), w2_ref[...]))
    a = _sigmoid(a0_ref[...] + _dot(a_lo, a2_ref[...]))
    g_o[...] = _dot(_sigmoid(g_lo), g2_ref[...]).astype(g_o.dtype)
    lw_o[...] = log_w

    bd = bd_ref[...]
    k = section(1)
    kk = k * kk_ref[...]
    ss = _head_sum(kk * kk, bd)
    kk = kk * lax.rsqrt(jnp.maximum(ss, 1e-24))
    kk_o[...] = kk
    b_o[...] = kk * a
    k = k * (1.0 + (a - 1.0) * ka_ref[...])
    k_o[...] = k
    r = section(0)
    r_o[...] = r
    v = section(2)
    v_o[...] = v.astype(v_o.dtype)
    bonus_o[...] = (_head_sum(r * k * rk_ref[...], bd) * v).astype(bonus_o.dtype)


def _rwkv_prep(p, bnd_rkv, bnd_lora, mu_rkv, mu_lora, w0, a0, k_k, k_a, r_k, w2p, a2p, g2, bd, n_prompt):
    n = p.shape[0]
    n_prompt_tiles = n_prompt // TR
    carry_blk = TR // 8
    lora_blk = COL_LORA // LORA_COLS

    def row_spec(cols, cb=0):
        return pl.BlockSpec((TR, cols), lambda i: (i, cb))

    def carry_spec(cols, cb=0):
        return pl.BlockSpec((8, cols), lambda i: (jnp.maximum(i * carry_blk - 1, 0), cb))

    def bnd_spec(cols):
        return pl.BlockSpec((TR, cols), lambda i: (jnp.maximum(i - n_prompt_tiles, 0), 0))

    def const_spec(shape):
        return pl.BlockSpec(shape, lambda i: (0,) * len(shape))

    out = jax.ShapeDtypeStruct((n, D), F32)
    out_act = jax.ShapeDtypeStruct((n, D), ACT)
    return pl.pallas_call(
        functools.partial(_prep_kernel, n_prompt_tiles),
        grid=(n // TR,),
        in_specs=[row_spec(3 * D), row_spec(LORA_COLS, lora_blk),
                  carry_spec(3 * D), carry_spec(LORA_COLS, lora_blk),
                  bnd_spec(3 * D), bnd_spec(LORA_COLS),
                  const_spec((1, 3 * D)), const_spec((1, LORA_COLS)),
                  const_spec((1, D)), const_spec((1, D)), const_spec((1, D)), const_spec((1, D)),
                  const_spec((1, D)),
                  const_spec((LORA_PAD, D)), const_spec((LORA_PAD, D)), const_spec((LORA_G, D)),
                  const_spec((GROUP_LANES, GROUP_LANES))],
        out_specs=[row_spec(D)] * 8,
        out_shape=[out, out, out_act, out, out, out, out_act, out_act],
        compiler_params=_cparams(("arbitrary",)),
        name="rwkv_prep",
    )(p, p, p, p, bnd_rkv, bnd_lora, mu_rkv, mu_lora, w0, a0, k_k, k_a, r_k, w2p, a2p, g2, bd)


def _scan_kernel(n_prompt_tiles, r_ref, k_ref, v_ref, kk_ref, b_ref, lw_ref, s_in_ref, y_ref, s_out_ref, s_scr):
    i = pl.program_id(0)
    is_sample = i >= n_prompt_tiles
    n_chunks = r_ref.shape[0] // CHUNK

    lane = lax.broadcasted_iota(jnp.int32, (1, GROUP_LANES), 1)
    head_masks = [(lane // HEAD == h).astype(F32) for h in range(GROUP_HEADS)]
    ri = lax.broadcasted_iota(jnp.int32, (STACK, 2 * STACK), 0)
    ci = lax.broadcasted_iota(jnp.int32, (STACK, 2 * STACK), 1)
    same_head = (ri // CHUNK) == ((ci % STACK) // CHUNK)
    strict_lower = jnp.logical_and(same_head, (ci % CHUNK) < (ri % CHUNK))
    mask_incl = jnp.logical_and(same_head, (ci % CHUNK) <= (ri % CHUNK)).astype(F32)
    mask_strict_b = jnp.logical_and(strict_lower, ci < STACK).astype(F32)
    mask_strict_k = jnp.logical_and(strict_lower, ci >= STACK).astype(F32)
    eye = (ri == ci).astype(F32)
    trow = lax.broadcasted_iota(jnp.int32, (CHUNK, 1), 0)

    def running_sum(x):
        d = 1
        while d < CHUNK:
            x = x + jnp.where(trow >= d, pltpu.roll(x, d, 0), 0.0)
            d *= 2
        return x

    rb = lax.broadcasted_iota(jnp.int32, (GROUP_LANES, GROUP_LANES), 0)
    cb = lax.broadcasted_iota(jnp.int32, (GROUP_LANES, GROUP_LANES), 1)
    block_diag = ((rb // HEAD) == (cb // HEAD)).astype(F32)

    def stack(x):
        return jnp.concatenate([x * m for m in head_masks], axis=0)

    def unstack(x):
        out = x[0:CHUNK]
        for h in range(1, GROUP_HEADS):
            out = out + x[h * CHUNK:(h + 1) * CHUNK]
        return out

    def compact(s):
        out = s[0:HEAD]
        for h in range(1, GROUP_HEADS):
            out = out + s[h * HEAD:(h + 1) * HEAD]
        return out

    @pl.when(i == 0)
    def _():
        s_out_ref[...] = jnp.zeros_like(s_out_ref)

    groups = range(N_LANE_GROUPS)
    lanes = [slice(g * GROUP_LANES, (g + 1) * GROUP_LANES) for g in groups]

    def twice(x):
        return jnp.concatenate([x, x], axis=0)

    def a0(c):
        rows = slice(c * CHUNK, (c + 1) * CHUNK)
        st = {"rows": rows}
        st["lw"] = [lw_ref[rows, lanes[g]] for g in groups]
        st["cum"] = [running_sum(st["lw"][g]) for g in groups]
        return st

    def a1(st):
        rows = st["rows"]
        lhs_s, bk_s, v_s, kr_t, p_end = [], [], [], [], []
        for g in groups:
            cum, lw = st["cum"][g], st["lw"][g]
            e_incl = jnp.exp(cum)
            e_excl = jnp.exp(cum - lw)
            e_neg = jnp.exp(-cum)
            p_end.append(e_incl[CHUNK - 1:CHUNK, :])
            r_t = r_ref[rows, lanes[g]] * e_incl
            kk_t = kk_ref[rows, lanes[g]] * e_excl
            b_t = b_ref[rows, lanes[g]] * e_neg
            k_t = k_ref[rows, lanes[g]] * e_neg
            kr_t.append(jnp.concatenate([kk_t, r_t], axis=0))
            lhs_s.append(jnp.concatenate([stack(kk_t), stack(r_t)], axis=0))
            bk_s.append(jnp.concatenate([stack(b_t), stack(k_t)], axis=0))
            v_s.append(stack(v_ref[rows, lanes[g]]))
        st.update(bk_s=bk_s, v_s=v_s, kr_t=kr_t, p_end=p_end)
        st["mn"] = [_sdot(lhs_s[g], bk_s[g], _NT) for g in groups]

    def a2(st):
        mn = st.pop("mn")
        st["m1"] = [mn[g][0:STACK] * mask_strict_b for g in groups]
        m_k = [mn[g][0:STACK] * mask_strict_k for g in groups]
        st["n_bk"] = [mn[g][STACK:] * mask_incl for g in groups]
        st["m2"] = [_sdot(st["m1"][g], twice(st["m1"][g])) for g in groups]
        st["mv"] = [_sdot(m_k[g], twice(st["v_s"][g])) for g in groups]

    def a3(st):
        st["m4"] = [_sdot(st["m2"][g], twice(st["m2"][g])) for g in groups]
        st["t_inv"] = [_sdot(eye - st["m1"][g], twice(eye + st["m2"][g])) for g in groups]

    def a4(st):
        st["m8"] = [_sdot(st["m4"][g], twice(st["m4"][g])) for g in groups]
        st["t_inv"] = [_sdot(st["t_inv"][g], twice(eye + st["m4"][g])) for g in groups]

    def a5(st):
        st["t_inv"] = [_sdot(st["t_inv"][g], twice(eye + st["m8"][g])) for g in groups]

    def b1(c, st, s_prev):
        load_state = is_sample if c > 0 else jnp.logical_or(is_sample, i == 0)
        slot = jnp.where(is_sample, c, 0)
        s0 = []
        for g in groups:
            s_loaded = jnp.concatenate([s_in_ref[slot, :, lanes[g]]] * GROUP_HEADS, axis=0) * block_diag
            s0.append(jnp.where(load_state, s_loaded, s_prev[g]))
        st["s0"] = s0
        st["gr"] = [_sdot(st["kr_t"][g], s0[g], _NT) for g in groups]

    def b2(st):
        u_s = [-_sdot(st["t_inv"][g], twice(stack(st["gr"][g][0:CHUNK]) + st["mv"][g])) for g in groups]
        st["uv"] = [jnp.concatenate([u_s[g], st["v_s"][g]], axis=0) for g in groups]

    def b3(c, st):
        slot = jnp.where(is_sample, c, 0)
        for g in groups:
            y = st["gr"][g][CHUNK:] + unstack(_sdot(st["n_bk"][g], st["uv"][g]))
            y_ref[st["rows"], lanes[g]] = y.astype(y_ref.dtype)
        s_new = []
        for g in groups:
            s_new.append(st["s0"][g] * st["p_end"][g] + _sdot(st["uv"][g], st["bk_s"][g] * st["p_end"][g], _TN))
            s_out_ref[slot, :, lanes[g]] = compact(s_new[g])
        return s_new

    s_cur = [s_scr[g] for g in groups]
    cur = a0(0)
    for lvl in (a1, a2, a3, a4, a5):
        lvl(cur)
    for c in range(n_chunks):
        last = c == n_chunks - 1
        nxt = None if last else a0(c + 1)
        b1(c, cur, s_cur)
        if not last:
            a1(nxt)
        b2(cur)
        if not last:
            a2(nxt)
        s_cur = b3(c, cur)
        if not last:
            for lvl in (a3, a4, a5):
                lvl(nxt)
        cur = nxt
    for g in groups:
        s_scr[g] = s_cur[g]


def _rwkv_scan(r, k, v, kk, b, lw, s_in, n_prompt):
    n = r.shape[0]
    n_prompt_tiles = n_prompt // SCAN_ROWS
    row_spec = pl.BlockSpec((SCAN_ROWS, D), lambda i: (i, 0))
    state_spec = pl.BlockSpec((STATE_SLOTS, HEAD, D),
                              lambda i: (jnp.maximum(i - n_prompt_tiles + 1, 0), 0, 0))
    return pl.pallas_call(
        functools.partial(_scan_kernel, n_prompt_tiles),
        grid=(n // SCAN_ROWS,),
        in_specs=[row_spec] * 6 + [state_spec],
        out_specs=[row_spec, state_spec],
        out_shape=[jax.ShapeDtypeStruct((n, D), ACT), jax.ShapeDtypeStruct(s_in.shape, F32)],
        scratch_shapes=[pltpu.VMEM((N_LANE_GROUPS, GROUP_LANES, GROUP_LANES), F32)],
        compiler_params=_cparams(("arbitrary",)),
        name="rwkv_scan",
    )(r, k, v, kk, b, lw, s_in)


def _conv_kernel(n_prompt_tiles, cb_ref, cc_ref, ch_ref, ccc_ref, cch_ref, bnd1_ref, bnd2_ref,
                 cw_ref, wout_ref, o_ref, u_ref):
    i = pl.program_id(0)
    rows = cb_ref.shape[0]
    row = lax.broadcasted_iota(jnp.int32, (rows, 1), 0)
    is_sample = i >= n_prompt_tiles
    pos = row % SEQ_S
    u = cc_ref[...] * ch_ref[...]
    u_ref[...] = u
    u_prev = jnp.where(i == 0, 0.0, ccc_ref[...] * cch_ref[...])
    prev1 = pltpu.roll(u, 1, 0)
    prev1 = jnp.where(row == 0, u_prev[7:8, :], prev1)
    prev2 = pltpu.roll(u, 2, 0)
    prev2 = jnp.where(row == 0, u_prev[6:7, :], prev2)
    prev2 = jnp.where(row == 1, u_prev[7:8, :], prev2)
    bnd1 = bnd1_ref[...]
    prev1 = jnp.where(jnp.logical_and(is_sample, pos == 0), bnd1, prev1)
    prev2 = jnp.where(jnp.logical_and(is_sample, pos == 0), bnd2_ref[...], prev2)
    prev2 = jnp.where(jnp.logical_and(is_sample, pos == 1), bnd1, prev2)
    cw = cw_ref[...]
    conv = prev2 * cw[0:1, :] + prev1 * cw[1:2, :] + u * cw[2:3, :]
    o_ref[...] = _dot(cb_ref[...] * conv, wout_ref[...]).astype(o_ref.dtype)


def _short_conv(p, bnd1, bnd2, conv_w, w_out, n_prompt):
    n = p.shape[0]
    n_prompt_tiles = n_prompt // TR
    cblk = COL_CONV // CONV_DIM
    carry_blk = TR // 8

    def row_spec(cb):
        return pl.BlockSpec((TR, CONV_DIM), lambda i: (i, cb))

    def carry_spec(cb):
        return pl.BlockSpec((8, CONV_DIM), lambda i: (jnp.maximum(i * carry_blk - 1, 0), cb))

    bnd_spec = pl.BlockSpec((TR, CONV_DIM), lambda i: (jnp.maximum(i - n_prompt_tiles, 0), 0))
    return pl.pallas_call(
        functools.partial(_conv_kernel, n_prompt_tiles),
        grid=(n // TR,),
        in_specs=[row_spec(cblk), row_spec(cblk + 1), row_spec(cblk + 2),
                  carry_spec(cblk + 1), carry_spec(cblk + 2), bnd_spec, bnd_spec,
                  pl.BlockSpec((8, CONV_DIM), lambda i: (0, 0)),
                  pl.BlockSpec((CONV_DIM, D), lambda i: (0, 0))],
        out_specs=[pl.BlockSpec((TR, D), lambda i: (i, 0)), pl.BlockSpec((TR, CONV_DIM), lambda i: (i, 0))],
        out_shape=[jax.ShapeDtypeStruct((n, D), ACT), jax.ShapeDtypeStruct((n, CONV_DIM), F32)],
        compiler_params=_cparams(("arbitrary",)),
        name="short_conv",
    )(p, p, p, p, p, bnd1, bnd2, conv_w, w_out)


def _mem_kernel(q_ref, k_ref, v_ref, wo_ref, o_ref):
    n_seq = k_ref.shape[0]
    rows = q_ref.shape[0] // n_seq
    per_seq = []
    for s_i in range(n_seq):
        q = q_ref[s_i * rows:(s_i + 1) * rows, :]
        k = k_ref[s_i]
        v = v_ref[s_i]
        outs = []
        for h in range(MEM_HEADS):
            sl = slice(h * MEM_HEAD_DIM, (h + 1) * MEM_HEAD_DIM)
            s = _dot(q[:, sl], k[:, sl], _NT) * (MEM_HEAD_DIM ** -0.5)
            s = s - jnp.max(s, axis=-1, keepdims=True)
            e = jnp.exp(s)
            pr = e / jnp.sum(e, axis=-1, keepdims=True)
            outs.append(_dot(pr, v[:, sl]))
        per_seq.append(jnp.concatenate(outs, axis=1))
    o_ref[...] = _dot(jnp.concatenate(per_seq, axis=0), wo_ref[...]).astype(o_ref.dtype)


MEM_SEQS = 4


def _mem_sample(p, mem_k, mem_v, w_o, row_start, n_seq):
    qblk = COL_Q // MEM_DIM
    rows = MEM_SEQS * SEQ_S
    rb0 = row_start // rows
    return pl.pallas_call(
        _mem_kernel,
        grid=(n_seq // MEM_SEQS,),
        in_specs=[pl.BlockSpec((rows, MEM_DIM), lambda i: (rb0 + i, qblk)),
                  pl.BlockSpec((MEM_SEQS, N_MEM, MEM_DIM), lambda i: (i, 0, 0)),
                  pl.BlockSpec((MEM_SEQS, N_MEM, MEM_DIM), lambda i: (i, 0, 0)),
                  pl.BlockSpec((MEM_DIM, D), lambda i: (0, 0))],
        out_specs=pl.BlockSpec((rows, D), lambda i: (i, 0)),
        out_shape=jax.ShapeDtypeStruct((n_seq * SEQ_S, D), ACT),
        compiler_params=_cparams(("arbitrary",)),
        name="mem_attention_sample",
    )(p, mem_k, mem_v, w_o)


MEM_TILE = 256


def _mem_prompt_kernel(n_prompt_tiles, q_ref, k_ref, v_ref, wo_ref, tail_ref, o_ref):
    i = pl.program_id(0)

    @pl.when(i < n_prompt_tiles)
    def _():
        _mem_kernel(q_ref, k_ref, v_ref, wo_ref, o_ref)

    @pl.when(i >= n_prompt_tiles)
    def _():
        o_ref[...] = tail_ref[...]


def _mem_attention(p, mem_k, mem_v, w_o, o_sample, n_prompt):
    n = p.shape[0]
    qblk = COL_Q // MEM_DIM
    n_prompt_tiles = n_prompt // MEM_TILE
    return pl.pallas_call(
        functools.partial(_mem_prompt_kernel, n_prompt_tiles),
        grid=(n // MEM_TILE,),
        in_specs=[pl.BlockSpec((MEM_TILE, MEM_DIM), lambda i: (jnp.minimum(i, n_prompt_tiles - 1), qblk)),
                  pl.BlockSpec((1, N_MEM, MEM_DIM), lambda i: (0, 0, 0)),
                  pl.BlockSpec((1, N_MEM, MEM_DIM), lambda i: (0, 0, 0)),
                  pl.BlockSpec((MEM_DIM, D), lambda i: (0, 0)),
                  pl.BlockSpec((MEM_TILE, D), lambda i: (jnp.maximum(i - n_prompt_tiles, 0), 0))],
        out_specs=pl.BlockSpec((MEM_TILE, D), lambda i: (i, 0)),
        out_shape=jax.ShapeDtypeStruct((n, D), ACT),
        compiler_params=_cparams(("arbitrary",)),
        name="mem_attention",
    )(p, mem_k, mem_v, w_o, o_sample)


def _layer_norm(z, g, b):
    mu = jnp.mean(z, axis=-1, keepdims=True)
    d = z - mu
    var = jnp.mean(d * d, axis=-1, keepdims=True)
    return d * lax.rsqrt(var + LN_EPS) * g + b


def _merge_kernel(n_prompt_tiles, xp_ref, xs_ref, ga_ref, gb_ref, gm_ref, y_ref, bonus_ref, g_ref, ocv_ref, omem_ref,
                  gng_ref, gnb_ref, bd_ref, wo_ref, l1g_ref, l1b_ref, wr_ref, h_o, lt_o):
    x = jnp.where(pl.program_id(0) < n_prompt_tiles, xp_ref[...], xs_ref[...])
    bd = bd_ref[...]
    y = y_ref[...].astype(F32)
    mean = _head_sum(y, bd) * (1.0 / HEAD)
    d = y - mean
    var = _head_sum(d * d, bd) * (1.0 / HEAD)
    yn = d * lax.rsqrt(var + GN_EPS) * gng_ref[...] + gnb_ref[...]
    o_rw = (yn + bonus_ref[...].astype(F32)) * g_ref[...].astype(F32)
    merged = (_sigmoid(ga_ref[...]) * o_rw + _sigmoid(gb_ref[...]) * ocv_ref[...].astype(F32)
              + _sigmoid(gm_ref[...]) * omem_ref[...].astype(F32))
    z = ALPHA * x + _dot(merged, wo_ref[...])
    h = _layer_norm(z, l1g_ref[...], l1b_ref[...])
    h_o[...] = h
    lt_o[...] = _dot3(wr_ref[...], h, _NT)


MERGE_TILE = 256


def _merge_ln1(xp, xs, p, y_raw, bonus, g, o_cv, o_mem, gn_g, gn_b, bd, w_o, ln_g, ln_b, w_router_t):
    n = p.shape[0]
    tile = MERGE_TILE
    n_prompt_tiles = xp.shape[0] // tile
    gblk = COL_GATE // D
    row = pl.BlockSpec((tile, D), lambda i: (i, 0))
    xp_spec = pl.BlockSpec((tile, D), lambda i: (jnp.minimum(i, n_prompt_tiles - 1), 0))
    xs_spec = pl.BlockSpec((tile, D), lambda i: (jnp.maximum(i - n_prompt_tiles, 0), 0))

    def gate_spec(j):
        return pl.BlockSpec((tile, D), lambda i: (i, gblk + j))

    def const_spec(shape):
        return pl.BlockSpec(shape, lambda i: (0,) * len(shape))

    vec = const_spec((1, D))
    return pl.pallas_call(
        functools.partial(_merge_kernel, n_prompt_tiles),
        grid=(n // tile,),
        in_specs=[xp_spec, xs_spec, gate_spec(0), gate_spec(1), gate_spec(2), row, row, row, row, row,
                  vec, vec, const_spec((GROUP_LANES, GROUP_LANES)), const_spec((D, D)), vec, vec,
                  const_spec((N_EXPERTS, D))],
        out_specs=[row, pl.BlockSpec((N_EXPERTS, tile), lambda i: (0, i))],
        out_shape=[jax.ShapeDtypeStruct((n, D), F32), jax.ShapeDtypeStruct((N_EXPERTS, n), F32)],
        compiler_params=_cparams(("arbitrary",)),
        name="merge_ln1",
    )(xp, xs, p, p, p, y_raw, bonus, g, o_cv, o_mem, gn_g, gn_b, bd, w_o, ln_g, ln_b, w_router_t)


ROUTE_TILE = 256


def _routing_kernel(lt_ref, bias_ref, tri_ref, idx_o, w_o, pos_o, cnt_o, carry):
    i = pl.program_id(0)
    tile = lt_ref.shape[1]

    @pl.when(i == 0)
    def _():
        carry[...] = jnp.zeros_like(carry)

    neg_inf = -jnp.inf
    scores = _sigmoid(lt_ref[...])
    choice = scores + bias_ref[...]
    row = lax.broadcasted_iota(jnp.int32, (N_EXPERTS, tile), 0)
    rowf = row.astype(F32)
    grpf = (row // GROUP_SIZE).astype(F32)

    def group_allreduce(x, op):
        for s in (1, 2, 4):
            up = pltpu.roll(x, N_EXPERTS - s, 0)
            dn = pltpu.roll(x, s, 0)
            x = op(x, jnp.where((row & s) == 0, up, dn))
        return x

    m1 = group_allreduce(choice, jnp.maximum)
    first = group_allreduce(jnp.where(choice == m1, rowf, float(N_EXPERTS)), jnp.minimum)
    m2 = group_allreduce(jnp.where(rowf == first, neg_inf, choice), jnp.maximum)
    gscore = m1 + m2

    gsel = jnp.zeros_like(choice)
    for _ in range(TOPK_GROUPS):
        gmax = jnp.max(gscore, axis=0, keepdims=True)
        pick = jnp.min(jnp.where(gscore == gmax, grpf, float(N_GROUPS)), axis=0, keepdims=True)
        hit = grpf == pick
        gsel = jnp.where(hit, 1.0, gsel)
        gscore = jnp.where(hit, neg_inf, gscore)

    masked = jnp.where(gsel > 0.0, choice, neg_inf)
    row8 = lax.broadcasted_iota(jnp.int32, (TOP_K, tile), 0)
    idx_acc = jnp.zeros((TOP_K, tile), F32)
    w_acc = jnp.zeros((TOP_K, tile), F32)
    sel_all = jnp.zeros_like(choice)
    for kk in range(TOP_K):
        mx = jnp.max(masked, axis=0, keepdims=True)
        pick = jnp.min(jnp.where(masked == mx, rowf, float(N_EXPERTS)), axis=0, keepdims=True)
        hit = rowf == pick
        wk = jnp.sum(jnp.where(hit, scores, 0.0), axis=0, keepdims=True)
        idx_acc = jnp.where(row8 == kk, pick, idx_acc)
        w_acc = jnp.where(row8 == kk, wk, w_acc)
        sel_all = jnp.where(hit, 1.0, sel_all)
        masked = jnp.where(hit, neg_inf, masked)

    w_sum = jnp.sum(w_acc, axis=0, keepdims=True)
    w_o[...] = w_acc / w_sum * ROUTED_SCALE
    idx_o[...] = idx_acc.astype(jnp.int32)

    prefix = lax.dot_general(sel_all.astype(BF16), tri_ref[...], _NN, preferred_element_type=F32) + carry[...]
    pos_acc = jnp.zeros((TOP_K, tile), F32)
    for kk in range(TOP_K):
        hit = rowf == idx_acc[kk:kk + 1, :]
        pk = jnp.sum(jnp.where(hit, prefix, 0.0), axis=0, keepdims=True)
        pos_acc = jnp.where(row8 == kk, pk, pos_acc)
    pos_o[...] = pos_acc.astype(jnp.int32)
    carry[...] = carry[...] + jnp.sum(sel_all, axis=1, keepdims=True)
    cnt_o[...] = carry[...]


def _routing(logits_t, bias_col, tri):
    n = logits_t.shape[1]
    tile = ROUTE_TILE
    tok = pl.BlockSpec((TOP_K, tile), lambda i: (0, i))
    return pl.pallas_call(
        _routing_kernel,
        grid=(n // tile,),
        in_specs=[pl.BlockSpec((N_EXPERTS, tile), lambda i: (0, i)),
                  pl.BlockSpec((N_EXPERTS, 1), lambda i: (0, 0)),
                  pl.BlockSpec((tile, tile), lambda i: (0, 0))],
        out_specs=[tok, tok, tok, pl.BlockSpec((N_EXPERTS, 1), lambda i: (0, 0))],
        out_shape=[jax.ShapeDtypeStruct((TOP_K, n), jnp.int32), jax.ShapeDtypeStruct((TOP_K, n), F32),
                   jax.ShapeDtypeStruct((TOP_K, n), jnp.int32), jax.ShapeDtypeStruct((N_EXPERTS, 1), F32)],
        scratch_shapes=[pltpu.VMEM((N_EXPERTS, 1), F32)],
        compiler_params=_cparams(("arbitrary",)),
        name="routing",
    )(logits_t, bias_col, tri)


ZERO_ROWS = 128
PACKED = D // 2


def _pack_rows(h):
    hi = pltpu.bitcast(h[:, :PACKED].astype(BF16).astype(F32), jnp.uint32)
    lo = pltpu.bitcast(h[:, PACKED:].astype(BF16).astype(F32), jnp.uint32)
    return hi | (lo >> 16)


def _unpack_halves(w):
    return pltpu.bitcast(w & jnp.uint32(0xFFFF0000), F32), pltpu.bitcast(w << 16, F32)


def _unpack_rows(w):
    first, second = _unpack_halves(w)
    return jnp.concatenate([first.astype(BF16), second.astype(BF16)], axis=1)


def _dispatch_kernel(zs_ref, zc_ref, dest_ref, h_ref, xb_out, dest_smem, zbuf, packed, sem, idx_sem, zsem):
    i = pl.program_id(0)
    rows = h_ref.shape[0]

    @pl.when(i == 0)
    def _():
        zbuf[...] = jnp.zeros_like(zbuf)

        def zero_copy(piece):
            dst0 = pl.multiple_of(piece * ZERO_ROWS, ZERO_ROWS)
            return pltpu.make_async_copy(zbuf, xb_out.at[pl.ds(dst0, ZERO_ROWS), :], zsem)

        def per_range(e, c):
            def issue_piece(j, c2):
                zero_copy(zs_ref[e] + j).start()
                return c2
            lax.fori_loop(0, zc_ref[e], issue_piece, 0)
            return c

        def per_range_wait(e, c):
            def wait_piece(j, c2):
                zero_copy(zs_ref[e] + j).wait()
                return c2
            lax.fori_loop(0, zc_ref[e], wait_piece, 0)
            return c

        lax.fori_loop(0, N_EXPERTS + 1, per_range, 0)
        lax.fori_loop(0, N_EXPERTS + 1, per_range_wait, 0)

    cp = pltpu.make_async_copy(dest_ref, dest_smem, idx_sem)
    cp.start()
    packed[...] = _pack_rows(h_ref[...])
    cp.wait()

    def row_copy(t, k):
        return pltpu.make_async_copy(packed.at[pl.ds(t, 1), :],
                                     xb_out.at[pl.ds(dest_smem[k, t], 1), :], sem)

    def issue(t, c):
        for k in range(TOP_K):
            row_copy(t, k).start()
        return c

    def drain(t, c):
        for k in range(TOP_K):
            row_copy(t, k).wait()
        return c

    lax.fori_loop(0, rows, issue, 0, unroll=4)
    lax.fori_loop(0, rows, drain, 0)


def _dispatch(zero_start, zero_count, dest_t, h, n_rows):
    n = h.shape[0]
    grid_spec = pltpu.PrefetchScalarGridSpec(
        num_scalar_prefetch=2,
        grid=(n // TR,),
        in_specs=[pl.BlockSpec((TOP_K, TR), lambda i, zs, zc: (0, i)),
                  pl.BlockSpec((TR, D), lambda i, zs, zc: (i, 0))],
        out_specs=pl.BlockSpec(memory_space=pl.ANY),
        scratch_shapes=[pltpu.SMEM((TOP_K, TR), jnp.int32), pltpu.VMEM((ZERO_ROWS, PACKED), jnp.uint32),
                        pltpu.VMEM((TR, PACKED), jnp.uint32),
                        pltpu.SemaphoreType.DMA, pltpu.SemaphoreType.DMA, pltpu.SemaphoreType.DMA],
    )
    return pl.pallas_call(
        _dispatch_kernel,
        grid_spec=grid_spec,
        out_shape=jax.ShapeDtypeStruct((n_rows, PACKED), jnp.uint32),
        compiler_params=_cparams(("arbitrary",)),
        name="moe_dispatch",
    )(zero_start, zero_count, dest_t, h)


def _silu(x):
    return x * _sigmoid(x)


def _expert_kernel(be_ref, nu_ref, first_ref, slot_ref, next_ref, x_ref, wu_hbm, wd_hbm, o_ref,
                   wu_f32, wd_f32, wu_bf, wd_bf, sem_u, sem_d):
    b = pl.program_id(0)

    def weight_copies(e, slot):
        return (pltpu.make_async_copy(wu_hbm.at[e], wu_f32.at[slot], sem_u.at[slot]),
                pltpu.make_async_copy(wd_hbm.at[e], wd_f32.at[slot], sem_d.at[slot]))

    @pl.when(b == 0)
    def _():
        for cp in weight_copies(be_ref[0], 0):
            cp.start()

    @pl.when(first_ref[b] == 1)
    def _():
        slot = slot_ref[b]
        for cp in weight_copies(be_ref[b], slot):
            cp.wait()
        wu_bf[...] = wu_f32[slot].astype(BF16)
        wd_bf[...] = wd_f32[slot].astype(BF16)

        @pl.when(next_ref[b] >= 0)
        def _():
            for cp in weight_copies(next_ref[b], 1 - slot):
                cp.start()

    @pl.when(b < nu_ref[0])
    def _():
        up = _dot(_unpack_rows(x_ref[...]), wu_bf[...])
        act = _silu(up[:, :EXPERT_FF]) * up[:, EXPERT_FF:]
        o_ref[...] = _pack_rows(_dot(act, wd_bf[...]))

    @pl.when(b >= nu_ref[0])
    def _():
        o_ref[...] = jnp.zeros_like(o_ref)


def _experts(block_e, n_used, xb, w_up, w_down):
    rows = xb.shape[0]
    nb = rows // EXPERT_BM

    bidx = jnp.arange(nb, dtype=jnp.int32)
    prev_e = jnp.concatenate([jnp.full((1,), -1, jnp.int32), block_e[:-1]])
    first = jnp.logical_and(bidx < n_used[0], block_e != prev_e)
    slot = (jnp.cumsum(first.astype(jnp.int32)) - 1) % 2
    first_pos = jnp.where(first, bidx, nb)
    next_first = jnp.concatenate([jnp.flip(lax.cummin(jnp.flip(first_pos)))[1:], jnp.full((1,), nb, jnp.int32)])
    next_e = jnp.where(next_first < nb, block_e[jnp.minimum(next_first, nb - 1)], -1)

    def xmap(b, be, nu, fi, sl, ne):
        return (jnp.minimum(b, nu[0] - 1), 0)

    grid_spec = pltpu.PrefetchScalarGridSpec(
        num_scalar_prefetch=5,
        grid=(nb,),
        in_specs=[pl.BlockSpec((EXPERT_BM, PACKED), xmap),
                  pl.BlockSpec(memory_space=pl.ANY),
                  pl.BlockSpec(memory_space=pl.ANY)],
        out_specs=pl.BlockSpec((EXPERT_BM, PACKED), lambda b, be, nu, fi, sl, ne: (b, 0)),
        scratch_shapes=[pltpu.VMEM((2, D, 2 * EXPERT_FF), F32), pltpu.VMEM((2, EXPERT_FF, D), F32),
                        pltpu.VMEM((D, 2 * EXPERT_FF), BF16), pltpu.VMEM((EXPERT_FF, D), BF16),
                        pltpu.SemaphoreType.DMA((2,)), pltpu.SemaphoreType.DMA((2,))],
    )
    return pl.pallas_call(
        _expert_kernel,
        grid_spec=grid_spec,
        out_shape=jax.ShapeDtypeStruct((rows, PACKED), jnp.uint32),
        compiler_params=_cparams(("arbitrary",)),
        name="moe_experts",
    )(block_e, n_used, first.astype(jnp.int32), slot.astype(jnp.int32), next_e.astype(jnp.int32), xb, w_up, w_down)


SHARED_TILE = 512


def _shared_kernel(h_ref, wu_ref, wd_ref, o_ref):
    up = _dot(h_ref[...], wu_ref[...])
    act = _silu(up[:, :SHARED_FF]) * up[:, SHARED_FF:]
    o_ref[...] = _dot(act, wd_ref[...])


def _shared_ffn(h, w_up, w_down):
    n = h.shape[0]
    row = pl.BlockSpec((SHARED_TILE, D), lambda i: (i, 0))
    return pl.pallas_call(
        _shared_kernel,
        grid=(n // SHARED_TILE,),
        in_specs=[row, pl.BlockSpec((D, 2 * SHARED_FF), lambda i: (0, 0)),
                  pl.BlockSpec((SHARED_FF, D), lambda i: (0, 0))],
        out_specs=row,
        out_shape=jax.ShapeDtypeStruct((n, D), F32),
        compiler_params=_cparams(("parallel",)),
        name="shared_ffn",
    )(h, w_up, w_down)


def _combine_kernel(n_prompt_tiles, dest_ref, w_ref, h_ref, sh_ref, yb_ref, l2g_ref, l2b_ref, yp_o, ys_o,
                    buf, dest_smem, sem, idx_sem):
    i = pl.program_id(0)
    rows = h_ref.shape[0]
    cp = pltpu.make_async_copy(dest_ref, dest_smem, idx_sem)
    cp.start()
    cp.wait()

    def row_copy(t, k):
        return pltpu.make_async_copy(yb_ref.at[pl.ds(dest_smem[k, t], 1), :],
                                     buf.at[k, pl.ds(t, 1), :], sem)

    def issue(t, c):
        for k in range(TOP_K):
            row_copy(t, k).start()
        return c

    def drain(t, c):
        for k in range(TOP_K):
            row_copy(t, k).wait()
        return c

    lax.fori_loop(0, rows, issue, 0, unroll=4)
    lax.fori_loop(0, rows, drain, 0)
    w = w_ref[...]
    sh = sh_ref[...]
    f_first, f_second = sh[:, :PACKED], sh[:, PACKED:]
    for k in range(TOP_K):
        y_first, y_second = _unpack_halves(buf[k])
        f_first = f_first + w[:, k:k + 1] * y_first
        f_second = f_second + w[:, k:k + 1] * y_second
    z = ALPHA * h_ref[...] + jnp.concatenate([f_first, f_second], axis=1)
    y = _layer_norm(z, l2g_ref[...], l2b_ref[...])

    @pl.when(i < n_prompt_tiles)
    def _():
        yp_o[...] = y

    @pl.when(i >= n_prompt_tiles)
    def _():
        ys_o[...] = y


def _combine_ln2(dest_t, w_tok, h, shared, yb, ln_g, ln_b, n_prompt):
    n = h.shape[0]
    n_prompt_tiles = n_prompt // TR
    row = pl.BlockSpec((TR, D), lambda i: (i, 0))
    vec = pl.BlockSpec((1, D), lambda i: (0, 0))
    return pl.pallas_call(
        functools.partial(_combine_kernel, n_prompt_tiles),
        grid=(n // TR,),
        in_specs=[pl.BlockSpec((TOP_K, TR), lambda i: (0, i)),
                  pl.BlockSpec((TR, TOP_K), lambda i: (i, 0)),
                  row, row, pl.BlockSpec(memory_space=pl.ANY), vec, vec],
        out_specs=[pl.BlockSpec((TR, D), lambda i: (jnp.minimum(i, n_prompt_tiles - 1), 0)),
                   pl.BlockSpec((TR, D), lambda i: (jnp.maximum(i - n_prompt_tiles, 0), 0))],
        out_shape=[jax.ShapeDtypeStruct((n_prompt, D), F32), jax.ShapeDtypeStruct((n - n_prompt, D), F32)],
        scratch_shapes=[pltpu.VMEM((TOP_K, TR, PACKED), jnp.uint32), pltpu.SMEM((TOP_K, TR), jnp.int32),
                        pltpu.SemaphoreType.DMA, pltpu.SemaphoreType.DMA],
        compiler_params=_cparams(("arbitrary",)),
        name="moe_combine_ln2",
    )(dest_t, w_tok, h, shared, yb, ln_g, ln_b)


def _rw_cols_split(v):
    pad = jnp.zeros(v.shape[:-1] + (LORA_PAD - LORA_W,), v.dtype)
    lora = jnp.concatenate([v[..., 3 * D:3 * D + LORA_W], pad,
                            v[..., 3 * D + LORA_W:3 * D + LORA_W + LORA_A], pad,
                            v[..., 3 * D + LORA_W + LORA_A:]], axis=-1)
    return v[..., 0:3 * D], lora


def _pad_rows(w, rows):
    return jnp.concatenate([w, jnp.zeros((rows - w.shape[0],) + w.shape[1:], w.dtype)], axis=0)


def kernel(x_prompt, x_sample, mem_prompt, state_rwkv, state_shift, state_conv, cache_mem_k, cache_mem_v,
           w_in, mu_shift, rw_w0, rw_w2, rw_a0, rw_a2, rw_g2, rw_k_k, rw_k_a, rw_r_k, rw_gn_g, rw_gn_b,
           conv_w, w_conv_out, w_mem_k, w_mem_v, w_mem_o, w_o, ln1_g, ln1_b, w_router, router_bias,
           w_exp_up, w_exp_down, w_sh_up, w_sh_down, ln2_g, ln2_b):
    n_prompt = x_prompt.shape[0] * x_prompt.shape[1]
    n_seq_s, seq_s = x_sample.shape[0], x_sample.shape[1]
    n_sample = n_seq_s * seq_s
    n = n_prompt + n_sample
    assert x_prompt.shape[0] == 1 and seq_s == SEQ_S and n_prompt % TR == 0 and n_sample % TR == 0
    assert n % SHARED_TILE == 0 and n % ROUTE_TILE == 0 and w_in.shape[0] == 1
    assert n_prompt % MERGE_TILE == 0 and n_sample % MERGE_TILE == 0
    assert n_prompt % PROJ_TM == 0 and n_sample % PROJ_TM == 0

    xp = x_prompt.reshape(n_prompt, D)
    xs = x_sample.reshape(n_sample, D)

    def vec(v):
        return v.reshape(1, -1).astype(F32)

    w_in_r = _wt_relayout(w_in[0].T)
    p = _in_proj(xp, xs, w_in_r)

    w_kv = jnp.concatenate([w_mem_k[0], w_mem_v[0]], axis=1).astype(BF16)
    kv = _matmul(mem_prompt[0], w_kv, N_MEM, 512, "mem_kv")
    mem_k_p, mem_v_p = kv[:, :MEM_DIM], kv[:, MEM_DIM:]

    mu_rkv, mu_lora = _rw_cols_split(vec(mu_shift[0]))
    sh_rkv, sh_lora = _rw_cols_split(state_shift[0, :, 0, :])
    bnd_rkv = jnp.repeat(sh_rkv, seq_s, axis=0)
    bnd_lora = jnp.repeat(sh_lora, seq_s, axis=0)
    hi = lax.broadcasted_iota(jnp.int32, (GROUP_LANES, GROUP_LANES), 0) // HEAD
    hj = lax.broadcasted_iota(jnp.int32, (GROUP_LANES, GROUP_LANES), 1) // HEAD
    bd = (hi == hj).astype(BF16)
    r, k, v, kk, b, lw, g, bonus = _rwkv_prep(
        p, bnd_rkv, bnd_lora, mu_rkv, mu_lora, vec(rw_w0[0]), vec(rw_a0[0]), vec(rw_k_k[0]), vec(rw_k_a[0]),
        vec(rw_r_k[0]), _pad_rows(rw_w2[0], LORA_PAD).astype(BF16), _pad_rows(rw_a2[0], LORA_PAD).astype(BF16),
        rw_g2[0].astype(BF16), bd, n_prompt)

    s_sample = jnp.transpose(state_rwkv[0], (0, 2, 1, 3)).reshape(n_seq_s, HEAD, D)
    s_in = jnp.concatenate([jnp.zeros((STATE_SLOTS, HEAD, D), F32), s_sample.astype(F32)], axis=0)
    y_raw, s_out = _rwkv_scan(r, k, v, kk, b, lw, s_in, n_prompt)

    bnd1 = jnp.repeat(state_conv[0, :, 1, :], seq_s, axis=0)
    bnd2 = jnp.repeat(state_conv[0, :, 0, :], seq_s, axis=0)
    o_cv, u = _short_conv(p, bnd1, bnd2, _pad_rows(conv_w[0], 8), w_conv_out[0].astype(BF16), n_prompt)

    w_mem_o_b = w_mem_o[0].astype(BF16)
    o_mem_s = _mem_sample(p, cache_mem_k[0].reshape(n_seq_s, N_MEM, MEM_DIM),
                          cache_mem_v[0].reshape(n_seq_s, N_MEM, MEM_DIM), w_mem_o_b, n_prompt, n_seq_s)
    o_mem = _mem_attention(p, mem_k_p[None], mem_v_p[None], w_mem_o_b, o_mem_s, n_prompt)

    h, logits_t = _merge_ln1(xp, xs, p, y_raw, bonus, g, o_cv, o_mem, vec(rw_gn_g[0]), vec(rw_gn_b[0]), bd,
                             w_o[0].astype(BF16), vec(ln1_g[0]), vec(ln1_b[0]), w_router[0].T)

    ti = lax.broadcasted_iota(jnp.int32, (ROUTE_TILE, ROUTE_TILE), 0)
    tj = lax.broadcasted_iota(jnp.int32, (ROUTE_TILE, ROUTE_TILE), 1)
    tri = (ti < tj).astype(BF16)
    idx_t, w_t, pos_t, counts = _routing(logits_t, router_bias[0].reshape(N_EXPERTS, 1).astype(F32), tri)

    counts = counts[:, 0].astype(jnp.int32)
    padded = (counts + EXPERT_BM - 1) // EXPERT_BM * EXPERT_BM
    seg_end = jnp.cumsum(padded)
    seg_start = seg_end - padded
    expert_ids = jnp.arange(N_EXPERTS, dtype=jnp.int32)
    dest_t = pos_t + jnp.sum(
        jnp.where(idx_t[None] == expert_ids[:, None, None], seg_start[:, None, None], 0), axis=0)
    nb = (n * TOP_K) // EXPERT_BM + N_EXPERTS
    block_rows = jnp.arange(nb, dtype=jnp.int32) * EXPERT_BM
    block_e = jnp.minimum(jnp.sum((seg_end[None, :] <= block_rows[:, None]).astype(jnp.int32), axis=1),
                          N_EXPERTS - 1)
    n_used = (seg_end[-1:] // EXPERT_BM).astype(jnp.int32)

    pieces_per_block = EXPERT_BM // ZERO_ROWS
    valid_last = counts - (padded - EXPERT_BM)
    first_piece = valid_last // ZERO_ROWS
    zero_start = jnp.where(padded > 0, (seg_end - EXPERT_BM) // ZERO_ROWS + first_piece, 0)
    zero_count = jnp.where(padded > 0, pieces_per_block - first_piece, 0)
    total_pieces = nb * pieces_per_block
    zero_start = jnp.concatenate([zero_start, seg_end[-1:] // ZERO_ROWS]).astype(jnp.int32)
    zero_count = jnp.concatenate([zero_count, total_pieces - seg_end[-1:] // ZERO_ROWS]).astype(jnp.int32)

    xb = _dispatch(zero_start, zero_count, dest_t, h, nb * EXPERT_BM)
    yb = _experts(block_e, n_used, xb, w_exp_up[0], w_exp_down[0])
    shared = _shared_ffn(h, w_sh_up[0].astype(BF16), w_sh_down[0].astype(BF16))
    y_p, y_s = _combine_ln2(dest_t, w_t.T, h, shared, yb, vec(ln2_g[0]), vec(ln2_b[0]), n_prompt)

    dt = x_prompt.dtype
    y_p = y_p.reshape(x_prompt.shape)
    y_s = y_s.reshape(x_sample.shape)

    def state_out(s):
        q = s.reshape(s.shape[0], HEAD, N_HEADS, HEAD)
        return jnp.transpose(q, (0, 2, 1, 3))[None].astype(dt)

    rw_p = state_out(s_out[0:1])
    rw_s = state_out(s_out[STATE_SLOTS:])

    last_rows = jnp.concatenate([jnp.array([n_prompt - 1], jnp.int32),
                                 n_prompt + seq_s - 1 + seq_s * jnp.arange(n_seq_s, dtype=jnp.int32)])
    p_last = p[last_rows]
    shift = jnp.concatenate([p_last[:, 0:3 * D],
                             p_last[:, COL_LORA:COL_LORA + LORA_W],
                             p_last[:, COL_LORA + LORA_PAD:COL_LORA + LORA_PAD + LORA_A],
                             p_last[:, COL_LORA + 2 * LORA_PAD:]], axis=1)
    sh_p = shift[0:1].reshape(1, 1, 1, RW_COLS)
    sh_s = shift[1:].reshape(1, n_seq_s, 1, RW_COLS)

    cv_p = u[n_prompt - 2:n_prompt].reshape(1, 1, 2, CONV_DIM)
    cv_s = u[n_prompt:].reshape(n_seq_s, seq_s, CONV_DIM)[:, seq_s - 2:, :][None]

    mk_p = mem_k_p.reshape(1, 1, N_MEM, MEM_HEADS, MEM_HEAD_DIM)
    mv_p = mem_v_p.reshape(1, 1, N_MEM, MEM_HEADS, MEM_HEAD_DIM)
    return (y_p, y_s, rw_p, sh_p, cv_p, mk_p, mv_p, rw_s, sh_s, cv_s)
```
